```python
import math
import jax, jax.numpy as jnp
from jax import lax
import numpy as np

D_MODEL = 1024
BATCH = 4
SEQ = 4096
DEPTH = 1
DEC_BATCH = 128
DEC_SEQ = 8
PAST_LEN = 8192
PAGE_SIZE = 128

HEAD_DIM = 64
NSA_WIDTH = D_MODEL // 2
GMLP_WIDTH = D_MODEL - NSA_WIDTH
NSA_HEADS = NSA_WIDTH // HEAD_DIM
NSA_KV_HEADS = 2
GQA_GROUP = NSA_HEADS // NSA_KV_HEADS
KV_DIM = NSA_KV_HEADS * HEAD_DIM
N_KV_STREAMS = 4
N_BRANCH = 3
CMP_BLOCK = 32
CMP_STRIDE = 16
CMP_HIDDEN = 128
SLC_BLOCK = 64
N_SELECT = 16
WINDOW = 512
QUERY_BLOCK = 128
FORCE_SCORE = 1e9
GMLP_GROUPS = GMLP_WIDTH // 64
GMLP_GROUP_DIM = GMLP_WIDTH // GMLP_GROUPS
CHUNK = 128
D_FF = 4 * D_MODEL
IN_COLS = NSA_WIDTH + N_KV_STREAMS * KV_DIM + 2 * KV_DIM + N_BRANCH * NSA_HEADS + 2 * GMLP_WIDTH
EPS = 1e-6

kernel_name = 'nsa_gmlp_hybrid_step'


def rmsnorm(x, g):
    xf = x.astype(jnp.float32)
    y = xf * lax.rsqrt(jnp.mean(xf * xf, axis=-1, keepdims=True) + EPS)
    return (y * g).astype(x.dtype)


def layernorm(x, g, b):
    xf = x.astype(jnp.float32)
    mu = jnp.mean(xf, axis=-1, keepdims=True)
    var = jnp.mean(jnp.square(xf - mu), axis=-1, keepdims=True)
    return ((xf - mu) * lax.rsqrt(var + EPS) * g + b).astype(x.dtype)


def masked_softmax(s, mask):
    s = jnp.where(mask, s.astype(jnp.float32), -jnp.inf)
    m = jnp.max(s, axis=-1, keepdims=True)
    m = jnp.where(jnp.isfinite(m), m, 0.0)
    p = jnp.exp(s - m)
    return p / jnp.maximum(jnp.sum(p, axis=-1, keepdims=True), 1e-30)


def project(x, ln1_g, w_in, ln_v_g, ln_v_b):
    B, S, _ = x.shape
    z = rmsnorm(x, ln1_g) @ w_in
    sizes = [NSA_WIDTH, N_KV_STREAMS * KV_DIM, 2 * KV_DIM, N_BRANCH * NSA_HEADS, 2 * GMLP_WIDTH]
    idx = [int(c) for c in np.cumsum(sizes)[:-1]]
    zq, zkv, zwin, zgate, zg = jnp.split(z, idx, axis=-1)
    q = zq.reshape(B, S, NSA_HEADS, HEAD_DIM)
    kv = zkv.reshape(B, S, N_KV_STREAMS, NSA_KV_HEADS, HEAD_DIM)
    wkv = zwin.reshape(B, S, 2, NSA_KV_HEADS, HEAD_DIM)
    gate_logits = zgate.reshape(B, S, N_BRANCH, NSA_HEADS)
    zg = jax.nn.gelu(zg)
    u = zg[..., :GMLP_WIDTH]
    v = layernorm(zg[..., GMLP_WIDTH:], ln_v_g, ln_v_b)
    return q, kv, wkv, gate_logits, u, v


def compress_blocks(k, w1, b1, w2, b2, pos):
    B, T, KVH, dh = k.shape
    R = CMP_BLOCK // CMP_STRIDE
    n_seg = -(-T // CMP_STRIDE)
    seg = jnp.pad(k, ((0, 0), (0, n_seg * CMP_STRIDE - T), (0, 0), (0, 0)))
    seg = seg.reshape(B, n_seg, CMP_STRIDE, KVH, dh)
    n_c = n_seg - R + 1
    w1r = w1.reshape(R, CMP_STRIDE, dh, CMP_HIDDEN)
    posr = pos.reshape(R, CMP_STRIDE, 1, dh)
    h = b1
    for r in range(R):
        h = h + jnp.einsum('bnskd,sdh->bnkh', seg[:, r:r + n_c] + posr[r], w1r[r])
    return jax.nn.gelu(h) @ w2 + b2


def cmp_to_slc_matrix(n_c, n_s):
    ci = jnp.arange(n_c)[:, None] * CMP_STRIDE
    sj = jnp.arange(n_s)[None, :] * SLC_BLOCK
    cover = jnp.clip(jnp.minimum(ci + CMP_BLOCK, sj + SLC_BLOCK) - jnp.maximum(ci, sj), 0, None)
    return cover.astype(jnp.float32) / CMP_BLOCK


def nsa_mixer(q, kv_full, win_ext, gate_logits, cmp_w1, cmp_b1, cmp_w2, cmp_b2, cmp_pos):
    B, Sq, H, dh = q.shape
    T = kv_full.shape[1]
    q_pos0 = T - Sq
    qg = q.reshape(B, Sq, NSA_KV_HEADS, GQA_GROUP, HEAD_DIM) * (HEAD_DIM ** -0.5)
    t = q_pos0 + jnp.arange(Sq)

    kc = compress_blocks(kv_full[:, :, 0], cmp_w1[0], cmp_b1[0], cmp_w2[0], cmp_b2[0], cmp_pos[0])
    vc = compress_blocks(kv_full[:, :, 1], cmp_w1[1], cmp_b1[1], cmp_w2[1], cmp_b2[1], cmp_pos[1])
    n_c = kc.shape[1]
    s_c = jnp.einsum('bqkgd,bnkd->bkgqn', qg, kc)
    mask_c = (jnp.arange(n_c) * CMP_STRIDE + CMP_BLOCK - 1)[None, :] <= t[:, None]
    p_c = masked_softmax(s_c, mask_c)
    o_cmp = jnp.einsum('bkgqn,bnkd->bqkgd', p_c.astype(vc.dtype), vc)

    n_s = -(-T // SLC_BLOCK)
    imp = jnp.einsum('bkgqn,ns->bkqs', p_c, cmp_to_slc_matrix(n_c, n_s))
    blk = jnp.arange(n_s)[None, :]
    cur = (t // SLC_BLOCK)[:, None]
    visible = blk * SLC_BLOCK <= t[:, None]
    forced = (blk == 0) | (blk == cur) | (blk == cur - 1)
    score = jnp.where(forced, FORCE_SCORE, jnp.where(visible, imp, -jnp.inf))
    n_sel = min(N_SELECT, n_s)
    _, sel = lax.top_k(score, n_sel)

    pad = n_s * SLC_BLOCK - T

    def to_blocks(x):
        x = jnp.pad(x, ((0, 0), (0, pad), (0, 0), (0, 0)))
        return x.reshape(B, n_s, SLC_BLOCK, NSA_KV_HEADS, HEAD_DIM).transpose(0, 3, 1, 2, 4)

    ks = to_blocks(kv_full[:, :, 2])
    vs = to_blocks(kv_full[:, :, 3])
    qb_size = math.gcd(Sq, QUERY_BLOCK)
    n_qb = Sq // qb_size
    n_keys = n_sel * SLC_BLOCK
    gather = jax.vmap(jax.vmap(lambda blocks, i: blocks[i]))

    def sel_block(args):
        qb, ib, tb = args
        gk = gather(ks, ib).reshape(B, NSA_KV_HEADS, qb_size, n_keys, HEAD_DIM)
        gv = gather(vs, ib).reshape(B, NSA_KV_HEADS, qb_size, n_keys, HEAD_DIM)
        s = jnp.einsum('bqkgd,bkqmd->bkgqm', qb, gk)
        pos = (ib[..., None] * SLC_BLOCK + jnp.arange(SLC_BLOCK)).reshape(B, NSA_KV_HEADS, qb_size, n_keys)
        mask = (pos <= tb[None, None, :, None])[:, :, None]
        p = masked_softmax(s, mask)
        return jnp.einsum('bkgqm,bkqmd->bqkgd', p.astype(gv.dtype), gv)

    q_blocks = qg.reshape(B, n_qb, qb_size, NSA_KV_HEADS, GQA_GROUP, HEAD_DIM).transpose(1, 0, 2, 3, 4, 5)
    sel_blocks = sel.reshape(B, NSA_KV_HEADS, n_qb, qb_size, n_sel).transpose(2, 0, 1, 3, 4)
    o_slc = lax.map(sel_block, (q_blocks, sel_blocks, t.reshape(n_qb, qb_size)))
    o_slc = o_slc.transpose(1, 0, 2, 3, 4, 5).reshape(B, Sq, NSA_KV_HEADS, GQA_GROUP, HEAD_DIM)

    kw = win_ext[:, :, 0]
    vw = win_ext[:, :, 1]

    def win_block(args):
        qb, start = args
        kb = lax.dynamic_slice_in_dim(kw, start, qb_size + WINDOW, axis=1)
        vb = lax.dynamic_slice_in_dim(vw, start, qb_size + WINDOW, axis=1)
        s = jnp.einsum('bqkgd,bmkd->bkgqm', qb, kb)
        qi = start + jnp.arange(qb_size)
        ki = start + jnp.arange(qb_size + WINDOW)
        mask = ((ki[None, :] > qi[:, None]) & (ki[None, :] <= qi[:, None] + WINDOW)
                & (ki[None, :] >= WINDOW - q_pos0))
        p = masked_softmax(s, mask)
        return jnp.einsum('bkgqm,bmkd->bqkgd', p.astype(vb.dtype), vb)

    o_win = lax.map(win_block, (q_blocks, jnp.arange(n_qb) * qb_size))
    o_win = o_win.transpose(1, 0, 2, 3, 4, 5).reshape(B, Sq, NSA_KV_HEADS, GQA_GROUP, HEAD_DIM)

    g = jax.nn.sigmoid(gate_logits.astype(jnp.float32)).astype(q.dtype)
    g = g.reshape(B, Sq, N_BRANCH, NSA_KV_HEADS, GQA_GROUP)[..., None]
    o = g[:, :, 0] * o_cmp + g[:, :, 1] * o_slc + g[:, :, 2] * o_win
    return o.reshape(B, Sq, NSA_WIDTH)


def spatial_gating(u, v, w_s, b_s):
    B, S, _ = v.shape
    n_chunk = -(-S // CHUNK)
    vp = jnp.pad(v, ((0, 0), (0, n_chunk * CHUNK - S), (0, 0)))
    vp = vp.reshape(B, n_chunk, CHUNK, GMLP_GROUPS, GMLP_GROUP_DIM)
    w = jnp.where(jnp.tril(jnp.ones((CHUNK, CHUNK), dtype=bool)), w_s, 0.0)
    s = jnp.einsum('gij,bnjgc->bnigc', w, vp) + b_s.T[None, None, :, :, None]
    s = s.reshape(B, n_chunk * CHUNK, GMLP_WIDTH)[:, :S]
    return u * s


def residual_ffn(x, o_a, o_b, w_out, ln2_g, w_ff1, w_ff2):
    h = x + jnp.concatenate([o_a, o_b], axis=-1) @ w_out
    f = jax.nn.relu(rmsnorm(h, ln2_g) @ w_ff1)
    return h + (f * f) @ w_ff2


def layer_step(x, kv_past, win_past, ln1_g, w_in, cmp_w1, cmp_b1, cmp_w2, cmp_b2, cmp_pos,
               ln_v_g, ln_v_b, w_s, b_s, w_out, ln2_g, w_ff1, w_ff2):
    Sq = x.shape[1]
    q, kv, wkv, gate_logits, u, v = project(x, ln1_g, w_in, ln_v_g, ln_v_b)
    kv_full = kv if kv_past is None else jnp.concatenate([kv_past, kv], axis=1)
    win_cat = wkv if win_past is None else jnp.concatenate([win_past, wkv], axis=1)
    win_ext = jnp.pad(win_cat, ((0, 0), (WINDOW + Sq - win_cat.shape[1], 0), (0, 0), (0, 0), (0, 0)))
    o_a = nsa_mixer(q, kv_full, win_ext, gate_logits, cmp_w1, cmp_b1, cmp_w2, cmp_b2, cmp_pos)
    o_b = spatial_gating(u, v, w_s, b_s)
    y = residual_ffn(x, o_a, o_b, w_out, ln2_g, w_ff1, w_ff2)
    keep = min(WINDOW, Sq) if win_past is None else win_past.shape[1]
    return y, kv, win_cat[:, win_cat.shape[1] - keep:], v


def setup_inputs(seed: int = 0) -> dict:
    key = jax.random.key(seed)
    ks = jax.random.split(key, 24)
    n_pages = PAST_LEN // PAGE_SIZE
    n_used = DEC_BATCH * n_pages
    n_pool = n_used + n_used // 4
    win_buf = min(WINDOW, PAST_LEN)

    def nrm(k, shape, scale):
        return scale * jax.random.normal(k, shape, jnp.float32)

    page_table = jax.random.permutation(ks[4], n_pool)[:n_used].reshape(DEC_BATCH, n_pages).astype(jnp.int32)
    return {
        'x_prompt': nrm(ks[0], (BATCH, SEQ, D_MODEL), 1.0),
        'x_sample': nrm(ks[1], (DEC_BATCH, DEC_SEQ, D_MODEL), 1.0),
        'cache_kv': nrm(ks[2], (DEPTH, n_pool, PAGE_SIZE, N_KV_STREAMS, NSA_KV_HEADS, HEAD_DIM), 1.0),
        'state_win_kv': nrm(ks[3], (DEPTH, DEC_BATCH, win_buf, 2, NSA_KV_HEADS, HEAD_DIM), 1.0),
        'page_table': page_table,
        'ln1_g': 1.0 + nrm(ks[5], (DEPTH, D_MODEL), 0.02),
        'w_in': nrm(ks[6], (DEPTH, D_MODEL, IN_COLS), D_MODEL ** -0.5),
        'cmp_w1': nrm(ks[7], (DEPTH, 2, CMP_BLOCK * HEAD_DIM, CMP_HIDDEN), (CMP_BLOCK * HEAD_DIM) ** -0.5),
        'cmp_b1': nrm(ks[8], (DEPTH, 2, CMP_HIDDEN), 0.02),
        'cmp_w2': nrm(ks[9], (DEPTH, 2, CMP_HIDDEN, HEAD_DIM), CMP_HIDDEN ** -0.5),
        'cmp_b2': nrm(ks[10], (DEPTH, 2, HEAD_DIM), 0.02),
        'cmp_pos': nrm(ks[11], (DEPTH, 2, CMP_BLOCK, HEAD_DIM), 0.1),
        'ln_v_g': 1.0 + nrm(ks[12], (DEPTH, GMLP_WIDTH), 0.02),
        'ln_v_b': nrm(ks[13], (DEPTH, GMLP_WIDTH), 0.02),
        'w_s': nrm(ks[14], (DEPTH, GMLP_GROUPS, CHUNK, CHUNK), CHUNK ** -0.5),
        'b_s': 1.0 + nrm(ks[15], (DEPTH, GMLP_GROUPS, CHUNK), 0.02),
        'w_out': nrm(ks[16], (DEPTH, NSA_WIDTH + GMLP_WIDTH, D_MODEL), (NSA_WIDTH + GMLP_WIDTH) ** -0.5),
        'ln2_g': 1.0 + nrm(ks[17], (DEPTH, D_MODEL), 0.02),
        'w_ff1': nrm(ks[18], (DEPTH, D_MODEL, D_FF), D_MODEL ** -0.5),
        'w_ff2': nrm(ks[19], (DEPTH, D_FF, D_MODEL), D_FF ** -0.5),
        'ln_f_g': 1.0 + nrm(ks[20], (D_MODEL,), 0.02),
    }


def reference(x_prompt, x_sample, cache_kv, state_win_kv, page_table, ln1_g, w_in, cmp_w1, cmp_b1,
              cmp_w2, cmp_b2, cmp_pos, ln_v_g, ln_v_b, w_s, b_s, w_out, ln2_g, w_ff1, w_ff2, ln_f_g):
    hp = x_prompt
    hs = x_sample
    kv_p, kv_s, win_p, win_s, v_s = [], [], [], [], []
    for l in range(DEPTH):
        params = (ln1_g[l], w_in[l], cmp_w1[l], cmp_b1[l], cmp_w2[l], cmp_b2[l], cmp_pos[l],
                  ln_v_g[l], ln_v_b[l], w_s[l], b_s[l], w_out[l], ln2_g[l], w_ff1[l], w_ff2[l])
        hp, kv_new, win_new, _ = layer_step(hp, None, None, *params)
        kv_p.append(kv_new)
        win_p.append(win_new)
        past = cache_kv[l][page_table].reshape(DEC_BATCH, PAST_LEN, N_KV_STREAMS, NSA_KV_HEADS, HEAD_DIM)
        hs, kv_new, win_new, v_new = layer_step(hs, past, state_win_kv[l], *params)
        kv_s.append(kv_new)
        win_s.append(win_new)
        v_s.append(v_new)
    y_prompt = rmsnorm(hp, ln_f_g)
    y_sample = rmsnorm(hs, ln_f_g)
    return (y_prompt, y_sample, jnp.stack(kv_p), jnp.stack(kv_s), jnp.stack(win_p), jnp.stack(win_s), jnp.stack(v_s))
```

```python
import functools

import numpy as np
import jax
import jax.numpy as jnp
from jax import lax
from jax.experimental import pallas as pl
from jax.experimental.pallas import tpu as pltpu

F32 = jnp.float32
BF16 = jnp.bfloat16

D_MODEL = 1024
HEAD_DIM = 64
N_HEADS = 8
N_KVH = 2
GQA = 4
CMP_BLOCK = 32
CMP_STRIDE = 16
CMP_HIDDEN = 128
SLC_BLOCK = 64
N_SELECT = 16
WINDOW = 512
CHUNK = 128
D_FF = 4096
PAGE = 128
EPS = 1e-6
NEG = -1e30
FORCE_SCORE = 1e9
LANES = 128
VMEM_LIMIT = 56 * 1024 * 1024

C_Q, C_KV, C_WIN, C_GATE, C_U, C_V, C_END = 0, 512, 1024, 1280, 1408, 1920, 2432


def _nn(a, b):
    return jnp.dot(a, b, preferred_element_type=F32)


def _nt(a, b):
    return lax.dot_general(a, b, (((1,), (1,)), ((), ())), preferred_element_type=F32)


def _gelu(x):
    return 0.5 * x * (1.0 + jnp.tanh(0.7978845608028654 * (x + 0.044715 * (x * x * x))))


def _rms(x, g):
    return x * lax.rsqrt(jnp.mean(x * x, axis=-1, keepdims=True) + EPS) * g


def _softmax_rows(s, mask):
    s = jnp.where(mask, s, NEG)
    m = jnp.max(s, axis=-1, keepdims=True)
    p = jnp.where(mask, jnp.exp(s - m), 0.0)
    l = jnp.sum(p, axis=-1, keepdims=True)
    return p / jnp.maximum(l, 1e-30)


def _proj_kernel(x_ref, g1_ref, w_ref, lvg_ref, lvb_ref,
                 q_ref, kv_ref, wkv_ref, kvw_ref, gate_ref, u_ref, v_ref):
    x = x_ref[...]
    h = _rms(x, g1_ref[...]).astype(BF16)

    def z(a, b):
        return _nn(h, w_ref[:, a:b])

    q_ref[...] = z(C_Q, C_KV) * (HEAD_DIM ** -0.5)
    kv = z(C_KV, C_WIN)
    kv_ref[...] = kv
    wkv = z(C_WIN, C_GATE)
    wkv_ref[...] = wkv
    kvw_ref[:, 0:256] = kv[:, 256:512].astype(BF16)
    kvw_ref[:, 256:512] = wkv.astype(BF16)
    gate_ref[...] = jax.nn.sigmoid(z(C_GATE, C_U))
    u_ref[...] = _gelu(z(C_U, C_V))
    zv = _gelu(z(C_V, C_END))
    mu = jnp.mean(zv, axis=-1, keepdims=True)
    var = jnp.mean(jnp.square(zv - mu), axis=-1, keepdims=True)
    v_ref[...] = (zv - mu) * lax.rsqrt(var + EPS) * lvg_ref[...] + lvb_ref[...]


def _project(x, g1, w_all, lvg, lvb, tm=512):
    n = x.shape[0]
    row = lambda i: (i, 0)
    const = lambda i: (0, 0)
    widths = (512, 512, 256, 512, 128, 512, 512)
    dtypes = (F32, F32, F32, BF16, F32, F32, F32)
    return pl.pallas_call(
        _proj_kernel,
        grid=(n // tm,),
        in_specs=[pl.BlockSpec((tm, D_MODEL), row), pl.BlockSpec((1, D_MODEL), const),
                  pl.BlockSpec((D_MODEL, C_END), const), pl.BlockSpec((1, 512), const),
                  pl.BlockSpec((1, 512), const)],
        out_specs=[pl.BlockSpec((tm, w), row) for w in widths],
        out_shape=[jax.ShapeDtypeStruct((n, w), d) for w, d in zip(widths, dtypes)],
        compiler_params=pltpu.CompilerParams(dimension_semantics=("arbitrary",), vmem_limit_bytes=VMEM_LIMIT),
        name="nsa_proj",
    )(x, g1, w_all, lvg, lvb)


def _compress_core(lhs0, lhs1, w1_ref, pos_ref, b1_ref, w2a_ref, w2b_ref, b2_ref, n_seg):
    w1 = w1_ref[...]
    pos = pos_ref[...]
    p0 = jnp.broadcast_to(pos[0:1], (8, 1024)).astype(BF16)
    p1 = jnp.broadcast_to(pos[1:2], (8, 1024)).astype(BF16)
    c1 = b1_ref[...] + _nn(p0, w1[:, 0:128])[0:1] + _nn(p1, w1[:, 128:256])[0:1]

    def hidden(lhs):
        h = _nn(lhs, w1)
        nxt = pltpu.roll(h[:, 128:256], n_seg - 1, 0)
        return _gelu(h[:, 0:128] + nxt + c1).astype(BF16)

    return _nn(hidden(lhs0), w2a_ref[...]) + _nn(hidden(lhs1), w2b_ref[...]) + b2_ref[...]


def _cmp_prompt_kernel(k_ref, v_ref, w1k_ref, w1v_ref, posk_ref, posv_ref, b1k_ref, b1v_ref,
                       w2ak_ref, w2bk_ref, w2av_ref, w2bv_ref, b2k_ref, b2v_ref,
                       kc_ref, vc_ref, u_scr):
    n_seg = k_ref.shape[0] // CMP_STRIDE
    lo = lax.broadcasted_iota(jnp.int32, (n_seg, LANES), 1) < HEAD_DIM

    def run(x_ref, w1, pos, b1, w2a, w2b, b2, out_ref):
        for pr in range(CMP_STRIDE // 2):
            a = x_ref[pl.ds(2 * pr, n_seg, stride=CMP_STRIDE), :]
            b = x_ref[pl.ds(2 * pr + 1, n_seg, stride=CMP_STRIDE), :]
            u_scr[0, :, pr * LANES:(pr + 1) * LANES] = jnp.where(lo, a, pltpu.roll(b, HEAD_DIM, 1)).astype(BF16)
            u_scr[1, :, pr * LANES:(pr + 1) * LANES] = jnp.where(lo, pltpu.roll(a, HEAD_DIM, 1), b).astype(BF16)
        out_ref[0] = _compress_core(u_scr[0], u_scr[1], w1, pos, b1, w2a, w2b, b2, n_seg)

    run(k_ref, w1k_ref, posk_ref, b1k_ref, w2ak_ref, w2bk_ref, b2k_ref, kc_ref)
    run(v_ref, w1v_ref, posv_ref, b1v_ref, w2av_ref, w2bv_ref, b2v_ref, vc_ref)


def _cmp_weight_specs(const):
    return ([pl.BlockSpec((1024, 256), const)] * 2 + [pl.BlockSpec((2, 1024), const)] * 2
            + [pl.BlockSpec((1, 128), const)] * 2 + [pl.BlockSpec((128, 128), const)] * 4
            + [pl.BlockSpec((1, 128), const)] * 2)


def _compress_prompt(kv, cw, batch, seq):
    n_seg = seq // CMP_STRIDE
    const = lambda b: (0, 0)
    return pl.pallas_call(
        _cmp_prompt_kernel,
        grid=(batch,),
        in_specs=[pl.BlockSpec((seq, LANES), lambda b: (b, 0)), pl.BlockSpec((seq, LANES), lambda b: (b, 1))]
        + _cmp_weight_specs(const),
        out_specs=[pl.BlockSpec((1, n_seg, LANES), lambda b: (b, 0, 0))] * 2,
        out_shape=[jax.ShapeDtypeStruct((batch, n_seg, LANES), F32)] * 2,
        scratch_shapes=[pltpu.VMEM((2, n_seg, 1024), BF16)],
        compiler_params=pltpu.CompilerParams(dimension_semantics=("arbitrary",), vmem_limit_bytes=VMEM_LIMIT),
        name="nsa_cmp_prompt",
    )(kv, kv, *cw)


def _rank_row(score, blk, j, rj):
    lower = jnp.where(blk < j, 1.0, 0.0)
    beat = jnp.where(score > rj, 1.0, jnp.where(score == rj, lower, 0.0))
    return jnp.sum(beat, axis=0, keepdims=True)


def _scores_t(imp_t, blk, t):
    cur = t >> 6
    forced = (blk == 0) | (blk == cur) | (blk == cur - 1)
    visible = blk * SLC_BLOCK <= t
    return jnp.where(forced, FORCE_SCORE, jnp.where(visible, imp_t, -jnp.inf)), visible


def _importance_t(cov_t, psum):
    hi = psum.astype(BF16)
    lo_ = (psum - hi.astype(F32)).astype(BF16)
    c = cov_t.astype(BF16)
    return _nt(c, hi) + _nt(c, lo_)


def _nsa_prompt_kernel(q_ref, g_ref, kc_ref, vc_ref, kvw_ref, covt_ref, o_ref, m_scr, l_scr, acc_scr):
    qb = pl.program_id(1)
    t0 = qb * 128
    lane = lax.broadcasted_iota(jnp.int32, (128, LANES), 1)
    lo = lane < HEAD_DIM
    t_col = t0 + lax.broadcasted_iota(jnp.int32, (128, 1), 0)
    q = q_ref[...]
    qm = [[jnp.where(lo if k == 0 else jnp.logical_not(lo), q[:, g * LANES:(g + 1) * LANES], 0.0).astype(BF16)
           for g in range(GQA)] for k in range(N_KVH)]

    kc = kc_ref[0].astype(BF16)
    vc = vc_ref[0].astype(BF16)
    n_c = kc.shape[0]
    n_iota = lax.broadcasted_iota(jnp.int32, (128, n_c), 1)
    mask_c = n_iota * CMP_STRIDE + (CMP_BLOCK - 1) <= t_col
    o_cmp = [[None] * GQA for _ in range(N_KVH)]
    psum = [None] * N_KVH
    for k in range(N_KVH):
        for g in range(GQA):
            p = _softmax_rows(_nt(qm[k][g], kc), mask_c)
            o_cmp[k][g] = _nn(p.astype(BF16), vc)
            psum[k] = p if g == 0 else psum[k] + p

    n_s = covt_ref.shape[0]
    blk = lax.broadcasted_iota(jnp.int32, (n_s, 128), 0)
    t_lane = t0 + lax.broadcasted_iota(jnp.int32, (n_s, 128), 1)
    sel = []
    for k in range(N_KVH):
        score, visible = _scores_t(_importance_t(covt_ref[...], psum[k]), blk, t_lane)
        cnt = jnp.concatenate([_rank_row(score, blk, j, score[j:j + 1, :]) for j in range(n_s)], axis=0)
        m_t = jnp.where((cnt < N_SELECT) & visible, 1.0, 0.0)
        m_t = jnp.concatenate([m_t, jnp.zeros((128 - n_s, 128), F32)], axis=0)
        sel.append(m_t.T.astype(BF16))

    m_scr[...] = jnp.full(m_scr.shape, NEG, F32)
    l_scr[...] = jnp.zeros(l_scr.shape, F32)
    acc_scr[...] = jnp.zeros(acc_scr.shape, F32)
    tk = 256

    def body(kt, carry):
        k0 = pl.multiple_of(kt * tk, tk)
        k_t = kvw_ref[pl.ds(k0, tk), 0:128]
        v_t = kvw_ref[pl.ds(k0, tk), 128:256]
        jj = lax.broadcasted_iota(jnp.int32, (128, tk), 0)
        cc = lax.broadcasted_iota(jnp.int32, (128, tk), 1)
        e_t = jnp.where(jj == kt * (tk // SLC_BLOCK) + (cc >> 6), 1.0, 0.0).astype(BF16)
        causal = (k0 + cc) <= t_col
        for k in range(N_KVH):
            allowed = (_nn(sel[k], e_t) > 0.5) & causal
            bias = jnp.where(allowed, 0.0, NEG)
            for g in range(GQA):
                h = k * GQA + g
                s = _nt(qm[k][g], k_t) + bias
                m_old = m_scr[h]
                m_new = jnp.maximum(m_old, jnp.max(s, axis=-1, keepdims=True))
                alpha = jnp.exp(m_old - m_new)
                p = jnp.exp(s - m_new)
                l_scr[h] = alpha * l_scr[h] + jnp.sum(p, axis=-1, keepdims=True)
                acc_scr[h] = alpha * acc_scr[h] + _nn(p.astype(BF16), v_t)
                m_scr[h] = m_new
        return carry

    lax.fori_loop(0, qb // 2 + 1, body, 0)

    s0 = pl.multiple_of(jnp.maximum(t0 - WINDOW, 0), 128)
    wk = kvw_ref[pl.ds(s0, WINDOW + 128), 256:384]
    wv = kvw_ref[pl.ds(s0, WINDOW + 128), 384:512]
    kpos = s0 + lax.broadcasted_iota(jnp.int32, (128, WINDOW + 128), 1)
    mask_w = (kpos <= t_col) & (kpos > t_col - WINDOW)

    gt = g_ref[...]
    for c in range(GQA):
        halves = []
        for k in range(N_KVH):
            h = k * GQA + c
            o_slc = acc_scr[h] / l_scr[h]
            pw = _softmax_rows(_nt(qm[k][c], wk), mask_w)
            o_win = _nn(pw.astype(BF16), wv)
            halves.append(gt[:, h:h + 1] * o_cmp[k][c] + gt[:, 8 + h:9 + h] * o_slc + gt[:, 16 + h:17 + h] * o_win)
        o_ref[:, c * LANES:(c + 1) * LANES] = jnp.where(lo, halves[0], halves[1]).astype(BF16)


def _nsa_prompt(q, gate, kc, vc, kvw, cov_t, batch, seq):
    n_qb = seq // 128
    n_seg = kc.shape[1]
    row = lambda b, i: (b * n_qb + i, 0)
    return pl.pallas_call(
        _nsa_prompt_kernel,
        grid=(batch, n_qb),
        in_specs=[pl.BlockSpec((128, 512), row), pl.BlockSpec((128, 128), row),
                  pl.BlockSpec((1, n_seg, LANES), lambda b, i: (b, 0, 0)),
                  pl.BlockSpec((1, n_seg, LANES), lambda b, i: (b, 0, 0)),
                  pl.BlockSpec((seq, 512), lambda b, i: (b, 0)),
                  pl.BlockSpec(cov_t.shape, lambda b, i: (0, 0))],
        out_specs=pl.BlockSpec((128, 512), row),
        out_shape=jax.ShapeDtypeStruct((batch * seq, 512), BF16),
        scratch_shapes=[pltpu.VMEM((N_HEADS, 128, 1), F32), pltpu.VMEM((N_HEADS, 128, 1), F32),
                        pltpu.VMEM((N_HEADS, 128, LANES), F32)],
        compiler_params=pltpu.CompilerParams(dimension_semantics=("arbitrary", "arbitrary"),
                                             vmem_limit_bytes=VMEM_LIMIT),
        name="nsa_prompt_attn",
    )(q, gate, kc, vc, kvw, cov_t)


def _stack_queries(q, lo):
    rows = [jnp.where(lo if k == 0 else jnp.logical_not(lo), q[:, g * LANES:(g + 1) * LANES], 0.0)
            for k in range(N_KVH) for g in range(GQA)]
    return jnp.concatenate(rows, axis=0).astype(BF16)


def _smp_cmp_kernel(n_pages, pt_ref, *refs):
    pages = refs[:n_pages]
    (q_ref, w1k_ref, w1v_ref, posk_ref, posv_ref, b1k_ref, b1v_ref, w2ak_ref, w2bk_ref, w2av_ref, w2bv_ref,
     b2k_ref, b2v_ref, covt_ref, ocmp_ref, msel_ref, u_scr, score_scr, cnt_scr) = refs[n_pages:]
    n_seg = n_pages * (PAGE // CMP_STRIDE)
    past = n_pages * PAGE
    flat = [p.reshape(PAGE * 4, HEAD_DIM) for p in pages]

    def gather(pr, carry):
        for st in range(2):
            for k in range(N_KVH):
                for pp in range(n_pages // 2):
                    def seg_rows(s):
                        return jnp.concatenate(
                            [flat[2 * pp + a][pl.ds(s * 4 + st * 2 + k, PAGE // CMP_STRIDE, stride=CMP_STRIDE * 4), :]
                             for a in range(2)], axis=0)
                    piece = jnp.concatenate([seg_rows(2 * pr), seg_rows(2 * pr + 1)], axis=1)
                    u_scr[pr, st * 2 + k, pp * 16:(pp + 1) * 16, :] = piece.astype(BF16)
        return carry

    lax.fori_loop(0, CMP_STRIDE // 2, gather, 0)

    def lhs(st, k):
        return jnp.concatenate([u_scr[pr, st * 2 + k] for pr in range(CMP_STRIDE // 2)], axis=1)

    kc = _compress_core(lhs(0, 0), lhs(0, 1), w1k_ref, posk_ref, b1k_ref, w2ak_ref, w2bk_ref, b2k_ref, n_seg)
    vc = _compress_core(lhs(1, 0), lhs(1, 1), w1v_ref, posv_ref, b1v_ref, w2av_ref, w2bv_ref, b2v_ref, n_seg)

    q = q_ref[...]
    n_q = q.shape[0]
    lo8 = lax.broadcasted_iota(jnp.int32, (n_q, LANES), 1) < HEAD_DIM
    q_all = _stack_queries(q, lo8)
    rows = lax.broadcasted_iota(jnp.int32, (N_HEADS * n_q, n_seg), 0)
    t_row = past + (rows & (n_q - 1))
    n_iota = lax.broadcasted_iota(jnp.int32, (N_HEADS * n_q, n_seg), 1)
    p = _softmax_rows(_nt(q_all, kc.astype(BF16)), n_iota * CMP_STRIDE + (CMP_BLOCK - 1) <= t_row)
    o = _nn(p.astype(BF16), vc.astype(BF16))
    half = GQA * n_q
    for c in range(GQA):
        ocmp_ref[:, c * LANES:(c + 1) * LANES] = jnp.where(lo8, o[c * n_q:(c + 1) * n_q],
                                                         o[half + c * n_q:half + (c + 1) * n_q])
    psum = [sum(p[k * half + g * n_q:k * half + (g + 1) * n_q] for g in range(GQA)) for k in range(N_KVH)]
    psum = jnp.concatenate(psum + [jnp.zeros((LANES - N_KVH * n_q, n_seg), F32)], axis=0)

    n_sp = covt_ref.shape[0]
    n_s = past // SLC_BLOCK + 1
    rows_used = -(-n_s // 8) * 8
    blk = lax.broadcasted_iota(jnp.int32, (n_sp, LANES), 0)
    t_lane = past + (lax.broadcasted_iota(jnp.int32, (n_sp, LANES), 1) & (n_q - 1))
    score, visible = _scores_t(_importance_t(covt_ref[...], psum), blk, t_lane)
    score_scr[...] = score
    cnt_scr[...] = jnp.full(cnt_scr.shape, float(n_sp), F32)
    blk_u = blk[0:rows_used]

    def rank(j, carry):
        rj = score_scr[pl.ds(j, 1), :]
        cnt_scr[pl.ds(j, 1), :] = _rank_row(score_scr[0:rows_used, :], blk_u, j, rj)
        return carry

    lax.fori_loop(0, n_s, rank, 0)
    m_t = jnp.where((cnt_scr[...] < N_SELECT) & visible, 1.0, 0.0)
    msel_ref[...] = m_t.T[0:N_KVH * n_q, :]


def _page_specs(n_pages, half):
    def spec(p):
        return pl.BlockSpec((None, PAGE, 2, N_KVH, HEAD_DIM), lambda b, pt: (pt[b * n_pages + p], 0, half, 0, 0))
    return [spec(p) for p in range(n_pages)]


def _sample_compress(pt, cache, q_s, cw, cov_t):
    n_seq, n_q, _ = q_s.shape
    n_pages = pt.shape[0] // n_seq
    n_seg = n_pages * (PAGE // CMP_STRIDE)
    n_sp = cov_t.shape[0]
    const = lambda b, pt: (0, 0)
    kern = functools.partial(_smp_cmp_kernel, n_pages)
    return pl.pallas_call(
        kern,
        grid_spec=pltpu.PrefetchScalarGridSpec(
            num_scalar_prefetch=1, grid=(n_seq,),
            in_specs=_page_specs(n_pages, 0)
            + [pl.BlockSpec((None, n_q, 512), lambda b, pt: (b, 0, 0))]
            + _cmp_weight_specs(const) + [pl.BlockSpec(cov_t.shape, const)],
            out_specs=[pl.BlockSpec((None, n_q, 512), lambda b, pt: (b, 0, 0)),
                       pl.BlockSpec((None, N_KVH * n_q, n_sp), lambda b, pt: (b, 0, 0))],
            scratch_shapes=[pltpu.VMEM((CMP_STRIDE // 2, 4, n_seg, LANES), BF16),
                            pltpu.VMEM((n_sp, LANES), F32), pltpu.VMEM((n_sp, LANES), F32)]),
        out_shape=[jax.ShapeDtypeStruct((n_seq, n_q, 512), F32),
                   jax.ShapeDtypeStruct((n_seq, N_KVH * n_q, n_sp), F32)],
        compiler_params=pltpu.CompilerParams(dimension_semantics=("arbitrary",), vmem_limit_bytes=VMEM_LIMIT),
        name="nsa_sample_cmp",
    )(pt, *([cache] * n_pages), q_s, *cw, cov_t)


def _smp_attn_kernel(n_pages, pt_ref, *refs):
    pages = refs[:n_pages]
    (q_ref, msel_ref, exp_ref, kvn_ref, win_ref, wkvn_ref, g_ref, ocmp_ref, o_ref, k_scr, v_scr) = refs[n_pages:]
    past = n_pages * PAGE
    q = q_ref[...]
    n_q = q.shape[0]
    n_rows = N_HEADS * n_q
    lo8 = lax.broadcasted_iota(jnp.int32, (n_q, LANES), 1) < HEAD_DIM
    q_all = _stack_queries(q, lo8)
    qi = lax.broadcasted_iota(jnp.int32, (n_rows, 1), 0) & (n_q - 1)

    def pad_rows(x):
        return jnp.concatenate([x, jnp.zeros((LANES - n_q, LANES), F32)], axis=0).astype(BF16)

    def both(ref2d, j):
        n = ref2d.shape[0] // 4
        return jnp.concatenate([ref2d[pl.ds(j * 2, n, stride=4), :], ref2d[pl.ds(j * 2 + 1, n, stride=4), :]], axis=1)

    for p in range(n_pages):
        flat = pages[p].reshape(PAGE * 4, HEAD_DIM)
        k_scr[p * PAGE:(p + 1) * PAGE, :] = both(flat, 0).astype(BF16)
        v_scr[p * PAGE:(p + 1) * PAGE, :] = both(flat, 1).astype(BF16)

    def rep(x):
        return jnp.concatenate([x[0:n_q]] * GQA + [x[n_q:2 * n_q]] * GQA, axis=0)

    msel = msel_ref[...]
    n_blk = past // SLC_BLOCK
    allowed = rep(_nn(msel[:, 0:n_blk].astype(BF16), exp_ref[...])) > 0.5
    s_main = jnp.where(allowed, _nt(q_all, k_scr[...]), NEG)
    kvn = kvn_ref[...]
    k_tail = pad_rows(kvn[:, 256:384])
    v_tail = pad_rows(kvn[:, 384:512])
    ti = lax.broadcasted_iota(jnp.int32, (n_rows, LANES), 1)
    ok_tail = (rep(jnp.broadcast_to(msel[:, n_blk:n_blk + 1], (2 * n_q, LANES))) > 0.5) & (ti <= qi)
    s_tail = jnp.where(ok_tail, _nt(q_all, k_tail), NEG)
    m = jnp.maximum(jnp.max(s_main, axis=-1, keepdims=True), jnp.max(s_tail, axis=-1, keepdims=True))
    p_main = jnp.exp(s_main - m)
    p_tail = jnp.exp(s_tail - m)
    l = jnp.sum(p_main, axis=-1, keepdims=True) + jnp.sum(p_tail, axis=-1, keepdims=True)
    o_slc = (_nn(p_main.astype(BF16), v_scr[...]) + _nn(p_tail.astype(BF16), v_tail)) / l

    wflat = win_ref.reshape(WINDOW * 4, HEAD_DIM)
    wk = both(wflat, 0).astype(BF16)
    wv = both(wflat, 1).astype(BF16)
    wn = wkvn_ref[...]
    wk_tail = pad_rows(wn[:, 0:128])
    wv_tail = pad_rows(wn[:, 128:256])
    mi = lax.broadcasted_iota(jnp.int32, (n_rows, WINDOW), 1)
    sw_main = jnp.where(mi > qi, _nt(q_all, wk), NEG)
    sw_tail = jnp.where(ti <= qi, _nt(q_all, wk_tail), NEG)
    mw = jnp.maximum(jnp.max(sw_main, axis=-1, keepdims=True), jnp.max(sw_tail, axis=-1, keepdims=True))
    pw_main = jnp.exp(sw_main - mw)
    pw_tail = jnp.exp(sw_tail - mw)
    lw = jnp.sum(pw_main, axis=-1, keepdims=True) + jnp.sum(pw_tail, axis=-1, keepdims=True)
    o_win = (_nn(pw_main.astype(BF16), wv) + _nn(pw_tail.astype(BF16), wv_tail)) / lw

    gt = g_ref[...]
    ocmp = ocmp_ref[...]
    half = GQA * n_q
    for c in range(GQA):
        def pick(o):
            return jnp.where(lo8, o[c * n_q:(c + 1) * n_q], o[half + c * n_q:half + (c + 1) * n_q])

        def gate(br):
            return jnp.where(lo8, gt[:, br * 8 + c:br * 8 + c + 1], gt[:, br * 8 + GQA + c:br * 8 + GQA + c + 1])

        o_ref[:, c * LANES:(c + 1) * LANES] = (gate(0) * ocmp[:, c * LANES:(c + 1) * LANES]
                                               + gate(1) * pick(o_slc) + gate(2) * pick(o_win))


def _sample_attend(pt, cache, q_s, msel, expand, kv_new, win_state, wkv_new, gate, o_cmp):
    n_seq, n_q, _ = q_s.shape
    n_pages = pt.shape[0] // n_seq
    past = n_pages * PAGE
    n_sp = msel.shape[-1]
    seq3 = lambda b, pt: (b, 0, 0)
    kern = functools.partial(_smp_attn_kernel, n_pages)
    return pl.pallas_call(
        kern,
        grid_spec=pltpu.PrefetchScalarGridSpec(
            num_scalar_prefetch=1, grid=(n_seq,),
            in_specs=_page_specs(n_pages, 1)
            + [pl.BlockSpec((None, n_q, 512), seq3), pl.BlockSpec((None, N_KVH * n_q, n_sp), seq3),
               pl.BlockSpec(expand.shape, lambda b, pt: (0, 0)),
               pl.BlockSpec((None, n_q, 512), seq3),
               pl.BlockSpec((None, WINDOW, 2, N_KVH, HEAD_DIM), lambda b, pt: (b, 0, 0, 0, 0)),
               pl.BlockSpec((None, n_q, 256), seq3), pl.BlockSpec((None, n_q, 128), seq3),
               pl.BlockSpec((None, n_q, 512), seq3)],
            out_specs=pl.BlockSpec((None, n_q, 512), seq3),
            scratch_shapes=[pltpu.VMEM((past, LANES), BF16), pltpu.VMEM((past, LANES), BF16)]),
        out_shape=jax.ShapeDtypeStruct((n_seq, n_q, 512), F32),
        compiler_params=pltpu.CompilerParams(dimension_semantics=("arbitrary",), vmem_limit_bytes=VMEM_LIMIT),
        name="nsa_sample_attn",
    )(pt, *([cache] * n_pages), q_s, msel, expand, kv_new, win_state, wkv_new, gate, o_cmp)


def _ffn_kernel(x_ref, oa_ref, u_ref, v_ref, ws_ref, bs_ref, woa_ref, wob_ref, g2_ref, w1_ref, w2_ref, gf_ref,
                y_ref, h_scr, hn_scr, ob_scr, acc_scr):
    j = pl.program_id(1)
    tm = x_ref.shape[0]

    @pl.when(j == 0)
    def _():
        lo = lax.broadcasted_iota(jnp.int32, (CHUNK, LANES), 1) < HEAD_DIM
        for ch in range(tm // CHUNK):
            rows = slice(ch * CHUNK, (ch + 1) * CHUNK)
            cols = []
            for c in range(4):
                vc = v_ref[rows, c * LANES:(c + 1) * LANES].astype(BF16)
                cols.append(jnp.where(lo, _nn(ws_ref[2 * c], vc), _nn(ws_ref[2 * c + 1], vc)))
            s = jnp.concatenate(cols, axis=1) + bs_ref[...]
            ob_scr[rows, :] = (u_ref[rows, :] * s).astype(BF16)
        h = x_ref[...] + _nn(oa_ref[...].astype(BF16), woa_ref[...]) + _nn(ob_scr[...], wob_ref[...])
        h_scr[...] = h
        hn_scr[...] = _rms(h, g2_ref[...]).astype(BF16)
        acc_scr[...] = jnp.zeros(acc_scr.shape, F32)

    f = jnp.maximum(_nn(hn_scr[...], w1_ref[...]), 0.0)
    acc_scr[...] += _nn((f * f).astype(BF16), w2_ref[...])

    @pl.when(j == pl.num_programs(1) - 1)
    def _():
        y_ref[...] = _rms(h_scr[...] + acc_scr[...], gf_ref[...])


def _out_ffn(x, o_a, u, v, ws, bs, woa, wob, g2, w1, w2, gf, tm=512, tf=1024):
    n = x.shape[0]
    row = lambda i, j: (i, 0)
    const = lambda i, j: (0, 0)
    return pl.pallas_call(
        _ffn_kernel,
        grid=(n // tm, D_FF // tf),
        in_specs=[pl.BlockSpec((tm, D_MODEL), row), pl.BlockSpec((tm, 512), row), pl.BlockSpec((tm, 512), row),
                  pl.BlockSpec((tm, 512), row), pl.BlockSpec((8, CHUNK, CHUNK), lambda i, j: (0, 0, 0)),
                  pl.BlockSpec((CHUNK, 512), const), pl.BlockSpec((512, D_MODEL), const),
                  pl.BlockSpec((512, D_MODEL), const), pl.BlockSpec((1, D_MODEL), const),
                  pl.BlockSpec((D_MODEL, tf), lambda i, j: (0, j)), pl.BlockSpec((tf, D_MODEL), lambda i, j: (j, 0)),
                  pl.BlockSpec((1, D_MODEL), const)],
        out_specs=pl.BlockSpec((tm, D_MODEL), row),
        out_shape=jax.ShapeDtypeStruct((n, D_MODEL), F32),
        scratch_shapes=[pltpu.VMEM((tm, D_MODEL), F32), pltpu.VMEM((tm, D_MODEL), BF16),
                        pltpu.VMEM((tm, 512), BF16), pltpu.VMEM((tm, D_MODEL), F32)],
        compiler_params=pltpu.CompilerParams(dimension_semantics=("arbitrary", "arbitrary"),
                                             vmem_limit_bytes=VMEM_LIMIT),
        name="nsa_out_ffn",
    )(x, o_a, u, v, ws, bs, woa, wob, g2, w1, w2, gf)


def _head_perm():
    j = np.arange(512)
    return ((j // 128) + 4 * ((j % 128) // 64)) * 64 + (j % 64)


def _cover_t(n_c, n_s, n_c_pad, n_s_pad):
    ci = np.arange(n_c)[:, None] * CMP_STRIDE
    sj = np.arange(n_s)[None, :] * SLC_BLOCK
    cover = np.clip(np.minimum(ci + CMP_BLOCK, sj + SLC_BLOCK) - np.maximum(ci, sj), 0, None) / CMP_BLOCK
    out = np.zeros((n_s_pad, n_c_pad), np.float32)
    out[:n_s, :n_c] = cover.T
    return jnp.asarray(out)


def _cmp_weights(cmp_w1, cmp_b1, cmp_w2, cmp_b2, cmp_pos):
    z = jnp.zeros((CMP_HIDDEN, HEAD_DIM), F32)
    w1 = [jnp.concatenate([cmp_w1[i, :1024], cmp_w1[i, 1024:]], axis=1).astype(BF16) for i in range(2)]
    pos = [cmp_pos[i].reshape(2, 1024) for i in range(2)]
    b1 = [cmp_b1[i].reshape(1, CMP_HIDDEN) for i in range(2)]
    w2a = [jnp.concatenate([cmp_w2[i], z], axis=1).astype(BF16) for i in range(2)]
    w2b = [jnp.concatenate([z, cmp_w2[i]], axis=1).astype(BF16) for i in range(2)]
    b2 = [jnp.concatenate([cmp_b2[i], cmp_b2[i]]).reshape(1, LANES) for i in range(2)]
    return (w1[0], w1[1], pos[0], pos[1], b1[0], b1[1], w2a[0], w2b[0], w2a[1], w2b[1], b2[0], b2[1])


def kernel(x_prompt, x_sample, cache_kv, state_win_kv, page_table, ln1_g, w_in, cmp_w1, cmp_b1, cmp_w2, cmp_b2,
           cmp_pos, ln_v_g, ln_v_b, w_s, b_s, w_out, ln2_g, w_ff1, w_ff2, ln_f_g):
    batch, seq, _ = x_prompt.shape
    n_seq, n_q, _ = x_sample.shape
    n_pages = page_table.shape[1]
    past = n_pages * PAGE
    perm = _head_perm()

    wi = w_in[0]
    w_all = jnp.concatenate(
        [wi[:, 0:512][:, perm], wi[:, 512:1304], jnp.zeros((D_MODEL, C_U - C_GATE - 24), F32), wi[:, 1304:2328]],
        axis=1).astype(BF16)
    g1 = ln1_g[0].reshape(1, D_MODEL)
    lvg = ln_v_g[0].reshape(1, 512)
    lvb = ln_v_b[0].reshape(1, 512)
    cw = _cmp_weights(cmp_w1[0], cmp_b1[0], cmp_w2[0], cmp_b2[0], cmp_pos[0])
    tril = jnp.tril(jnp.ones((CHUNK, CHUNK), F32))
    ws_p = (w_s[0] * tril).astype(BF16)
    bs_p = jnp.repeat(b_s[0].T, HEAD_DIM, axis=1)
    reps = CHUNK // n_q
    ws_s = jnp.einsum("ab,gij->gaibj", jnp.eye(reps, dtype=F32), (w_s[0] * tril)[:, :n_q, :n_q])
    ws_s = ws_s.reshape(8, CHUNK, CHUNK).astype(BF16)
    bs_s = jnp.tile(jnp.repeat(b_s[0].T[:n_q], HEAD_DIM, axis=1), (reps, 1))
    woa = w_out[0][:512][perm].astype(BF16)
    wob = w_out[0][512:].astype(BF16)
    g2 = ln2_g[0].reshape(1, D_MODEL)
    gf = ln_f_g.reshape(1, D_MODEL)
    w1 = w_ff1[0].astype(BF16)
    w2 = w_ff2[0].astype(BF16)

    xp = x_prompt.reshape(batch * seq, D_MODEL)
    q_p, kv_p, wkv_p, kvw_p, gate_p, u_p, v_p = _project(xp, g1, w_all, lvg, lvb)
    kc_p, vc_p = _compress_prompt(kv_p, cw, batch, seq)
    n_seg_p = seq // CMP_STRIDE
    cov_p = _cover_t(n_seg_p - 1, seq // SLC_BLOCK, n_seg_p, seq // SLC_BLOCK)
    oa_p = _nsa_prompt(q_p, gate_p, kc_p, vc_p, kvw_p, cov_p, batch, seq)
    y_p = _out_ffn(xp, oa_p, u_p, v_p, ws_p, bs_p, woa, wob, g2, w1, w2, gf)

    xs = x_sample.reshape(n_seq * n_q, D_MODEL)
    q_s, kv_s, wkv_s, _, gate_s, u_s, v_s = _project(xs, g1, w_all, lvg, lvb)
    cache = cache_kv[0]
    pt = page_table.reshape(-1)
    n_seg_s = past // CMP_STRIDE
    n_s = past // SLC_BLOCK + 1
    n_sp = -(-n_s // LANES) * LANES
    cov_s = _cover_t(n_seg_s, n_s, n_seg_s, n_sp)
    q_s3 = q_s.reshape(n_seq, n_q, 512)
    ocmp_s, msel = _sample_compress(pt, cache, q_s3, cw, cov_s)
    n_blk = past // SLC_BLOCK
    expand = jnp.asarray((np.arange(n_blk)[:, None] == (np.arange(past)[None, :] // SLC_BLOCK)).astype(np.float32)).astype(BF16)
    oa_s = _sample_attend(pt, cache, q_s3, msel, expand, kv_s.reshape(n_seq, n_q, 512), state_win_kv[0],
                          wkv_s.reshape(n_seq, n_q, 256), gate_s.reshape(n_seq, n_q, 128), ocmp_s)
    y_s = _out_ffn(xs, oa_s.reshape(n_seq * n_q, 512), u_s, v_s, ws_s, bs_s, woa, wob, g2, w1, w2, gf)

    keep = min(WINDOW, seq)
    new_win_p = wkv_p.reshape(batch, seq, 2, N_KVH, HEAD_DIM)[:, seq - keep:]
    new_win_s = jnp.concatenate([state_win_kv[0], wkv_s.reshape(n_seq, n_q, 2, N_KVH, HEAD_DIM)], axis=1)[:, n_q:]
    return (y_p.reshape(batch, seq, D_MODEL),
            y_s.reshape(n_seq, n_q, D_MODEL),
            kv_p.reshape(1, batch, seq, 4, N_KVH, HEAD_DIM),
            kv_s.reshape(1, n_seq, n_q, 4, N_KVH, HEAD_DIM),
            new_win_p[None],
            new_win_s[None],
            v_s.reshape(1, n_seq, n_q, 512))
```

```python
import functools

import numpy as np
import jax
import jax.numpy as jnp
from jax import lax
from jax.experimental import pallas as pl
from jax.experimental.pallas import tpu as pltpu

F32 = jnp.float32
BF16 = jnp.bfloat16

D_MODEL = 1024
HEAD_DIM = 64
N_HEADS = 8
N_KVH = 2
GQA = 4
CMP_BLOCK = 32
CMP_STRIDE = 16
CMP_HIDDEN = 128
SLC_BLOCK = 64
N_SELECT = 16
WINDOW = 512
CHUNK = 128
D_FF = 4096
PAGE = 128
EPS = 1e-6
NEG = -1e30
FORCE_SCORE = 1e9
LANES = 128
VMEM_LIMIT = 56 * 1024 * 1024

C_Q, C_KV, C_WIN, C_GATE, C_U, C_V, C_END = 0, 512, 1024, 1280, 1408, 1920, 2432


def _nn(a, b):
    return jnp.dot(a, b, preferred_element_type=F32)


def _nt(a, b):
    return lax.dot_general(a, b, (((1,), (1,)), ((), ())), preferred_element_type=F32)


def _gelu(x):
    return 0.5 * x * (1.0 + jnp.tanh(0.7978845608028654 * (x + 0.044715 * (x * x * x))))


def _rms(x, g):
    return x * lax.rsqrt(jnp.mean(x * x, axis=-1, keepdims=True) + EPS) * g


def _softmax_rows(s, mask):
    s = jnp.where(mask, s, NEG)
    m = jnp.max(s, axis=-1, keepdims=True)
    p = jnp.where(mask, jnp.exp(s - m), 0.0)
    l = jnp.sum(p, axis=-1, keepdims=True)
    return p / jnp.maximum(l, 1e-30)


def _proj_kernel(x_ref, g1_ref, w_ref, lvg_ref, lvb_ref,
                 q_ref, kv_ref, wkv_ref, kvw_ref, gate_ref, u_ref, v_ref):
    x = x_ref[...]
    h = _rms(x, g1_ref[...]).astype(BF16)

    def z(a, b):
        return _nn(h, w_ref[:, a:b])

    q_ref[...] = z(C_Q, C_KV) * (HEAD_DIM ** -0.5)
    kv = z(C_KV, C_WIN)
    kv_ref[...] = kv
    wkv = z(C_WIN, C_GATE)
    wkv_ref[...] = wkv
    kvw_ref[:, 0:256] = kv[:, 256:512].astype(BF16)
    kvw_ref[:, 256:512] = wkv.astype(BF16)
    gate_ref[...] = jax.nn.sigmoid(z(C_GATE, C_U))
    u_ref[...] = _gelu(z(C_U, C_V))
    zv = _gelu(z(C_V, C_END))
    mu = jnp.mean(zv, axis=-1, keepdims=True)
    var = jnp.mean(jnp.square(zv - mu), axis=-1, keepdims=True)
    v_ref[...] = (zv - mu) * lax.rsqrt(var + EPS) * lvg_ref[...] + lvb_ref[...]


def _project(x, g1, w_all, lvg, lvb, tm=512):
    n = x.shape[0]
    row = lambda i: (i, 0)
    const = lambda i: (0, 0)
    widths = (512, 512, 256, 512, 128, 512, 512)
    dtypes = (F32, F32, F32, BF16, F32, F32, F32)
    return pl.pallas_call(
        _proj_kernel,
        grid=(n // tm,),
        in_specs=[pl.BlockSpec((tm, D_MODEL), row), pl.BlockSpec((1, D_MODEL), const),
                  pl.BlockSpec((D_MODEL, C_END), const), pl.BlockSpec((1, 512), const),
                  pl.BlockSpec((1, 512), const)],
        out_specs=[pl.BlockSpec((tm, w), row) for w in widths],
        out_shape=[jax.ShapeDtypeStruct((n, w), d) for w, d in zip(widths, dtypes)],
        compiler_params=pltpu.CompilerParams(dimension_semantics=("arbitrary",), vmem_limit_bytes=VMEM_LIMIT),
        name="nsa_proj",
    )(x, g1, w_all, lvg, lvb)


def _compress_core(lhs0, lhs1, w1_ref, pos_ref, b1_ref, w2a_ref, w2b_ref, b2_ref, n_seg):
    w1 = w1_ref[...]
    pos = pos_ref[...]
    p0 = jnp.broadcast_to(pos[0:1], (8, 1024)).astype(BF16)
    p1 = jnp.broadcast_to(pos[1:2], (8, 1024)).astype(BF16)
    c1 = b1_ref[...] + _nn(p0, w1[:, 0:128])[0:1] + _nn(p1, w1[:, 128:256])[0:1]

    def hidden(lhs):
        h = _nn(lhs, w1)
        nxt = pltpu.roll(h[:, 128:256], n_seg - 1, 0)
        return _gelu(h[:, 0:128] + nxt + c1).astype(BF16)

    return _nn(hidden(lhs0), w2a_ref[...]) + _nn(hidden(lhs1), w2b_ref[...]) + b2_ref[...]


def _cmp_prompt_kernel(k_ref, v_ref, w1k_ref, w1v_ref, posk_ref, posv_ref, b1k_ref, b1v_ref,
                       w2ak_ref, w2bk_ref, w2av_ref, w2bv_ref, b2k_ref, b2v_ref,
                       kc_ref, vc_ref, u_scr):
    n_seg = k_ref.shape[0] // CMP_STRIDE
    lo = lax.broadcasted_iota(jnp.int32, (n_seg, LANES), 1) < HEAD_DIM

    def run(x_ref, w1, pos, b1, w2a, w2b, b2, out_ref):
        for pr in range(CMP_STRIDE // 2):
            a = x_ref[pl.ds(2 * pr, n_seg, stride=CMP_STRIDE), :]
            b = x_ref[pl.ds(2 * pr + 1, n_seg, stride=CMP_STRIDE), :]
            u_scr[0, :, pr * LANES:(pr + 1) * LANES] = jnp.where(lo, a, pltpu.roll(b, HEAD_DIM, 1)).astype(BF16)
            u_scr[1, :, pr * LANES:(pr + 1) * LANES] = jnp.where(lo, pltpu.roll(a, HEAD_DIM, 1), b).astype(BF16)
        out_ref[0] = _compress_core(u_scr[0], u_scr[1], w1, pos, b1, w2a, w2b, b2, n_seg)

    run(k_ref, w1k_ref, posk_ref, b1k_ref, w2ak_ref, w2bk_ref, b2k_ref, kc_ref)
    run(v_ref, w1v_ref, posv_ref, b1v_ref, w2av_ref, w2bv_ref, b2v_ref, vc_ref)


def _cmp_weight_specs(const):
    return ([pl.BlockSpec((1024, 256), const)] * 2 + [pl.BlockSpec((2, 1024), const)] * 2
            + [pl.BlockSpec((1, 128), const)] * 2 + [pl.BlockSpec((128, 128), const)] * 4
            + [pl.BlockSpec((1, 128), const)] * 2)


def _compress_prompt(kv, cw, batch, seq):
    n_seg = seq // CMP_STRIDE
    const = lambda b: (0, 0)
    return pl.pallas_call(
        _cmp_prompt_kernel,
        grid=(batch,),
        in_specs=[pl.BlockSpec((seq, LANES), lambda b: (b, 0)), pl.BlockSpec((seq, LANES), lambda b: (b, 1))]
        + _cmp_weight_specs(const),
        out_specs=[pl.BlockSpec((1, n_seg, LANES), lambda b: (b, 0, 0))] * 2,
        out_shape=[jax.ShapeDtypeStruct((batch, n_seg, LANES), F32)] * 2,
        scratch_shapes=[pltpu.VMEM((2, n_seg, 1024), BF16)],
        compiler_params=pltpu.CompilerParams(dimension_semantics=("arbitrary",), vmem_limit_bytes=VMEM_LIMIT),
        name="nsa_cmp_prompt",
    )(kv, kv, *cw)


def _rank_row(score, blk, j, rj):
    lower = jnp.where(blk < j, 1.0, 0.0)
    beat = jnp.where(score > rj, 1.0, jnp.where(score == rj, lower, 0.0))
    return jnp.sum(beat, axis=0, keepdims=True)


def _scores_t(imp_t, blk, t):
    cur = t >> 6
    forced = (blk == 0) | (blk == cur) | (blk == cur - 1)
    visible = blk * SLC_BLOCK <= t
    return jnp.where(forced, FORCE_SCORE, jnp.where(visible, imp_t, -jnp.inf)), visible


def _importance_t(cov_t, psum):
    hi = psum.astype(BF16)
    lo_ = (psum - hi.astype(F32)).astype(BF16)
    c = cov_t.astype(BF16)
    return _nt(c, hi) + _nt(c, lo_)


def _nsa_prompt_kernel(q_ref, g_ref, kc_ref, vc_ref, kvw_ref, covt_ref, o_ref, m_scr, l_scr, acc_scr):
    qb = pl.program_id(1)
    t0 = qb * 128
    lane = lax.broadcasted_iota(jnp.int32, (128, LANES), 1)
    lo = lane < HEAD_DIM
    t_col = t0 + lax.broadcasted_iota(jnp.int32, (128, 1), 0)
    q = q_ref[...]
    qm = [[jnp.where(lo if k == 0 else jnp.logical_not(lo), q[:, g * LANES:(g + 1) * LANES], 0.0).astype(BF16)
           for g in range(GQA)] for k in range(N_KVH)]

    kc = kc_ref[0].astype(BF16)
    vc = vc_ref[0].astype(BF16)
    n_c = kc.shape[0]
    n_iota = lax.broadcasted_iota(jnp.int32, (128, n_c), 1)
    mask_c = n_iota * CMP_STRIDE + (CMP_BLOCK - 1) <= t_col
    o_cmp = [[None] * GQA for _ in range(N_KVH)]
    psum = [None] * N_KVH
    for k in range(N_KVH):
        for g in range(GQA):
            p = _softmax_rows(_nt(qm[k][g], kc), mask_c)
            o_cmp[k][g] = _nn(p.astype(BF16), vc)
            psum[k] = p if g == 0 else psum[k] + p

    n_s = covt_ref.shape[0]
    blk = lax.broadcasted_iota(jnp.int32, (n_s, 128), 0)
    t_lane = t0 + lax.broadcasted_iota(jnp.int32, (n_s, 128), 1)
    sel = []
    for k in range(N_KVH):
        score, visible = _scores_t(_importance_t(covt_ref[...], psum[k]), blk, t_lane)
        cnt = jnp.concatenate([_rank_row(score, blk, j, score[j:j + 1, :]) for j in range(n_s)], axis=0)
        m_t = jnp.where((cnt < N_SELECT) & visible, 1.0, 0.0)
        m_t = jnp.concatenate([m_t, jnp.zeros((128 - n_s, 128), F32)], axis=0)
        sel.append(m_t.T.astype(BF16))

    m_scr[...] = jnp.full(m_scr.shape, NEG, F32)
    l_scr[...] = jnp.zeros(l_scr.shape, F32)
    acc_scr[...] = jnp.zeros(acc_scr.shape, F32)
    tk = 256

    def body(kt, carry):
        k0 = pl.multiple_of(kt * tk, tk)
        k_t = kvw_ref[pl.ds(k0, tk), 0:128]
        v_t = kvw_ref[pl.ds(k0, tk), 128:256]
        jj = lax.broadcasted_iota(jnp.int32, (128, tk), 0)
        cc = lax.broadcasted_iota(jnp.int32, (128, tk), 1)
        e_t = jnp.where(jj == kt * (tk // SLC_BLOCK) + (cc >> 6), 1.0, 0.0).astype(BF16)
        causal = (k0 + cc) <= t_col
        for k in range(N_KVH):
            allowed = (_nn(sel[k], e_t) > 0.5) & causal
            bias = jnp.where(allowed, 0.0, NEG)
            for g in range(GQA):
                h = k * GQA + g
                s = _nt(qm[k][g], k_t) + bias
                m_old = m_scr[h]
                m_new = jnp.maximum(m_old, jnp.max(s, axis=-1, keepdims=True))
                alpha = jnp.exp(m_old - m_new)
                p = jnp.exp(s - m_new)
                l_scr[h] = alpha * l_scr[h] + jnp.sum(p, axis=-1, keepdims=True)
                acc_scr[h] = alpha * acc_scr[h] + _nn(p.astype(BF16), v_t)
                m_scr[h] = m_new
        return carry

    lax.fori_loop(0, qb // 2 + 1, body, 0)

    s0 = pl.multiple_of(jnp.maximum(t0 - WINDOW, 0), 128)
    wk = kvw_ref[pl.ds(s0, WINDOW + 128), 256:384]
    wv = kvw_ref[pl.ds(s0, WINDOW + 128), 384:512]
    kpos = s0 + lax.broadcasted_iota(jnp.int32, (128, WINDOW + 128), 1)
    mask_w = (kpos <= t_col) & (kpos > t_col - WINDOW)

    gt = g_ref[...]
    for c in range(GQA):
        halves = []
        for k in range(N_KVH):
            h = k * GQA + c
            o_slc = acc_scr[h] / l_scr[h]
            pw = _softmax_rows(_nt(qm[k][c], wk), mask_w)
            o_win = _nn(pw.astype(BF16), wv)
            halves.append(gt[:, h:h + 1] * o_cmp[k][c] + gt[:, 8 + h:9 + h] * o_slc + gt[:, 16 + h:17 + h] * o_win)
        o_ref[:, c * LANES:(c + 1) * LANES] = jnp.where(lo, halves[0], halves[1]).astype(BF16)


def _nsa_prompt(q, gate, kc, vc, kvw, cov_t, batch, seq):
    n_qb = seq // 128
    n_seg = kc.shape[1]
    row = lambda b, i: (b * n_qb + i, 0)
    return pl.pallas_call(
        _nsa_prompt_kernel,
        grid=(batch, n_qb),
        in_specs=[pl.BlockSpec((128, 512), row), pl.BlockSpec((128, 128), row),
                  pl.BlockSpec((1, n_seg, LANES), lambda b, i: (b, 0, 0)),
                  pl.BlockSpec((1, n_seg, LANES), lambda b, i: (b, 0, 0)),
                  pl.BlockSpec((seq, 512), lambda b, i: (b, 0)),
                  pl.BlockSpec(cov_t.shape, lambda b, i: (0, 0))],
        out_specs=pl.BlockSpec((128, 512), row),
        out_shape=jax.ShapeDtypeStruct((batch * seq, 512), BF16),
        scratch_shapes=[pltpu.VMEM((N_HEADS, 128, 1), F32), pltpu.VMEM((N_HEADS, 128, 1), F32),
                        pltpu.VMEM((N_HEADS, 128, LANES), F32)],
        compiler_params=pltpu.CompilerParams(dimension_semantics=("arbitrary", "arbitrary"),
                                             vmem_limit_bytes=VMEM_LIMIT),
        name="nsa_prompt_attn",
    )(q, gate, kc, vc, kvw, cov_t)


def _stack_queries(q, lo):
    rows = [jnp.where(lo if k == 0 else jnp.logical_not(lo), q[:, g * LANES:(g + 1) * LANES], 0.0)
            for k in range(N_KVH) for g in range(GQA)]
    return jnp.concatenate(rows, axis=0).astype(BF16)


def _smp_cmp_kernel(n_pages, pt_ref, *refs):
    pages = refs[:n_pages]
    (q_ref, perm_ref, w1k_ref, w1v_ref, posk_ref, posv_ref, b1k_ref, b1v_ref, w2ak_ref, w2bk_ref, w2av_ref,
     w2bv_ref, b2k_ref, b2v_ref, covt_ref, ocmp_ref, msel_ref, u_scr, score_scr, cnt_scr) = refs[n_pages:]
    seg_pp = PAGE // CMP_STRIDE
    n_seg = n_pages * seg_pp
    past = n_pages * PAGE

    perm = perm_ref[...]
    lo16 = lax.broadcasted_iota(jnp.int32, (2 * seg_pp, LANES), 1) < HEAD_DIM
    for pp in range(n_pages // 2):
        for st in range(2):
            ra = _nt(perm, pages[2 * pp][st].astype(BF16))
            rb = _nt(perm, pages[2 * pp + 1][st].astype(BF16))
            for pr in range(CMP_STRIDE // 2):
                r0 = 2 * pr * seg_pp
                a = jnp.concatenate([ra[r0:r0 + seg_pp], rb[r0:r0 + seg_pp]], axis=0)
                b = jnp.concatenate([ra[r0 + seg_pp:r0 + 2 * seg_pp], rb[r0 + seg_pp:r0 + 2 * seg_pp]], axis=0)
                rows = slice(pp * 2 * seg_pp, (pp + 1) * 2 * seg_pp)
                cols = slice(pr * LANES, (pr + 1) * LANES)
                u_scr[st * 2, rows, cols] = jnp.where(lo16, a, pltpu.roll(b, HEAD_DIM, 1)).astype(BF16)
                u_scr[st * 2 + 1, rows, cols] = jnp.where(lo16, pltpu.roll(a, HEAD_DIM, 1), b).astype(BF16)

    def lhs(st, k):
        return u_scr[st * 2 + k]

    kc = _compress_core(lhs(0, 0), lhs(0, 1), w1k_ref, posk_ref, b1k_ref, w2ak_ref, w2bk_ref, b2k_ref, n_seg)
    vc = _compress_core(lhs(1, 0), lhs(1, 1), w1v_ref, posv_ref, b1v_ref, w2av_ref, w2bv_ref, b2v_ref, n_seg)

    q = q_ref[...]
    n_q = q.shape[0]
    lo8 = lax.broadcasted_iota(jnp.int32, (n_q, LANES), 1) < HEAD_DIM
    q_all = _stack_queries(q, lo8)
    rows = lax.broadcasted_iota(jnp.int32, (N_HEADS * n_q, n_seg), 0)
    t_row = past + (rows & (n_q - 1))
    n_iota = lax.broadcasted_iota(jnp.int32, (N_HEADS * n_q, n_seg), 1)
    p = _softmax_rows(_nt(q_all, kc.astype(BF16)), n_iota * CMP_STRIDE + (CMP_BLOCK - 1) <= t_row)
    o = _nn(p.astype(BF16), vc.astype(BF16))
    half = GQA * n_q
    for c in range(GQA):
        ocmp_ref[:, c * LANES:(c + 1) * LANES] = jnp.where(lo8, o[c * n_q:(c + 1) * n_q],
                                                         o[half + c * n_q:half + (c + 1) * n_q])
    psum = [sum(p[k * half + g * n_q:k * half + (g + 1) * n_q] for g in range(GQA)) for k in range(N_KVH)]
    psum = jnp.concatenate(psum + [jnp.zeros((LANES - N_KVH * n_q, n_seg), F32)], axis=0)

    n_sp = covt_ref.shape[0]
    n_s = past // SLC_BLOCK + 1
    rows_used = -(-n_s // 8) * 8
    blk = lax.broadcasted_iota(jnp.int32, (n_sp, LANES), 0)
    t_lane = past + (lax.broadcasted_iota(jnp.int32, (n_sp, LANES), 1) & (n_q - 1))
    score, visible = _scores_t(_importance_t(covt_ref[...], psum), blk, t_lane)
    score_scr[...] = score
    cnt_scr[...] = jnp.full(cnt_scr.shape, float(n_sp), F32)
    blk_u = blk[0:rows_used]

    def rank(j, carry):
        rj = score_scr[pl.ds(j, 1), :]
        cnt_scr[pl.ds(j, 1), :] = _rank_row(score_scr[0:rows_used, :], blk_u, j, rj)
        return carry

    lax.fori_loop(0, n_s, rank, 0)
    m_t = jnp.where((cnt_scr[...] < N_SELECT) & visible, 1.0, 0.0)
    msel_ref[...] = m_t.T[0:N_KVH * n_q, :]


def _page_specs(n_pages, half):
    def spec(p):
        return pl.BlockSpec((None, 2, LANES, PAGE), lambda b, pt: (pt[b * n_pages + p], half, 0, 0))
    return [spec(p) for p in range(n_pages)]


def _segment_perm():
    r = np.arange(PAGE)
    m = np.zeros((PAGE, PAGE), np.float32)
    m[r, (r % (PAGE // CMP_STRIDE)) * CMP_STRIDE + r // (PAGE // CMP_STRIDE)] = 1.0
    return jnp.asarray(m).astype(BF16)


def _sample_compress(pt, cache, q_s, cw, cov_t):
    n_seq, n_q, _ = q_s.shape
    n_pages = pt.shape[0] // n_seq
    n_seg = n_pages * (PAGE // CMP_STRIDE)
    n_sp = cov_t.shape[0]
    const = lambda b, pt: (0, 0)
    kern = functools.partial(_smp_cmp_kernel, n_pages)
    return pl.pallas_call(
        kern,
        grid_spec=pltpu.PrefetchScalarGridSpec(
            num_scalar_prefetch=1, grid=(n_seq,),
            in_specs=_page_specs(n_pages, 0)
            + [pl.BlockSpec((None, n_q, 512), lambda b, pt: (b, 0, 0)), pl.BlockSpec((PAGE, PAGE), const)]
            + _cmp_weight_specs(const) + [pl.BlockSpec(cov_t.shape, const)],
            out_specs=[pl.BlockSpec((None, n_q, 512), lambda b, pt: (b, 0, 0)),
                       pl.BlockSpec((None, N_KVH * n_q, n_sp), lambda b, pt: (b, 0, 0))],
            scratch_shapes=[pltpu.VMEM((4, n_seg, 1024), BF16),
                            pltpu.VMEM((n_sp, LANES), F32), pltpu.VMEM((n_sp, LANES), F32)]),
        out_shape=[jax.ShapeDtypeStruct((n_seq, n_q, 512), F32),
                   jax.ShapeDtypeStruct((n_seq, N_KVH * n_q, n_sp), F32)],
        compiler_params=pltpu.CompilerParams(dimension_semantics=("arbitrary",), vmem_limit_bytes=VMEM_LIMIT),
        name="nsa_sample_cmp",
    )(pt, *([cache] * n_pages), q_s, _segment_perm(), *cw, cov_t)


def _smp_attn_kernel(n_pages, pt_ref, *refs):
    pages = refs[:n_pages]
    (q_ref, msel_ref, exp_ref, kvn_ref, win_ref, wkvn_ref, g_ref, ocmp_ref, o_ref) = refs[n_pages:]
    past = n_pages * PAGE
    q = q_ref[...]
    n_q = q.shape[0]
    n_rows = N_HEADS * n_q
    lo8 = lax.broadcasted_iota(jnp.int32, (n_q, LANES), 1) < HEAD_DIM
    q_all = _stack_queries(q, lo8)
    qi = lax.broadcasted_iota(jnp.int32, (n_rows, 1), 0) & (n_q - 1)

    def pad_rows(x):
        return jnp.concatenate([x, jnp.zeros((LANES - n_q, LANES), F32)], axis=0).astype(BF16)

    def rep(x):
        return jnp.concatenate([x[0:n_q]] * GQA + [x[n_q:2 * n_q]] * GQA, axis=0)

    msel = msel_ref[...]
    n_blk = past // SLC_BLOCK
    allowed = rep(_nn(msel[:, 0:n_blk].astype(BF16), exp_ref[...])) > 0.5
    s_main = jnp.concatenate([_nn(q_all, pages[p][0].astype(BF16)) for p in range(n_pages)], axis=1)
    s_main = jnp.where(allowed, s_main, NEG)
    kvn = kvn_ref[...]
    k_tail = pad_rows(kvn[:, 256:384])
    v_tail = pad_rows(kvn[:, 384:512])
    ti = lax.broadcasted_iota(jnp.int32, (n_rows, LANES), 1)
    ok_tail = (rep(jnp.broadcast_to(msel[:, n_blk:n_blk + 1], (2 * n_q, LANES))) > 0.5) & (ti <= qi)
    s_tail = jnp.where(ok_tail, _nt(q_all, k_tail), NEG)
    m = jnp.maximum(jnp.max(s_main, axis=-1, keepdims=True), jnp.max(s_tail, axis=-1, keepdims=True))
    p_main = jnp.exp(s_main - m)
    p_tail = jnp.exp(s_tail - m)
    l = jnp.sum(p_main, axis=-1, keepdims=True) + jnp.sum(p_tail, axis=-1, keepdims=True)
    pv = _nn(p_tail.astype(BF16), v_tail)
    for p in range(n_pages):
        pv = pv + _nt(p_main[:, p * PAGE:(p + 1) * PAGE].astype(BF16), pages[p][1].astype(BF16))
    o_slc = pv / l

    wk = win_ref[0].astype(BF16)
    wv = win_ref[1].astype(BF16)
    wn = wkvn_ref[...]
    wk_tail = pad_rows(wn[:, 0:128])
    wv_tail = pad_rows(wn[:, 128:256])
    mi = lax.broadcasted_iota(jnp.int32, (n_rows, WINDOW), 1)
    sw_main = jnp.where(mi > qi, _nn(q_all, wk), NEG)
    sw_tail = jnp.where(ti <= qi, _nt(q_all, wk_tail), NEG)
    mw = jnp.maximum(jnp.max(sw_main, axis=-1, keepdims=True), jnp.max(sw_tail, axis=-1, keepdims=True))
    pw_main = jnp.exp(sw_main - mw)
    pw_tail = jnp.exp(sw_tail - mw)
    lw = jnp.sum(pw_main, axis=-1, keepdims=True) + jnp.sum(pw_tail, axis=-1, keepdims=True)
    o_win = (_nt(pw_main.astype(BF16), wv) + _nn(pw_tail.astype(BF16), wv_tail)) / lw

    gt = g_ref[...]
    ocmp = ocmp_ref[...]
    half = GQA * n_q
    for c in range(GQA):
        def pick(o):
            return jnp.where(lo8, o[c * n_q:(c + 1) * n_q], o[half + c * n_q:half + (c + 1) * n_q])

        def gate(br):
            return jnp.where(lo8, gt[:, br * 8 + c:br * 8 + c + 1], gt[:, br * 8 + GQA + c:br * 8 + GQA + c + 1])

        o_ref[:, c * LANES:(c + 1) * LANES] = (gate(0) * ocmp[:, c * LANES:(c + 1) * LANES]
                                               + gate(1) * pick(o_slc) + gate(2) * pick(o_win))


def _sample_attend(pt, cache, q_s, msel, expand, kv_new, win_state, wkv_new, gate, o_cmp):
    n_seq, n_q, _ = q_s.shape
    n_pages = pt.shape[0] // n_seq
    past = n_pages * PAGE
    n_sp = msel.shape[-1]
    seq3 = lambda b, pt: (b, 0, 0)
    kern = functools.partial(_smp_attn_kernel, n_pages)
    return pl.pallas_call(
        kern,
        grid_spec=pltpu.PrefetchScalarGridSpec(
            num_scalar_prefetch=1, grid=(n_seq,),
            in_specs=_page_specs(n_pages, 1)
            + [pl.BlockSpec((None, n_q, 512), seq3), pl.BlockSpec((None, N_KVH * n_q, n_sp), seq3),
               pl.BlockSpec(expand.shape, lambda b, pt: (0, 0)),
               pl.BlockSpec((None, n_q, 512), seq3),
               pl.BlockSpec((None, 2, LANES, WINDOW), lambda b, pt: (b, 0, 0, 0)),
               pl.BlockSpec((None, n_q, 256), seq3), pl.BlockSpec((None, n_q, 128), seq3),
               pl.BlockSpec((None, n_q, 512), seq3)],
            out_specs=pl.BlockSpec((None, n_q, 512), seq3)),
        out_shape=jax.ShapeDtypeStruct((n_seq, n_q, 512), F32),
        compiler_params=pltpu.CompilerParams(dimension_semantics=("arbitrary",), vmem_limit_bytes=VMEM_LIMIT),
        name="nsa_sample_attn",
    )(pt, *([cache] * n_pages), q_s, msel, expand, kv_new, win_state, wkv_new, gate, o_cmp)


def _ffn_kernel(x_ref, oa_ref, u_ref, v_ref, ws_ref, bs_ref, woa_ref, wob_ref, g2_ref, w1_ref, w2_ref, gf_ref,
                y_ref, h_scr, hn_scr, ob_scr, acc_scr):
    j = pl.program_id(1)
    tm = x_ref.shape[0]

    @pl.when(j == 0)
    def _():
        lo = lax.broadcasted_iota(jnp.int32, (CHUNK, LANES), 1) < HEAD_DIM
        for ch in range(tm // CHUNK):
            rows = slice(ch * CHUNK, (ch + 1) * CHUNK)
            cols = []
            for c in range(4):
                vc = v_ref[rows, c * LANES:(c + 1) * LANES].astype(BF16)
                cols.append(jnp.where(lo, _nn(ws_ref[2 * c], vc), _nn(ws_ref[2 * c + 1], vc)))
            s = jnp.concatenate(cols, axis=1) + bs_ref[...]
            ob_scr[rows, :] = (u_ref[rows, :] * s).astype(BF16)
        h = x_ref[...] + _nn(oa_ref[...].astype(BF16), woa_ref[...]) + _nn(ob_scr[...], wob_ref[...])
        h_scr[...] = h
        hn_scr[...] = _rms(h, g2_ref[...]).astype(BF16)
        acc_scr[...] = jnp.zeros(acc_scr.shape, F32)

    f = jnp.maximum(_nn(hn_scr[...], w1_ref[...]), 0.0)
    acc_scr[...] += _nn((f * f).astype(BF16), w2_ref[...])

    @pl.when(j == pl.num_programs(1) - 1)
    def _():
        y_ref[...] = _rms(h_scr[...] + acc_scr[...], gf_ref[...])


def _out_ffn(x, o_a, u, v, ws, bs, woa, wob, g2, w1, w2, gf, tm=512, tf=1024):
    n = x.shape[0]
    row = lambda i, j: (i, 0)
    const = lambda i, j: (0, 0)
    return pl.pallas_call(
        _ffn_kernel,
        grid=(n // tm, D_FF // tf),
        in_specs=[pl.BlockSpec((tm, D_MODEL), row), pl.BlockSpec((tm, 512), row), pl.BlockSpec((tm, 512), row),
                  pl.BlockSpec((tm, 512), row), pl.BlockSpec((8, CHUNK, CHUNK), lambda i, j: (0, 0, 0)),
                  pl.BlockSpec((CHUNK, 512), const), pl.BlockSpec((512, D_MODEL), const),
                  pl.BlockSpec((512, D_MODEL), const), pl.BlockSpec((1, D_MODEL), const),
                  pl.BlockSpec((D_MODEL, tf), lambda i, j: (0, j)), pl.BlockSpec((tf, D_MODEL), lambda i, j: (j, 0)),
                  pl.BlockSpec((1, D_MODEL), const)],
        out_specs=pl.BlockSpec((tm, D_MODEL), row),
        out_shape=jax.ShapeDtypeStruct((n, D_MODEL), F32),
        scratch_shapes=[pltpu.VMEM((tm, D_MODEL), F32), pltpu.VMEM((tm, D_MODEL), BF16),
                        pltpu.VMEM((tm, 512), BF16), pltpu.VMEM((tm, D_MODEL), F32)],
        compiler_params=pltpu.CompilerParams(dimension_semantics=("arbitrary", "arbitrary"),
                                             vmem_limit_bytes=VMEM_LIMIT),
        name="nsa_out_ffn",
    )(x, o_a, u, v, ws, bs, woa, wob, g2, w1, w2, gf)


def _head_perm():
    j = np.arange(512)
    return ((j // 128) + 4 * ((j % 128) // 64)) * 64 + (j % 64)


def _cover_t(n_c, n_s, n_c_pad, n_s_pad):
    ci = np.arange(n_c)[:, None] * CMP_STRIDE
    sj = np.arange(n_s)[None, :] * SLC_BLOCK
    cover = np.clip(np.minimum(ci + CMP_BLOCK, sj + SLC_BLOCK) - np.maximum(ci, sj), 0, None) / CMP_BLOCK
    out = np.zeros((n_s_pad, n_c_pad), np.float32)
    out[:n_s, :n_c] = cover.T
    return jnp.asarray(out)


def _cmp_weights(cmp_w1, cmp_b1, cmp_w2, cmp_b2, cmp_pos):
    z = jnp.zeros((CMP_HIDDEN, HEAD_DIM), F32)
    w1 = [jnp.concatenate([cmp_w1[i, :1024], cmp_w1[i, 1024:]], axis=1).astype(BF16) for i in range(2)]
    pos = [cmp_pos[i].reshape(2, 1024) for i in range(2)]
    b1 = [cmp_b1[i].reshape(1, CMP_HIDDEN) for i in range(2)]
    w2a = [jnp.concatenate([cmp_w2[i], z], axis=1).astype(BF16) for i in range(2)]
    w2b = [jnp.concatenate([z, cmp_w2[i]], axis=1).astype(BF16) for i in range(2)]
    b2 = [jnp.concatenate([cmp_b2[i], cmp_b2[i]]).reshape(1, LANES) for i in range(2)]
    return (w1[0], w1[1], pos[0], pos[1], b1[0], b1[1], w2a[0], w2b[0], w2a[1], w2b[1], b2[0], b2[1])


def kernel(x_prompt, x_sample, cache_kv, state_win_kv, page_table, ln1_g, w_in, cmp_w1, cmp_b1, cmp_w2, cmp_b2,
           cmp_pos, ln_v_g, ln_v_b, w_s, b_s, w_out, ln2_g, w_ff1, w_ff2, ln_f_g):
    batch, seq, _ = x_prompt.shape
    n_seq, n_q, _ = x_sample.shape
    n_pages = page_table.shape[1]
    past = n_pages * PAGE
    perm = _head_perm()

    wi = w_in[0]
    w_all = jnp.concatenate(
        [wi[:, 0:512][:, perm], wi[:, 512:1304], jnp.zeros((D_MODEL, C_U - C_GATE - 24), F32), wi[:, 1304:2328]],
        axis=1).astype(BF16)
    g1 = ln1_g[0].reshape(1, D_MODEL)
    lvg = ln_v_g[0].reshape(1, 512)
    lvb = ln_v_b[0].reshape(1, 512)
    cw = _cmp_weights(cmp_w1[0], cmp_b1[0], cmp_w2[0], cmp_b2[0], cmp_pos[0])
    tril = jnp.tril(jnp.ones((CHUNK, CHUNK), F32))
    ws_p = (w_s[0] * tril).astype(BF16)
    bs_p = jnp.repeat(b_s[0].T, HEAD_DIM, axis=1)
    reps = CHUNK // n_q
    ws_s = jnp.einsum("ab,gij->gaibj", jnp.eye(reps, dtype=F32), (w_s[0] * tril)[:, :n_q, :n_q])
    ws_s = ws_s.reshape(8, CHUNK, CHUNK).astype(BF16)
    bs_s = jnp.tile(jnp.repeat(b_s[0].T[:n_q], HEAD_DIM, axis=1), (reps, 1))
    woa = w_out[0][:512][perm].astype(BF16)
    wob = w_out[0][512:].astype(BF16)
    g2 = ln2_g[0].reshape(1, D_MODEL)
    gf = ln_f_g.reshape(1, D_MODEL)
    w1 = w_ff1[0].astype(BF16)
    w2 = w_ff2[0].astype(BF16)

    xp = x_prompt.reshape(batch * seq, D_MODEL)
    q_p, kv_p, wkv_p, kvw_p, gate_p, u_p, v_p = _project(xp, g1, w_all, lvg, lvb)
    kc_p, vc_p = _compress_prompt(kv_p, cw, batch, seq)
    n_seg_p = seq // CMP_STRIDE
    cov_p = _cover_t(n_seg_p - 1, seq // SLC_BLOCK, n_seg_p, seq // SLC_BLOCK)
    oa_p = _nsa_prompt(q_p, gate_p, kc_p, vc_p, kvw_p, cov_p, batch, seq)
    y_p = _out_ffn(xp, oa_p, u_p, v_p, ws_p, bs_p, woa, wob, g2, w1, w2, gf)

    xs = x_sample.reshape(n_seq * n_q, D_MODEL)
    q_s, kv_s, wkv_s, _, gate_s, u_s, v_s = _project(xs, g1, w_all, lvg, lvb)
    cache = jnp.transpose(cache_kv[0], (0, 2, 3, 4, 1)).reshape(-1, 4, LANES, PAGE)
    win_state = jnp.transpose(state_win_kv[0], (0, 2, 3, 4, 1)).reshape(n_seq, 2, LANES, -1)
    pt = page_table.reshape(-1)
    n_seg_s = past // CMP_STRIDE
    n_s = past // SLC_BLOCK + 1
    n_sp = -(-n_s // LANES) * LANES
    cov_s = _cover_t(n_seg_s, n_s, n_seg_s, n_sp)
    q_s3 = q_s.reshape(n_seq, n_q, 512)
    ocmp_s, msel = _sample_compress(pt, cache, q_s3, cw, cov_s)
    n_blk = past // SLC_BLOCK
    expand = jnp.asarray((np.arange(n_blk)[:, None] == (np.arange(past)[None, :] // SLC_BLOCK)).astype(np.float32)).astype(BF16)
    oa_s = _sample_attend(pt, cache, q_s3, msel, expand, kv_s.reshape(n_seq, n_q, 512), win_state,
                          wkv_s.reshape(n_seq, n_q, 256), gate_s.reshape(n_seq, n_q, 128), ocmp_s)
    y_s = _out_ffn(xs, oa_s.reshape(n_seq * n_q, 512), u_s, v_s, ws_s, bs_s, woa, wob, g2, w1, w2, gf)

    keep = min(WINDOW, seq)
    new_win_p = wkv_p.reshape(batch, seq, 2, N_KVH, HEAD_DIM)[:, seq - keep:]
    new_win_s = jnp.concatenate([state_win_kv[0], wkv_s.reshape(n_seq, n_q, 2, N_KVH, HEAD_DIM)], axis=1)[:, n_q:]
    return (y_p.reshape(batch, seq, D_MODEL),
            y_s.reshape(n_seq, n_q, D_MODEL),
            kv_p.reshape(1, batch, seq, 4, N_KVH, HEAD_DIM),
            kv_s.reshape(1, n_seq, n_q, 4, N_KVH, HEAD_DIM),
            new_win_p[None],
            new_win_s[None],
            v_s.reshape(1, n_seq, n_q, 512))
```

```python
import functools

import numpy as np
import jax
import jax.numpy as jnp
from jax import lax
from jax.experimental import pallas as pl
from jax.experimental.pallas import tpu as pltpu

F32 = jnp.float32
BF16 = jnp.bfloat16

D_MODEL = 1024
HEAD_DIM = 64
N_HEADS = 8
N_KVH = 2
GQA = 4
CMP_BLOCK = 32
CMP_STRIDE = 16
CMP_HIDDEN = 128
SLC_BLOCK = 64
N_SELECT = 16
WINDOW = 512
CHUNK = 128
D_FF = 4096
PAGE = 128
EPS = 1e-6
NEG = -1e30
FORCE_SCORE = 1e9
LANES = 128
VMEM_LIMIT = 56 * 1024 * 1024
SLC_TILE = 512

C_Q, C_KV, C_WIN, C_GATE, C_U, C_V, C_END = 0, 512, 1024, 1280, 1408, 1920, 2432


def _nn(a, b):
    return jnp.dot(a, b, preferred_element_type=F32)


def _nt(a, b):
    return lax.dot_general(a, b, (((1,), (1,)), ((), ())), preferred_element_type=F32)


def _gelu(x):
    return 0.5 * x * (1.0 + jnp.tanh(0.7978845608028654 * (x + 0.044715 * (x * x * x))))


def _rms(x, g):
    return x * lax.rsqrt(jnp.mean(x * x, axis=-1, keepdims=True) + EPS) * g


def _softmax_rows(s, mask):
    s = jnp.where(mask, s, NEG)
    m = jnp.max(s, axis=-1, keepdims=True)
    p = jnp.where(mask, jnp.exp(s - m), 0.0)
    l = jnp.sum(p, axis=-1, keepdims=True)
    return p / jnp.maximum(l, 1e-30)


def _proj_kernel(x_ref, g1_ref, w_ref, lvg_ref, lvb_ref,
                 q_ref, kv_ref, wkv_ref, kvw_ref, gate_ref, u_ref, v_ref):
    x = x_ref[...]
    h = _rms(x, g1_ref[...]).astype(BF16)

    def z(a, b):
        return _nn(h, w_ref[:, a:b])

    q_ref[...] = z(C_Q, C_KV) * (HEAD_DIM ** -0.5)
    kv = z(C_KV, C_WIN)
    kv_ref[...] = kv
    wkv = z(C_WIN, C_GATE)
    wkv_ref[...] = wkv
    kvw_ref[:, 0:256] = kv[:, 256:512].astype(BF16)
    kvw_ref[:, 256:512] = wkv.astype(BF16)
    gate_ref[...] = jax.nn.sigmoid(z(C_GATE, C_U))
    u_ref[...] = _gelu(z(C_U, C_V))
    zv = _gelu(z(C_V, C_END))
    mu = jnp.mean(zv, axis=-1, keepdims=True)
    var = jnp.mean(jnp.square(zv - mu), axis=-1, keepdims=True)
    v_ref[...] = (zv - mu) * lax.rsqrt(var + EPS) * lvg_ref[...] + lvb_ref[...]


def _project(x, g1, w_all, lvg, lvb, tm=512):
    n = x.shape[0]
    row = lambda i: (i, 0)
    const = lambda i: (0, 0)
    widths = (512, 512, 256, 512, 128, 512, 512)
    dtypes = (F32, F32, F32, BF16, F32, F32, F32)
    return pl.pallas_call(
        _proj_kernel,
        grid=(n // tm,),
        in_specs=[pl.BlockSpec((tm, D_MODEL), row), pl.BlockSpec((1, D_MODEL), const),
                  pl.BlockSpec((D_MODEL, C_END), const), pl.BlockSpec((1, 512), const),
                  pl.BlockSpec((1, 512), const)],
        out_specs=[pl.BlockSpec((tm, w), row) for w in widths],
        out_shape=[jax.ShapeDtypeStruct((n, w), d) for w, d in zip(widths, dtypes)],
        compiler_params=pltpu.CompilerParams(dimension_semantics=("arbitrary",), vmem_limit_bytes=VMEM_LIMIT),
        name="nsa_proj",
    )(x, g1, w_all, lvg, lvb)


def _compress_core(lhs0, lhs1, w1_ref, pos_ref, b1_ref, w2a_ref, w2b_ref, b2_ref, n_seg):
    w1 = w1_ref[...]
    pos = pos_ref[...]
    p0 = jnp.broadcast_to(pos[0:1], (8, 1024)).astype(BF16)
    p1 = jnp.broadcast_to(pos[1:2], (8, 1024)).astype(BF16)
    c1 = b1_ref[...] + _nn(p0, w1[:, 0:128])[0:1] + _nn(p1, w1[:, 128:256])[0:1]

    def hidden(lhs):
        h = _nn(lhs, w1)
        nxt = pltpu.roll(h[:, 128:256], n_seg - 1, 0)
        return _gelu(h[:, 0:128] + nxt + c1).astype(BF16)

    return _nn(hidden(lhs0), w2a_ref[...]) + _nn(hidden(lhs1), w2b_ref[...]) + b2_ref[...]


def _cmp_prompt_kernel(k_ref, v_ref, w1k_ref, w1v_ref, posk_ref, posv_ref, b1k_ref, b1v_ref,
                       w2ak_ref, w2bk_ref, w2av_ref, w2bv_ref, b2k_ref, b2v_ref,
                       kc_ref, vc_ref, u_scr):
    n_seg = k_ref.shape[0] // CMP_STRIDE
    lo = lax.broadcasted_iota(jnp.int32, (n_seg, LANES), 1) < HEAD_DIM

    def run(x_ref, w1, pos, b1, w2a, w2b, b2, out_ref):
        for pr in range(CMP_STRIDE // 2):
            a = x_ref[pl.ds(2 * pr, n_seg, stride=CMP_STRIDE), :]
            b = x_ref[pl.ds(2 * pr + 1, n_seg, stride=CMP_STRIDE), :]
            u_scr[0, :, pr * LANES:(pr + 1) * LANES] = jnp.where(lo, a, pltpu.roll(b, HEAD_DIM, 1)).astype(BF16)
            u_scr[1, :, pr * LANES:(pr + 1) * LANES] = jnp.where(lo, pltpu.roll(a, HEAD_DIM, 1), b).astype(BF16)
        out_ref[0] = _compress_core(u_scr[0], u_scr[1], w1, pos, b1, w2a, w2b, b2, n_seg)

    run(k_ref, w1k_ref, posk_ref, b1k_ref, w2ak_ref, w2bk_ref, b2k_ref, kc_ref)
    run(v_ref, w1v_ref, posv_ref, b1v_ref, w2av_ref, w2bv_ref, b2v_ref, vc_ref)


def _cmp_weight_specs(const):
    return ([pl.BlockSpec((1024, 256), const)] * 2 + [pl.BlockSpec((2, 1024), const)] * 2
            + [pl.BlockSpec((1, 128), const)] * 2 + [pl.BlockSpec((128, 128), const)] * 4
            + [pl.BlockSpec((1, 128), const)] * 2)


def _compress_prompt(kv, cw, batch, seq):
    n_seg = seq // CMP_STRIDE
    const = lambda b: (0, 0)
    return pl.pallas_call(
        _cmp_prompt_kernel,
        grid=(batch,),
        in_specs=[pl.BlockSpec((seq, LANES), lambda b: (b, 0)), pl.BlockSpec((seq, LANES), lambda b: (b, 1))]
        + _cmp_weight_specs(const),
        out_specs=[pl.BlockSpec((1, n_seg, LANES), lambda b: (b, 0, 0))] * 2,
        out_shape=[jax.ShapeDtypeStruct((batch, n_seg, LANES), F32)] * 2,
        scratch_shapes=[pltpu.VMEM((2, n_seg, 1024), BF16)],
        compiler_params=pltpu.CompilerParams(dimension_semantics=("arbitrary",), vmem_limit_bytes=VMEM_LIMIT),
        name="nsa_cmp_prompt",
    )(kv, kv, *cw)


def _rank_row(score, blk, j, rj):
    lower = jnp.where(blk < j, 1.0, 0.0)
    beat = jnp.where(score > rj, 1.0, jnp.where(score == rj, lower, 0.0))
    return jnp.sum(beat, axis=0, keepdims=True)


def _scores_t(imp_t, blk, t):
    cur = t >> 6
    forced = (blk == 0) | (blk == cur) | (blk == cur - 1)
    visible = blk * SLC_BLOCK <= t
    return jnp.where(forced, FORCE_SCORE, jnp.where(visible, imp_t, -jnp.inf)), visible


def _importance_t(cov_t, psum):
    hi = psum.astype(BF16)
    lo_ = (psum - hi.astype(F32)).astype(BF16)
    c = cov_t.astype(BF16)
    return _nt(c, hi) + _nt(c, lo_)


def _nsa_prompt_kernel(q_ref, g_ref, kc_ref, vc_ref, kvw_ref, covt_ref, ebig_ref, gexp_ref, o_ref,
                       mrun_scr, acc_scr):
    qb = pl.program_id(1)
    t0 = qb * 128
    lane = lax.broadcasted_iota(jnp.int32, (128, LANES), 1)
    lo = lane < HEAD_DIM
    t_col = t0 + lax.broadcasted_iota(jnp.int32, (128, 1), 0)
    q = q_ref[...]
    qm = [[jnp.where(lo if k == 0 else jnp.logical_not(lo), q[:, g * LANES:(g + 1) * LANES], 0.0).astype(BF16)
           for g in range(GQA)] for k in range(N_KVH)]

    kc = kc_ref[0].astype(BF16)
    vc = vc_ref[0].astype(BF16)
    n_c = kc.shape[0]
    n_iota = lax.broadcasted_iota(jnp.int32, (128, n_c), 1)
    mask_c = n_iota * CMP_STRIDE + (CMP_BLOCK - 1) <= t_col
    o_cmp = [[None] * GQA for _ in range(N_KVH)]
    psum = [None] * N_KVH
    for k in range(N_KVH):
        for g in range(GQA):
            p = _softmax_rows(_nt(qm[k][g], kc), mask_c)
            o_cmp[k][g] = _nn(p.astype(BF16), vc)
            psum[k] = p if g == 0 else psum[k] + p

    n_s = covt_ref.shape[0]
    blk = lax.broadcasted_iota(jnp.int32, (n_s, 128), 0)
    t_lane = t0 + lax.broadcasted_iota(jnp.int32, (n_s, 128), 1)
    nsel = []
    for k in range(N_KVH):
        score, visible = _scores_t(_importance_t(covt_ref[...], psum[k]), blk, t_lane)
        cnt = jnp.concatenate([_rank_row(score, blk, j, score[j:j + 1, :]) for j in range(n_s)], axis=0)
        m_t = jnp.where((cnt < N_SELECT) & visible, 1.0, 0.0)
        m_t = jnp.concatenate([m_t, jnp.zeros((128 - n_s, 128), F32)], axis=0)
        nsel.append((1.0 - m_t.T).astype(BF16))

    lhs = [[jnp.concatenate([qm[k][g], nsel[k]], axis=1) for g in range(GQA)] for k in range(N_KVH)]
    mrun_scr[...] = jnp.full(mrun_scr.shape, NEG, F32)
    acc_scr[...] = jnp.zeros(acc_scr.shape, F32)
    tk = SLC_TILE
    cc = lax.broadcasted_iota(jnp.int32, (128, tk), 1)
    n_full = qb // (tk // 128)

    def rhs_tile(k0):
        return jnp.concatenate([kvw_ref[pl.ds(k0, tk), 0:128], ebig_ref[pl.ds(k0, tk), :]], axis=1)

    def scores(k0, rhs, causal):
        out = []
        for k in range(N_KVH):
            for g in range(GQA):
                s = _nt(lhs[k][g], rhs)
                out.append(jnp.where((k0 + cc) <= t_col, s, NEG) if causal else s)
        return out

    def max_tile(k0, causal):
        for h, s in enumerate(scores(k0, rhs_tile(k0), causal)):
            m = s[:, 0:LANES]
            for j in range(1, tk // LANES):
                m = jnp.maximum(m, s[:, j * LANES:(j + 1) * LANES])
            mrun_scr[h] = jnp.maximum(mrun_scr[h], m)

    def acc_tile(k0, causal, mb):
        v_t = kvw_ref[pl.ds(k0, tk), 128:256]
        lov = lax.broadcasted_iota(jnp.int32, (tk, LANES), 1) < HEAD_DIM
        v1 = [jnp.where(lov, v_t, 1.0).astype(BF16), jnp.where(lov, 1.0, v_t).astype(BF16)]
        for h, s in enumerate(scores(k0, rhs_tile(k0), causal)):
            p = jnp.exp(s - jnp.concatenate([mb[h]] * (tk // LANES), axis=1))
            acc_scr[h] += _nn(p.astype(BF16), v1[h // GQA])

    def loop1(kt, carry):
        max_tile(pl.multiple_of(kt * tk, tk), False)
        return carry

    lax.fori_loop(0, n_full, loop1, 0)
    k_last = pl.multiple_of(n_full * tk, tk)
    max_tile(k_last, True)
    mb = [jnp.broadcast_to(jnp.max(mrun_scr[h], axis=-1, keepdims=True), (128, LANES)) for h in range(N_HEADS)]

    def loop2(kt, carry):
        acc_tile(pl.multiple_of(kt * tk, tk), False, mb)
        return carry

    lax.fori_loop(0, n_full, loop2, 0)
    acc_tile(k_last, True, mb)

    s0 = pl.multiple_of(jnp.maximum(t0 - WINDOW, 0), 128)
    wk = kvw_ref[pl.ds(s0, WINDOW + 128), 256:384]
    wv = kvw_ref[pl.ds(s0, WINDOW + 128), 384:512]
    low = lax.broadcasted_iota(jnp.int32, (WINDOW + 128, LANES), 1) < HEAD_DIM
    wv1 = [jnp.where(low, wv, 1.0).astype(BF16), jnp.where(low, 1.0, wv).astype(BF16)]
    kpos = s0 + lax.broadcasted_iota(jnp.int32, (128, WINDOW + 128), 1)
    mask_w = (kpos <= t_col) & (kpos > t_col - WINDOW)

    def window(k, g):
        s = jnp.where(mask_w, _nt(qm[k][g], wk), NEG)
        p = jnp.exp(s - jnp.max(s, axis=-1, keepdims=True))
        return _nn(p.astype(BF16), wv1[k])

    def normalised(n0, n1):
        return jnp.where(lo, n0, n1) / pltpu.roll(jnp.where(lo, n1, n0), HEAD_DIM, 1)

    gt = g_ref[...]
    g_hi = gt.astype(BF16)
    g_lo = (gt - g_hi.astype(F32)).astype(BF16)
    gexp = _nn(g_hi, gexp_ref[...]) + _nn(g_lo, gexp_ref[...])
    for c in range(GQA):
        cols = slice(c * LANES, (c + 1) * LANES)
        o_c = jnp.where(lo, o_cmp[0][c], o_cmp[1][c])
        o_s = normalised(acc_scr[c], acc_scr[GQA + c])
        o_w = normalised(window(0, c), window(1, c))
        o_ref[:, cols] = (gexp[:, cols] * o_c + gexp[:, 512 + c * LANES:512 + (c + 1) * LANES] * o_s
                          + gexp[:, 1024 + c * LANES:1024 + (c + 1) * LANES] * o_w).astype(BF16)


def _block_bias(seq):
    m = (np.arange(seq)[:, None] // SLC_BLOCK == np.arange(LANES)[None, :]).astype(np.float32) * NEG
    return jnp.asarray(m).astype(BF16)


def _gate_expand():
    m = np.zeros((LANES, 3 * 512), np.float32)
    col = np.arange(512)
    head = col // LANES + GQA * ((col % LANES) // HEAD_DIM)
    for br in range(3):
        m[br * N_HEADS + head, br * 512 + col] = 1.0
    return jnp.asarray(m).astype(BF16)


def _nsa_prompt(q, gate, kc, vc, kvw, cov_t, batch, seq):
    n_qb = seq // 128
    n_seg = kc.shape[1]
    row = lambda b, i: (b * n_qb + i, 0)
    const = lambda b, i: (0, 0)
    return pl.pallas_call(
        _nsa_prompt_kernel,
        grid=(batch, n_qb),
        in_specs=[pl.BlockSpec((128, 512), row), pl.BlockSpec((128, 128), row),
                  pl.BlockSpec((1, n_seg, LANES), lambda b, i: (b, 0, 0)),
                  pl.BlockSpec((1, n_seg, LANES), lambda b, i: (b, 0, 0)),
                  pl.BlockSpec((seq, 512), lambda b, i: (b, 0)),
                  pl.BlockSpec(cov_t.shape, const), pl.BlockSpec((seq, LANES), const),
                  pl.BlockSpec((LANES, 3 * 512), const)],
        out_specs=pl.BlockSpec((128, 512), row),
        out_shape=jax.ShapeDtypeStruct((batch * seq, 512), BF16),
        scratch_shapes=[pltpu.VMEM((N_HEADS, 128, LANES), F32), pltpu.VMEM((N_HEADS, 128, LANES), F32)],
        compiler_params=pltpu.CompilerParams(dimension_semantics=("arbitrary", "arbitrary"),
                                             vmem_limit_bytes=VMEM_LIMIT),
        name="nsa_prompt_attn",
    )(q, gate, kc, vc, kvw, cov_t, _block_bias(seq), _gate_expand())


def _stack_queries(q, lo):
    rows = [jnp.where(lo if k == 0 else jnp.logical_not(lo), q[:, g * LANES:(g + 1) * LANES], 0.0)
            for k in range(N_KVH) for g in range(GQA)]
    return jnp.concatenate(rows, axis=0).astype(BF16)


def _smp_cmp_kernel(n_pages, pt_ref, *refs):
    pages = refs[:n_pages]
    (q_ref, perm_ref, w1k_ref, w1v_ref, posk_ref, posv_ref, b1k_ref, b1v_ref, w2ak_ref, w2bk_ref, w2av_ref,
     w2bv_ref, b2k_ref, b2v_ref, covt_ref, ocmp_ref, msel_ref, u_scr) = refs[n_pages:]
    seg_pp = PAGE // CMP_STRIDE
    n_seg = n_pages * seg_pp
    past = n_pages * PAGE

    perm = perm_ref[...]
    lo16 = lax.broadcasted_iota(jnp.int32, (2 * seg_pp, LANES), 1) < HEAD_DIM
    for pp in range(n_pages // 2):
        for st in range(2):
            ra = _nt(perm, pages[2 * pp][st].astype(BF16))
            rb = _nt(perm, pages[2 * pp + 1][st].astype(BF16))
            for pr in range(CMP_STRIDE // 2):
                r0 = 2 * pr * seg_pp
                a = jnp.concatenate([ra[r0:r0 + seg_pp], rb[r0:r0 + seg_pp]], axis=0)
                b = jnp.concatenate([ra[r0 + seg_pp:r0 + 2 * seg_pp], rb[r0 + seg_pp:r0 + 2 * seg_pp]], axis=0)
                rows = slice(pp * 2 * seg_pp, (pp + 1) * 2 * seg_pp)
                cols = slice(pr * LANES, (pr + 1) * LANES)
                u_scr[st * 2, rows, cols] = jnp.where(lo16, a, pltpu.roll(b, HEAD_DIM, 1)).astype(BF16)
                u_scr[st * 2 + 1, rows, cols] = jnp.where(lo16, pltpu.roll(a, HEAD_DIM, 1), b).astype(BF16)

    def lhs(st, k):
        return u_scr[st * 2 + k]

    kc = _compress_core(lhs(0, 0), lhs(0, 1), w1k_ref, posk_ref, b1k_ref, w2ak_ref, w2bk_ref, b2k_ref, n_seg)
    vc = _compress_core(lhs(1, 0), lhs(1, 1), w1v_ref, posv_ref, b1v_ref, w2av_ref, w2bv_ref, b2v_ref, n_seg)

    q = q_ref[...]
    n_q = q.shape[0]
    lo8 = lax.broadcasted_iota(jnp.int32, (n_q, LANES), 1) < HEAD_DIM
    q_all = _stack_queries(q, lo8)
    rows = lax.broadcasted_iota(jnp.int32, (N_HEADS * n_q, n_seg), 0)
    t_row = past + (rows & (n_q - 1))
    n_iota = lax.broadcasted_iota(jnp.int32, (N_HEADS * n_q, n_seg), 1)
    p = _softmax_rows(_nt(q_all, kc.astype(BF16)), n_iota * CMP_STRIDE + (CMP_BLOCK - 1) <= t_row)
    o = _nn(p.astype(BF16), vc.astype(BF16))
    half = GQA * n_q
    for c in range(GQA):
        ocmp_ref[:, c * LANES:(c + 1) * LANES] = jnp.where(lo8, o[c * n_q:(c + 1) * n_q],
                                                         o[half + c * n_q:half + (c + 1) * n_q])
    psum = [sum(p[k * half + g * n_q:k * half + (g + 1) * n_q] for g in range(GQA)) for k in range(N_KVH)]
    psum = jnp.concatenate(psum + [jnp.zeros((LANES - N_KVH * n_q, n_seg), F32)], axis=0)

    n_sp = covt_ref.shape[0]
    n_s = past // SLC_BLOCK + 1
    rows_used = -(-n_s // 8) * 8
    n_r = N_KVH * n_q
    blk_t = lax.broadcasted_iota(jnp.int32, (n_sp, LANES), 0)
    t_lane = past + (lax.broadcasted_iota(jnp.int32, (n_sp, LANES), 1) & (n_q - 1))
    score_t, visible_t = _scores_t(_importance_t(covt_ref[...], psum), blk_t, t_lane)
    score = score_t.T[0:n_r]
    visible = jnp.where(visible_t, 1.0, 0.0).T[0:n_r] > 0.5
    ii = lax.broadcasted_iota(jnp.int32, (rows_used, n_sp), 0)
    jj = lax.broadcasted_iota(jnp.int32, (rows_used, n_sp), 1)
    lower = jnp.where(ii < jj, 1.0, 0.0)
    cnt = []
    for r in range(n_r):
        col = score_t[0:rows_used, r:r + 1]
        row = score[r:r + 1, :]
        beat = jnp.where(col > row, 1.0, jnp.where(col == row, lower, 0.0))
        cnt.append(jnp.sum(beat, axis=0, keepdims=True))
    cnt = jnp.concatenate(cnt, axis=0)
    msel_ref[...] = jnp.where((cnt < N_SELECT) & visible, 1.0, 0.0)


def _page_specs(n_pages, half):
    def spec(p):
        return pl.BlockSpec((None, 2, LANES, PAGE), lambda b, pt: (pt[b * n_pages + p], half, 0, 0))
    return [spec(p) for p in range(n_pages)]


def _segment_perm():
    r = np.arange(PAGE)
    m = np.zeros((PAGE, PAGE), np.float32)
    m[r, (r % (PAGE // CMP_STRIDE)) * CMP_STRIDE + r // (PAGE // CMP_STRIDE)] = 1.0
    return jnp.asarray(m).astype(BF16)


def _sample_compress(pt, cache, q_s, cw, cov_t):
    n_seq, n_q, _ = q_s.shape
    n_pages = pt.shape[0] // n_seq
    n_seg = n_pages * (PAGE // CMP_STRIDE)
    n_sp = cov_t.shape[0]
    const = lambda b, pt: (0, 0)
    kern = functools.partial(_smp_cmp_kernel, n_pages)
    return pl.pallas_call(
        kern,
        grid_spec=pltpu.PrefetchScalarGridSpec(
            num_scalar_prefetch=1, grid=(n_seq,),
            in_specs=_page_specs(n_pages, 0)
            + [pl.BlockSpec((None, n_q, 512), lambda b, pt: (b, 0, 0)), pl.BlockSpec((PAGE, PAGE), const)]
            + _cmp_weight_specs(const) + [pl.BlockSpec(cov_t.shape, const)],
            out_specs=[pl.BlockSpec((None, n_q, 512), lambda b, pt: (b, 0, 0)),
                       pl.BlockSpec((None, N_KVH * n_q, n_sp), lambda b, pt: (b, 0, 0))],
            scratch_shapes=[pltpu.VMEM((4, n_seg, 1024), BF16)]),
        out_shape=[jax.ShapeDtypeStruct((n_seq, n_q, 512), F32),
                   jax.ShapeDtypeStruct((n_seq, N_KVH * n_q, n_sp), F32)],
        compiler_params=pltpu.CompilerParams(dimension_semantics=("arbitrary",), vmem_limit_bytes=VMEM_LIMIT),
        name="nsa_sample_cmp",
    )(pt, *([cache] * n_pages), q_s, _segment_perm(), *cw, cov_t)


def _smp_attn_kernel(n_pages, pt_ref, *refs):
    pages = refs[:n_pages]
    (q_ref, msel_ref, exp_ref, kvn_ref, win_ref, wkvn_ref, g_ref, ocmp_ref, o_ref) = refs[n_pages:]
    past = n_pages * PAGE
    q = q_ref[...]
    n_q = q.shape[0]
    n_rows = N_HEADS * n_q
    lo8 = lax.broadcasted_iota(jnp.int32, (n_q, LANES), 1) < HEAD_DIM
    q_all = _stack_queries(q, lo8)
    qi = lax.broadcasted_iota(jnp.int32, (n_rows, 1), 0) & (n_q - 1)

    def pad_rows(x):
        return jnp.concatenate([x, jnp.zeros((LANES - n_q, LANES), F32)], axis=0).astype(BF16)

    def rep(x):
        return jnp.concatenate([x[0:n_q]] * GQA + [x[n_q:2 * n_q]] * GQA, axis=0)

    msel = msel_ref[...]
    n_blk = past // SLC_BLOCK
    allowed = rep(_nn(msel[:, 0:n_blk].astype(BF16), exp_ref[...])) > 0.5
    s_main = jnp.concatenate([_nn(q_all, pages[p][0].astype(BF16)) for p in range(n_pages)], axis=1)
    s_main = jnp.where(allowed, s_main, NEG)
    kvn = kvn_ref[...]
    k_tail = pad_rows(kvn[:, 256:384])
    v_tail = pad_rows(kvn[:, 384:512])
    ti = lax.broadcasted_iota(jnp.int32, (n_rows, LANES), 1)
    ok_tail = (rep(jnp.broadcast_to(msel[:, n_blk:n_blk + 1], (2 * n_q, LANES))) > 0.5) & (ti <= qi)
    s_tail = jnp.where(ok_tail, _nt(q_all, k_tail), NEG)
    m = jnp.maximum(jnp.max(s_main, axis=-1, keepdims=True), jnp.max(s_tail, axis=-1, keepdims=True))
    p_main = jnp.exp(s_main - m)
    p_tail = jnp.exp(s_tail - m)
    l = jnp.sum(p_main, axis=-1, keepdims=True) + jnp.sum(p_tail, axis=-1, keepdims=True)
    pv = _nn(p_tail.astype(BF16), v_tail)
    for p in range(n_pages):
        pv = pv + _nt(p_main[:, p * PAGE:(p + 1) * PAGE].astype(BF16), pages[p][1].astype(BF16))
    o_slc = pv / l

    wk = win_ref[0].astype(BF16)
    wv = win_ref[1].astype(BF16)
    wn = wkvn_ref[...]
    wk_tail = pad_rows(wn[:, 0:128])
    wv_tail = pad_rows(wn[:, 128:256])
    mi = lax.broadcasted_iota(jnp.int32, (n_rows, WINDOW), 1)
    sw_main = jnp.where(mi > qi, _nn(q_all, wk), NEG)
    sw_tail = jnp.where(ti <= qi, _nt(q_all, wk_tail), NEG)
    mw = jnp.maximum(jnp.max(sw_main, axis=-1, keepdims=True), jnp.max(sw_tail, axis=-1, keepdims=True))
    pw_main = jnp.exp(sw_main - mw)
    pw_tail = jnp.exp(sw_tail - mw)
    lw = jnp.sum(pw_main, axis=-1, keepdims=True) + jnp.sum(pw_tail, axis=-1, keepdims=True)
    o_win = (_nt(pw_main.astype(BF16), wv) + _nn(pw_tail.astype(BF16), wv_tail)) / lw

    gt = g_ref[...]
    ocmp = ocmp_ref[...]
    half = GQA * n_q
    for c in range(GQA):
        def pick(o):
            return jnp.where(lo8, o[c * n_q:(c + 1) * n_q], o[half + c * n_q:half + (c + 1) * n_q])

        def gate(br):
            return jnp.where(lo8, gt[:, br * 8 + c:br * 8 + c + 1], gt[:, br * 8 + GQA + c:br * 8 + GQA + c + 1])

        o_ref[:, c * LANES:(c + 1) * LANES] = (gate(0) * ocmp[:, c * LANES:(c + 1) * LANES]
                                               + gate(1) * pick(o_slc) + gate(2) * pick(o_win))


def _sample_attend(pt, cache, q_s, msel, expand, kv_new, win_state, wkv_new, gate, o_cmp):
    n_seq, n_q, _ = q_s.shape
    n_pages = pt.shape[0] // n_seq
    past = n_pages * PAGE
    n_sp = msel.shape[-1]
    seq3 = lambda b, pt: (b, 0, 0)
    kern = functools.partial(_smp_attn_kernel, n_pages)
    return pl.pallas_call(
        kern,
        grid_spec=pltpu.PrefetchScalarGridSpec(
            num_scalar_prefetch=1, grid=(n_seq,),
            in_specs=_page_specs(n_pages, 1)
            + [pl.BlockSpec((None, n_q, 512), seq3), pl.BlockSpec((None, N_KVH * n_q, n_sp), seq3),
               pl.BlockSpec(expand.shape, lambda b, pt: (0, 0)),
               pl.BlockSpec((None, n_q, 512), seq3),
               pl.BlockSpec((None, 2, LANES, WINDOW), lambda b, pt: (b, 0, 0, 0)),
               pl.BlockSpec((None, n_q, 256), seq3), pl.BlockSpec((None, n_q, 128), seq3),
               pl.BlockSpec((None, n_q, 512), seq3)],
            out_specs=pl.BlockSpec((None, n_q, 512), seq3)),
        out_shape=jax.ShapeDtypeStruct((n_seq, n_q, 512), F32),
        compiler_params=pltpu.CompilerParams(dimension_semantics=("arbitrary",), vmem_limit_bytes=VMEM_LIMIT),
        name="nsa_sample_attn",
    )(pt, *([cache] * n_pages), q_s, msel, expand, kv_new, win_state, wkv_new, gate, o_cmp)


def _ffn_kernel(x_ref, oa_ref, u_ref, v_ref, ws_ref, bs_ref, woa_ref, wob_ref, g2_ref, w1_ref, w2_ref, gf_ref,
                y_ref, h_scr, hn_scr, ob_scr, acc_scr):
    j = pl.program_id(1)
    tm = x_ref.shape[0]

    @pl.when(j == 0)
    def _():
        lo = lax.broadcasted_iota(jnp.int32, (CHUNK, LANES), 1) < HEAD_DIM
        for ch in range(tm // CHUNK):
            rows = slice(ch * CHUNK, (ch + 1) * CHUNK)
            cols = []
            for c in range(4):
                vc = v_ref[rows, c * LANES:(c + 1) * LANES].astype(BF16)
                cols.append(jnp.where(lo, _nn(ws_ref[2 * c], vc), _nn(ws_ref[2 * c + 1], vc)))
            s = jnp.concatenate(cols, axis=1) + bs_ref[...]
            ob_scr[rows, :] = (u_ref[rows, :] * s).astype(BF16)
        h = x_ref[...] + _nn(oa_ref[...].astype(BF16), woa_ref[...]) + _nn(ob_scr[...], wob_ref[...])
        h_scr[...] = h
        hn_scr[...] = _rms(h, g2_ref[...]).astype(BF16)
        acc_scr[...] = jnp.zeros(acc_scr.shape, F32)

    f = jnp.maximum(_nn(hn_scr[...], w1_ref[...]), 0.0)
    acc_scr[...] += _nn((f * f).astype(BF16), w2_ref[...])

    @pl.when(j == pl.num_programs(1) - 1)
    def _():
        y_ref[...] = _rms(h_scr[...] + acc_scr[...], gf_ref[...])


def _out_ffn(x, o_a, u, v, ws, bs, woa, wob, g2, w1, w2, gf, tm=512, tf=1024):
    n = x.shape[0]
    row = lambda i, j: (i, 0)
    const = lambda i, j: (0, 0)
    return pl.pallas_call(
        _ffn_kernel,
        grid=(n // tm, D_FF // tf),
        in_specs=[pl.BlockSpec((tm, D_MODEL), row), pl.BlockSpec((tm, 512), row), pl.BlockSpec((tm, 512), row),
                  pl.BlockSpec((tm, 512), row), pl.BlockSpec((8, CHUNK, CHUNK), lambda i, j: (0, 0, 0)),
                  pl.BlockSpec((CHUNK, 512), const), pl.BlockSpec((512, D_MODEL), const),
                  pl.BlockSpec((512, D_MODEL), const), pl.BlockSpec((1, D_MODEL), const),
                  pl.BlockSpec((D_MODEL, tf), lambda i, j: (0, j)), pl.BlockSpec((tf, D_MODEL), lambda i, j: (j, 0)),
                  pl.BlockSpec((1, D_MODEL), const)],
        out_specs=pl.BlockSpec((tm, D_MODEL), row),
        out_shape=jax.ShapeDtypeStruct((n, D_MODEL), F32),
        scratch_shapes=[pltpu.VMEM((tm, D_MODEL), F32), pltpu.VMEM((tm, D_MODEL), BF16),
                        pltpu.VMEM((tm, 512), BF16), pltpu.VMEM((tm, D_MODEL), F32)],
        compiler_params=pltpu.CompilerParams(dimension_semantics=("arbitrary", "arbitrary"),
                                             vmem_limit_bytes=VMEM_LIMIT),
        name="nsa_out_ffn",
    )(x, o_a, u, v, ws, bs, woa, wob, g2, w1, w2, gf)


def _head_perm():
    j = np.arange(512)
    return ((j // 128) + 4 * ((j % 128) // 64)) * 64 + (j % 64)


def _cover_t(n_c, n_s, n_c_pad, n_s_pad):
    ci = np.arange(n_c)[:, None] * CMP_STRIDE
    sj = np.arange(n_s)[None, :] * SLC_BLOCK
    cover = np.clip(np.minimum(ci + CMP_BLOCK, sj + SLC_BLOCK) - np.maximum(ci, sj), 0, None) / CMP_BLOCK
    out = np.zeros((n_s_pad, n_c_pad), np.float32)
    out[:n_s, :n_c] = cover.T
    return jnp.asarray(out)


def _cmp_weights(cmp_w1, cmp_b1, cmp_w2, cmp_b2, cmp_pos):
    z = jnp.zeros((CMP_HIDDEN, HEAD_DIM), F32)
    w1 = [jnp.concatenate([cmp_w1[i, :1024], cmp_w1[i, 1024:]], axis=1).astype(BF16) for i in range(2)]
    pos = [cmp_pos[i].reshape(2, 1024) for i in range(2)]
    b1 = [cmp_b1[i].reshape(1, CMP_HIDDEN) for i in range(2)]
    w2a = [jnp.concatenate([cmp_w2[i], z], axis=1).astype(BF16) for i in range(2)]
    w2b = [jnp.concatenate([z, cmp_w2[i]], axis=1).astype(BF16) for i in range(2)]
    b2 = [jnp.concatenate([cmp_b2[i], cmp_b2[i]]).reshape(1, LANES) for i in range(2)]
    return (w1[0], w1[1], pos[0], pos[1], b1[0], b1[1], w2a[0], w2b[0], w2a[1], w2b[1], b2[0], b2[1])


def kernel(x_prompt, x_sample, cache_kv, state_win_kv, page_table, ln1_g, w_in, cmp_w1, cmp_b1, cmp_w2, cmp_b2,
           cmp_pos, ln_v_g, ln_v_b, w_s, b_s, w_out, ln2_g, w_ff1, w_ff2, ln_f_g):
    batch, seq, _ = x_prompt.shape
    n_seq, n_q, _ = x_sample.shape
    n_pages = page_table.shape[1]
    past = n_pages * PAGE
    perm = _head_perm()

    wi = w_in[0]
    w_all = jnp.concatenate(
        [wi[:, 0:512][:, perm], wi[:, 512:1304], jnp.zeros((D_MODEL, C_U - C_GATE - 24), F32), wi[:, 1304:2328]],
        axis=1).astype(BF16)
    g1 = ln1_g[0].reshape(1, D_MODEL)
    lvg = ln_v_g[0].reshape(1, 512)
    lvb = ln_v_b[0].reshape(1, 512)
    cw = _cmp_weights(cmp_w1[0], cmp_b1[0], cmp_w2[0], cmp_b2[0], cmp_pos[0])
    tril = jnp.tril(jnp.ones((CHUNK, CHUNK), F32))
    ws_p = (w_s[0] * tril).astype(BF16)
    bs_p = jnp.repeat(b_s[0].T, HEAD_DIM, axis=1)
    reps = CHUNK // n_q
    ws_s = jnp.einsum("ab,gij->gaibj", jnp.eye(reps, dtype=F32), (w_s[0] * tril)[:, :n_q, :n_q])
    ws_s = ws_s.reshape(8, CHUNK, CHUNK).astype(BF16)
    bs_s = jnp.tile(jnp.repeat(b_s[0].T[:n_q], HEAD_DIM, axis=1), (reps, 1))
    woa = w_out[0][:512][perm].astype(BF16)
    wob = w_out[0][512:].astype(BF16)
    g2 = ln2_g[0].reshape(1, D_MODEL)
    gf = ln_f_g.reshape(1, D_MODEL)
    w1 = w_ff1[0].astype(BF16)
    w2 = w_ff2[0].astype(BF16)

    xp = x_prompt.reshape(batch * seq, D_MODEL)
    q_p, kv_p, wkv_p, kvw_p, gate_p, u_p, v_p = _project(xp, g1, w_all, lvg, lvb)
    kc_p, vc_p = _compress_prompt(kv_p, cw, batch, seq)
    n_seg_p = seq // CMP_STRIDE
    cov_p = _cover_t(n_seg_p - 1, seq // SLC_BLOCK, n_seg_p, seq // SLC_BLOCK)
    oa_p = _nsa_prompt(q_p, gate_p, kc_p, vc_p, kvw_p, cov_p, batch, seq)
    y_p = _out_ffn(xp, oa_p, u_p, v_p, ws_p, bs_p, woa, wob, g2, w1, w2, gf)

    xs = x_sample.reshape(n_seq * n_q, D_MODEL)
    q_s, kv_s, wkv_s, _, gate_s, u_s, v_s = _project(xs, g1, w_all, lvg, lvb)
    cache = jnp.transpose(cache_kv[0], (0, 2, 3, 4, 1)).reshape(-1, 4, LANES, PAGE)
    win_state = jnp.transpose(state_win_kv[0], (0, 2, 3, 4, 1)).reshape(n_seq, 2, LANES, -1)
    pt = page_table.reshape(-1)
    n_seg_s = past // CMP_STRIDE
    n_s = past // SLC_BLOCK + 1
    n_sp = -(-n_s // LANES) * LANES
    cov_s = _cover_t(n_seg_s, n_s, n_seg_s, n_sp)
    q_s3 = q_s.reshape(n_seq, n_q, 512)
    ocmp_s, msel = _sample_compress(pt, cache, q_s3, cw, cov_s)
    n_blk = past // SLC_BLOCK
    expand = jnp.asarray((np.arange(n_blk)[:, None] == (np.arange(past)[None, :] // SLC_BLOCK)).astype(np.float32)).astype(BF16)
    oa_s = _sample_attend(pt, cache, q_s3, msel, expand, kv_s.reshape(n_seq, n_q, 512), win_state,
                          wkv_s.reshape(n_seq, n_q, 256), gate_s.reshape(n_seq, n_q, 128), ocmp_s)
    y_s = _out_ffn(xs, oa_s.reshape(n_seq * n_q, 512), u_s, v_s, ws_s, bs_s, woa, wob, g2, w1, w2, gf)

    keep = min(WINDOW, seq)
    new_win_p = wkv_p.reshape(batch, seq, 2, N_KVH, HEAD_DIM)[:, seq - keep:]
    new_win_s = jnp.concatenate([state_win_kv[0], wkv_s.reshape(n_seq, n_q, 2, N_KVH, HEAD_DIM)], axis=1)[:, n_q:]
    return (y_p.reshape(batch, seq, D_MODEL),
            y_s.reshape(n_seq, n_q, D_MODEL),
            kv_p.reshape(1, batch, seq, 4, N_KVH, HEAD_DIM),
            kv_s.reshape(1, n_seq, n_q, 4, N_KVH, HEAD_DIM),
            new_win_p[None],
            new_win_s[None],
            v_s.reshape(1, n_seq, n_q, 512))
```

```python
import functools

import numpy as np
import jax
import jax.numpy as jnp
from jax import lax
from jax.experimental import pallas as pl
from jax.experimental.pallas import tpu as pltpu

F32 = jnp.float32
BF16 = jnp.bfloat16

D_MODEL = 1024
HEAD_DIM = 64
N_HEADS = 8
N_KVH = 2
GQA = 4
CMP_BLOCK = 32
CMP_STRIDE = 16
CMP_HIDDEN = 128
SLC_BLOCK = 64
N_SELECT = 16
WINDOW = 512
CHUNK = 128
D_FF = 4096
PAGE = 128
EPS = 1e-6
NEG = -1e30
FORCE_SCORE = 1e9
LANES = 128
VMEM_LIMIT = 56 * 1024 * 1024
SLC_TILE = 512

C_Q, C_KV, C_WIN, C_GATE, C_U, C_V, C_END = 0, 512, 1024, 1280, 1408, 1920, 2432


def _nn(a, b):
    return jnp.dot(a, b, preferred_element_type=F32)


def _nt(a, b):
    return lax.dot_general(a, b, (((1,), (1,)), ((), ())), preferred_element_type=F32)


def _gelu(x):
    return 0.5 * x * (1.0 + jnp.tanh(0.7978845608028654 * (x + 0.044715 * (x * x * x))))


def _rms(x, g):
    return x * lax.rsqrt(jnp.mean(x * x, axis=-1, keepdims=True) + EPS) * g


def _softmax_rows(s, mask):
    s = jnp.where(mask, s, NEG)
    m = jnp.max(s, axis=-1, keepdims=True)
    p = jnp.where(mask, jnp.exp(s - m), 0.0)
    l = jnp.sum(p, axis=-1, keepdims=True)
    return p / jnp.maximum(l, 1e-30)


def _proj_kernel(x_ref, g1_ref, w_ref, lvg_ref, lvb_ref,
                 q_ref, kv_ref, wkv_ref, kvw_ref, gate_ref, u_ref, v_ref):
    x = x_ref[...]
    h = _rms(x, g1_ref[...]).astype(BF16)

    def z(a, b):
        return _nn(h, w_ref[:, a:b])

    q_ref[...] = z(C_Q, C_KV) * (HEAD_DIM ** -0.5)
    kv = z(C_KV, C_WIN)
    kv_ref[...] = kv
    wkv = z(C_WIN, C_GATE)
    wkv_ref[...] = wkv
    kvw_ref[:, 0:256] = kv[:, 256:512].astype(BF16)
    kvw_ref[:, 256:512] = wkv.astype(BF16)
    gate_ref[...] = jax.nn.sigmoid(z(C_GATE, C_U))
    u_ref[...] = _gelu(z(C_U, C_V))
    zv = _gelu(z(C_V, C_END))
    mu = jnp.mean(zv, axis=-1, keepdims=True)
    var = jnp.mean(jnp.square(zv - mu), axis=-1, keepdims=True)
    v_ref[...] = (zv - mu) * lax.rsqrt(var + EPS) * lvg_ref[...] + lvb_ref[...]


def _project(x, g1, w_all, lvg, lvb, tm=512):
    n = x.shape[0]
    row = lambda i: (i, 0)
    const = lambda i: (0, 0)
    widths = (512, 512, 256, 512, 128, 512, 512)
    dtypes = (F32, F32, F32, BF16, F32, F32, F32)
    return pl.pallas_call(
        _proj_kernel,
        grid=(n // tm,),
        in_specs=[pl.BlockSpec((tm, D_MODEL), row), pl.BlockSpec((1, D_MODEL), const),
                  pl.BlockSpec((D_MODEL, C_END), const), pl.BlockSpec((1, 512), const),
                  pl.BlockSpec((1, 512), const)],
        out_specs=[pl.BlockSpec((tm, w), row) for w in widths],
        out_shape=[jax.ShapeDtypeStruct((n, w), d) for w, d in zip(widths, dtypes)],
        compiler_params=pltpu.CompilerParams(dimension_semantics=("arbitrary",), vmem_limit_bytes=VMEM_LIMIT),
        name="nsa_proj",
    )(x, g1, w_all, lvg, lvb)


def _compress_tail(h0, h1, w1_ref, pos_ref, b1_ref, w2a_ref, w2b_ref, b2_ref, n_seg):
    w1 = w1_ref[...]
    pos = pos_ref[...]
    p0 = jnp.broadcast_to(pos[0:1], (8, 1024)).astype(BF16)
    p1 = jnp.broadcast_to(pos[1:2], (8, 1024)).astype(BF16)
    c1 = b1_ref[...] + _nn(p0, w1[:, 0:128])[0:1] + _nn(p1, w1[:, 128:256])[0:1]

    def hidden(h):
        nxt = pltpu.roll(h[:, 128:256], n_seg - 1, 0)
        return _gelu(h[:, 0:128] + nxt + c1).astype(BF16)

    return _nn(hidden(h0), w2a_ref[...]) + _nn(hidden(h1), w2b_ref[...]) + b2_ref[...]


def _compress_core(lhs0, lhs1, w1_ref, pos_ref, b1_ref, w2a_ref, w2b_ref, b2_ref, n_seg):
    w1 = w1_ref[...]
    return _compress_tail(_nn(lhs0, w1), _nn(lhs1, w1), w1_ref, pos_ref, b1_ref, w2a_ref, w2b_ref, b2_ref, n_seg)


def _cmp_prompt_kernel(k_ref, v_ref, w1k_ref, w1v_ref, posk_ref, posv_ref, b1k_ref, b1v_ref,
                       w2ak_ref, w2bk_ref, w2av_ref, w2bv_ref, b2k_ref, b2v_ref,
                       kc_ref, vc_ref, u_scr):
    n_seg = k_ref.shape[0] // CMP_STRIDE
    lo = lax.broadcasted_iota(jnp.int32, (n_seg, LANES), 1) < HEAD_DIM

    def run(x_ref, w1, pos, b1, w2a, w2b, b2, out_ref):
        for pr in range(CMP_STRIDE // 2):
            a = x_ref[pl.ds(2 * pr, n_seg, stride=CMP_STRIDE), :]
            b = x_ref[pl.ds(2 * pr + 1, n_seg, stride=CMP_STRIDE), :]
            u_scr[0, :, pr * LANES:(pr + 1) * LANES] = jnp.where(lo, a, pltpu.roll(b, HEAD_DIM, 1)).astype(BF16)
            u_scr[1, :, pr * LANES:(pr + 1) * LANES] = jnp.where(lo, pltpu.roll(a, HEAD_DIM, 1), b).astype(BF16)
        out_ref[0] = _compress_core(u_scr[0], u_scr[1], w1, pos, b1, w2a, w2b, b2, n_seg)

    run(k_ref, w1k_ref, posk_ref, b1k_ref, w2ak_ref, w2bk_ref, b2k_ref, kc_ref)
    run(v_ref, w1v_ref, posv_ref, b1v_ref, w2av_ref, w2bv_ref, b2v_ref, vc_ref)


def _cmp_weight_specs(const):
    return ([pl.BlockSpec((1024, 256), const)] * 2 + [pl.BlockSpec((2, 1024), const)] * 2
            + [pl.BlockSpec((1, 128), const)] * 2 + [pl.BlockSpec((128, 128), const)] * 4
            + [pl.BlockSpec((1, 128), const)] * 2)


def _compress_prompt(kv, cw, batch, seq):
    n_seg = seq // CMP_STRIDE
    const = lambda b: (0, 0)
    return pl.pallas_call(
        _cmp_prompt_kernel,
        grid=(batch,),
        in_specs=[pl.BlockSpec((seq, LANES), lambda b: (b, 0)), pl.BlockSpec((seq, LANES), lambda b: (b, 1))]
        + _cmp_weight_specs(const),
        out_specs=[pl.BlockSpec((1, n_seg, LANES), lambda b: (b, 0, 0))] * 2,
        out_shape=[jax.ShapeDtypeStruct((batch, n_seg, LANES), F32)] * 2,
        scratch_shapes=[pltpu.VMEM((2, n_seg, 1024), BF16)],
        compiler_params=pltpu.CompilerParams(dimension_semantics=("arbitrary",), vmem_limit_bytes=VMEM_LIMIT),
        name="nsa_cmp_prompt",
    )(kv, kv, *cw)


def _rank_row(score, blk, j, rj):
    lower = jnp.where(blk < j, 1.0, 0.0)
    beat = jnp.where(score > rj, 1.0, jnp.where(score == rj, lower, 0.0))
    return jnp.sum(beat, axis=0, keepdims=True)


def _scores_t(imp_t, blk, t):
    cur = t >> 6
    forced = (blk == 0) | (blk == cur) | (blk == cur - 1)
    visible = blk * SLC_BLOCK <= t
    return jnp.where(forced, FORCE_SCORE, jnp.where(visible, imp_t, -jnp.inf)), visible


def _importance_t(cov_t, psum):
    hi = psum.astype(BF16)
    lo_ = (psum - hi.astype(F32)).astype(BF16)
    c = cov_t.astype(BF16)
    return _nt(c, hi) + _nt(c, lo_)


def _nsa_prompt_kernel(q_ref, g_ref, kc_ref, vc_ref, kvw_ref, covt_ref, ebig_ref, gexp_ref, o_ref,
                       mrun_scr, acc_scr, s_scr):
    qb = pl.program_id(1)
    t0 = qb * 128
    lane = lax.broadcasted_iota(jnp.int32, (128, LANES), 1)
    lo = lane < HEAD_DIM
    t_col = t0 + lax.broadcasted_iota(jnp.int32, (128, 1), 0)
    q = q_ref[...]
    qm = [[jnp.where(lo if k == 0 else jnp.logical_not(lo), q[:, g * LANES:(g + 1) * LANES], 0.0).astype(BF16)
           for g in range(GQA)] for k in range(N_KVH)]

    kc = kc_ref[0].astype(BF16)
    vc = vc_ref[0].astype(BF16)
    n_c = kc.shape[0]
    n_iota = lax.broadcasted_iota(jnp.int32, (128, n_c), 1)
    mask_c = n_iota * CMP_STRIDE + (CMP_BLOCK - 1) <= t_col
    o_cmp = [[None] * GQA for _ in range(N_KVH)]
    psum = [None] * N_KVH
    for k in range(N_KVH):
        for g in range(GQA):
            p = _softmax_rows(_nt(qm[k][g], kc), mask_c)
            o_cmp[k][g] = _nn(p.astype(BF16), vc)
            psum[k] = p if g == 0 else psum[k] + p

    n_s = covt_ref.shape[0]
    blk = lax.broadcasted_iota(jnp.int32, (n_s, 128), 0)
    t_lane = t0 + lax.broadcasted_iota(jnp.int32, (n_s, 128), 1)
    nsel = []
    for k in range(N_KVH):
        score, visible = _scores_t(_importance_t(covt_ref[...], psum[k]), blk, t_lane)
        cnt = jnp.concatenate([_rank_row(score, blk, j, score[j:j + 1, :]) for j in range(n_s)], axis=0)
        m_t = jnp.where((cnt < N_SELECT) & visible, 1.0, 0.0)
        m_t = jnp.concatenate([m_t, jnp.zeros((128 - n_s, 128), F32)], axis=0)
        nsel.append((1.0 - m_t.T).astype(BF16))

    lhs = [[jnp.concatenate([qm[k][g], nsel[k]], axis=1) for g in range(GQA)] for k in range(N_KVH)]
    mrun_scr[...] = jnp.full(mrun_scr.shape, NEG, F32)
    acc_scr[...] = jnp.zeros(acc_scr.shape, F32)
    tk = SLC_TILE
    cc = lax.broadcasted_iota(jnp.int32, (128, tk), 1)
    n_full = qb // (tk // 128)

    def rhs_tile(k0):
        return jnp.concatenate([kvw_ref[pl.ds(k0, tk), 0:128], ebig_ref[pl.ds(k0, tk), :]], axis=1)

    def scores(k0, rhs, causal):
        out = []
        for k in range(N_KVH):
            for g in range(GQA):
                s = _nt(lhs[k][g], rhs)
                out.append(jnp.where((k0 + cc) <= t_col, s, NEG) if causal else s)
        return out

    def max_tile(k0, causal):
        for h, s in enumerate(scores(k0, rhs_tile(k0), causal)):
            s_scr[h, :, pl.ds(k0, tk)] = s
            m = s[:, 0:LANES]
            for j in range(1, tk // LANES):
                m = jnp.maximum(m, s[:, j * LANES:(j + 1) * LANES])
            mrun_scr[h] = jnp.maximum(mrun_scr[h], m)

    def acc_tile(k0, mb):
        v_t = kvw_ref[pl.ds(k0, tk), 128:256]
        lov = lax.broadcasted_iota(jnp.int32, (tk, LANES), 1) < HEAD_DIM
        v1 = [jnp.where(lov, v_t, 1.0).astype(BF16), jnp.where(lov, 1.0, v_t).astype(BF16)]
        for h in range(N_HEADS):
            p = jnp.exp(s_scr[h, :, pl.ds(k0, tk)] - jnp.concatenate([mb[h]] * (tk // LANES), axis=1))
            acc_scr[h] += _nn(p.astype(BF16), v1[h // GQA])

    def loop1(kt, carry):
        max_tile(pl.multiple_of(kt * tk, tk), False)
        return carry

    lax.fori_loop(0, n_full, loop1, 0)
    k_last = pl.multiple_of(n_full * tk, tk)
    max_tile(k_last, True)
    mb = [jnp.broadcast_to(jnp.max(mrun_scr[h], axis=-1, keepdims=True), (128, LANES)) for h in range(N_HEADS)]

    def loop2(kt, carry):
        acc_tile(pl.multiple_of(kt * tk, tk), mb)
        return carry

    lax.fori_loop(0, n_full + 1, loop2, 0)

    s0 = pl.multiple_of(jnp.maximum(t0 - WINDOW, 0), 128)
    wk = kvw_ref[pl.ds(s0, WINDOW + 128), 256:384]
    wv = kvw_ref[pl.ds(s0, WINDOW + 128), 384:512]
    low = lax.broadcasted_iota(jnp.int32, (WINDOW + 128, LANES), 1) < HEAD_DIM
    wv1 = [jnp.where(low, wv, 1.0).astype(BF16), jnp.where(low, 1.0, wv).astype(BF16)]
    kpos = s0 + lax.broadcasted_iota(jnp.int32, (128, WINDOW + 128), 1)
    mask_w = (kpos <= t_col) & (kpos > t_col - WINDOW)

    def window(k, g):
        s = jnp.where(mask_w, _nt(qm[k][g], wk), NEG)
        p = jnp.exp(s - jnp.max(s, axis=-1, keepdims=True))
        return _nn(p.astype(BF16), wv1[k])

    def normalised(n0, n1):
        return jnp.where(lo, n0, n1) / pltpu.roll(jnp.where(lo, n1, n0), HEAD_DIM, 1)

    gt = g_ref[...]
    g_hi = gt.astype(BF16)
    g_lo = (gt - g_hi.astype(F32)).astype(BF16)
    gexp = _nn(g_hi, gexp_ref[...]) + _nn(g_lo, gexp_ref[...])
    for c in range(GQA):
        cols = slice(c * LANES, (c + 1) * LANES)
        o_c = jnp.where(lo, o_cmp[0][c], o_cmp[1][c])
        o_s = normalised(acc_scr[c], acc_scr[GQA + c])
        o_w = normalised(window(0, c), window(1, c))
        o_ref[:, cols] = (gexp[:, cols] * o_c + gexp[:, 512 + c * LANES:512 + (c + 1) * LANES] * o_s
                          + gexp[:, 1024 + c * LANES:1024 + (c + 1) * LANES] * o_w).astype(BF16)


def _block_bias(seq):
    m = (np.arange(seq)[:, None] // SLC_BLOCK == np.arange(LANES)[None, :]).astype(np.float32) * NEG
    return jnp.asarray(m).astype(BF16)


def _gate_expand():
    m = np.zeros((LANES, 3 * 512), np.float32)
    col = np.arange(512)
    head = col // LANES + GQA * ((col % LANES) // HEAD_DIM)
    for br in range(3):
        m[br * N_HEADS + head, br * 512 + col] = 1.0
    return jnp.asarray(m).astype(BF16)


def _nsa_prompt(q, gate, kc, vc, kvw, cov_t, batch, seq):
    n_qb = seq // 128
    n_seg = kc.shape[1]
    row = lambda b, i: (b * n_qb + i, 0)
    const = lambda b, i: (0, 0)
    return pl.pallas_call(
        _nsa_prompt_kernel,
        grid=(batch, n_qb),
        in_specs=[pl.BlockSpec((128, 512), row), pl.BlockSpec((128, 128), row),
                  pl.BlockSpec((1, n_seg, LANES), lambda b, i: (b, 0, 0)),
                  pl.BlockSpec((1, n_seg, LANES), lambda b, i: (b, 0, 0)),
                  pl.BlockSpec((seq, 512), lambda b, i: (b, 0)),
                  pl.BlockSpec(cov_t.shape, const), pl.BlockSpec((seq, LANES), const),
                  pl.BlockSpec((LANES, 3 * 512), const)],
        out_specs=pl.BlockSpec((128, 512), row),
        out_shape=jax.ShapeDtypeStruct((batch * seq, 512), BF16),
        scratch_shapes=[pltpu.VMEM((N_HEADS, 128, LANES), F32), pltpu.VMEM((N_HEADS, 128, LANES), F32),
                        pltpu.VMEM((N_HEADS, 128, seq), F32)],
        compiler_params=pltpu.CompilerParams(dimension_semantics=("arbitrary", "arbitrary"),
                                             vmem_limit_bytes=VMEM_LIMIT),
        name="nsa_prompt_attn",
    )(q, gate, kc, vc, kvw, cov_t, _block_bias(seq), _gate_expand())


def _stack_queries(q, lo):
    rows = [jnp.where(lo if k == 0 else jnp.logical_not(lo), q[:, g * LANES:(g + 1) * LANES], 0.0)
            for k in range(N_KVH) for g in range(GQA)]
    return jnp.concatenate(rows, axis=0).astype(BF16)


def _smp_cmp_kernel(n_pages, pt_ref, *refs):
    pages = refs[:n_pages]
    (q_ref, perm_ref, wbdk_ref, wbdv_ref, w1k_ref, w1v_ref, posk_ref, posv_ref, b1k_ref, b1v_ref, w2ak_ref,
     w2bk_ref, w2av_ref, w2bv_ref, b2k_ref, b2v_ref, covt_ref, ocmp_ref, msel_ref, u_scr) = refs[n_pages:]
    seg_pp = PAGE // CMP_STRIDE
    n_seg = n_pages * seg_pp
    past = n_pages * PAGE

    perm = perm_ref[...]
    for pp in range(n_pages // 2):
        ra = _nt(perm, jnp.concatenate([pages[2 * pp][0], pages[2 * pp][1]], axis=0).astype(BF16))
        rb = _nt(perm, jnp.concatenate([pages[2 * pp + 1][0], pages[2 * pp + 1][1]], axis=0).astype(BF16))
        rows = slice(pp * 2 * seg_pp, (pp + 1) * 2 * seg_pp)
        for s in range(CMP_STRIDE):
            piece = jnp.concatenate([ra[s * seg_pp:(s + 1) * seg_pp], rb[s * seg_pp:(s + 1) * seg_pp]], axis=0)
            piece = piece.astype(BF16)
            u_scr[0, rows, s * LANES:(s + 1) * LANES] = piece[:, 0:LANES]
            u_scr[1, rows, s * LANES:(s + 1) * LANES] = piece[:, LANES:2 * LANES]

    hk = _nn(u_scr[0], wbdk_ref[...])
    hv = _nn(u_scr[1], wbdv_ref[...])
    kc = _compress_tail(hk[:, 0:256], hk[:, 256:512], w1k_ref, posk_ref, b1k_ref, w2ak_ref, w2bk_ref, b2k_ref, n_seg)
    vc = _compress_tail(hv[:, 0:256], hv[:, 256:512], w1v_ref, posv_ref, b1v_ref, w2av_ref, w2bv_ref, b2v_ref, n_seg)

    q = q_ref[...]
    n_q = q.shape[0]
    lo8 = lax.broadcasted_iota(jnp.int32, (n_q, LANES), 1) < HEAD_DIM
    q_all = _stack_queries(q, lo8)
    rows = lax.broadcasted_iota(jnp.int32, (N_HEADS * n_q, n_seg), 0)
    t_row = past + (rows & (n_q - 1))
    n_iota = lax.broadcasted_iota(jnp.int32, (N_HEADS * n_q, n_seg), 1)
    p = _softmax_rows(_nt(q_all, kc.astype(BF16)), n_iota * CMP_STRIDE + (CMP_BLOCK - 1) <= t_row)
    o = _nn(p.astype(BF16), vc.astype(BF16))
    half = GQA * n_q
    for c in range(GQA):
        ocmp_ref[:, c * LANES:(c + 1) * LANES] = jnp.where(lo8, o[c * n_q:(c + 1) * n_q],
                                                         o[half + c * n_q:half + (c + 1) * n_q])
    psum = [sum(p[k * half + g * n_q:k * half + (g + 1) * n_q] for g in range(GQA)) for k in range(N_KVH)]
    psum = jnp.concatenate(psum + [jnp.zeros((LANES - N_KVH * n_q, n_seg), F32)], axis=0)

    n_sp = covt_ref.shape[0]
    n_s = past // SLC_BLOCK + 1
    rows_used = -(-n_s // 8) * 8
    n_r = N_KVH * n_q
    blk_t = lax.broadcasted_iota(jnp.int32, (n_sp, LANES), 0)
    t_lane = past + (lax.broadcasted_iota(jnp.int32, (n_sp, LANES), 1) & (n_q - 1))
    score_t, visible_t = _scores_t(_importance_t(covt_ref[...], psum), blk_t, t_lane)
    score = score_t.T[0:n_r]
    visible = jnp.where(visible_t, 1.0, 0.0).T[0:n_r] > 0.5
    ii = lax.broadcasted_iota(jnp.int32, (rows_used, n_sp), 0)
    jj = lax.broadcasted_iota(jnp.int32, (rows_used, n_sp), 1)
    lower = jnp.where(ii < jj, 1.0, 0.0)
    cnt = []
    for r in range(n_r):
        col = score_t[0:rows_used, r:r + 1]
        row = score[r:r + 1, :]
        beat = jnp.where(col > row, 1.0, jnp.where(col == row, lower, 0.0))
        cnt.append(jnp.sum(beat, axis=0, keepdims=True))
    cnt = jnp.concatenate(cnt, axis=0)
    msel_ref[...] = jnp.where((cnt < N_SELECT) & visible, 1.0, 0.0)


def _page_specs(n_pages, half):
    def spec(p):
        return pl.BlockSpec((None, 2, LANES, PAGE), lambda b, pt: (pt[b * n_pages + p], half, 0, 0))
    return [spec(p) for p in range(n_pages)]


def _segment_perm():
    r = np.arange(PAGE)
    m = np.zeros((PAGE, PAGE), np.float32)
    m[r, (r % (PAGE // CMP_STRIDE)) * CMP_STRIDE + r // (PAGE // CMP_STRIDE)] = 1.0
    return jnp.asarray(m).astype(BF16)


def _blockdiag_w1(w1):
    w = jnp.transpose(w1.reshape(2, CMP_STRIDE, HEAD_DIM, CMP_HIDDEN), (1, 2, 0, 3))
    w = w.reshape(CMP_STRIDE, HEAD_DIM, 2 * CMP_HIDDEN)
    z = jnp.zeros_like(w)
    out = jnp.concatenate([jnp.concatenate([w, z], axis=2), jnp.concatenate([z, w], axis=2)], axis=1)
    return out.reshape(CMP_STRIDE * LANES, 4 * CMP_HIDDEN).astype(BF16)


def _sample_compress(pt, cache, q_s, wbd, cw, cov_t):
    n_seq, n_q, _ = q_s.shape
    n_pages = pt.shape[0] // n_seq
    n_seg = n_pages * (PAGE // CMP_STRIDE)
    n_sp = cov_t.shape[0]
    const = lambda b, pt: (0, 0)
    kern = functools.partial(_smp_cmp_kernel, n_pages)
    return pl.pallas_call(
        kern,
        grid_spec=pltpu.PrefetchScalarGridSpec(
            num_scalar_prefetch=1, grid=(n_seq,),
            in_specs=_page_specs(n_pages, 0)
            + [pl.BlockSpec((None, n_q, 512), lambda b, pt: (b, 0, 0)), pl.BlockSpec((PAGE, PAGE), const)]
            + [pl.BlockSpec((CMP_STRIDE * LANES, 4 * CMP_HIDDEN), const)] * 2
            + _cmp_weight_specs(const) + [pl.BlockSpec(cov_t.shape, const)],
            out_specs=[pl.BlockSpec((None, n_q, 512), lambda b, pt: (b, 0, 0)),
                       pl.BlockSpec((None, N_KVH * n_q, n_sp), lambda b, pt: (b, 0, 0))],
            scratch_shapes=[pltpu.VMEM((2, n_seg, CMP_STRIDE * LANES), BF16)]),
        out_shape=[jax.ShapeDtypeStruct((n_seq, n_q, 512), F32),
                   jax.ShapeDtypeStruct((n_seq, N_KVH * n_q, n_sp), F32)],
        compiler_params=pltpu.CompilerParams(dimension_semantics=("arbitrary",), vmem_limit_bytes=VMEM_LIMIT),
        name="nsa_sample_cmp",
    )(pt, *([cache] * n_pages), q_s, _segment_perm(), *wbd, *cw, cov_t)


def _smp_attn_kernel(n_pages, pt_ref, *refs):
    pages = refs[:n_pages]
    (q_ref, msel_ref, ebig_ref, kvn_ref, win_ref, wkvn_ref, g_ref, ocmp_ref, o_ref) = refs[n_pages:]
    past = n_pages * PAGE
    q = q_ref[...]
    n_q = q.shape[0]
    n_rows = N_HEADS * n_q
    lo8 = lax.broadcasted_iota(jnp.int32, (n_q, LANES), 1) < HEAD_DIM
    q_all = _stack_queries(q, lo8)
    qi = lax.broadcasted_iota(jnp.int32, (n_rows, 1), 0) & (n_q - 1)

    def pad_rows(x):
        return jnp.concatenate([x, jnp.zeros((LANES - n_q, LANES), F32)], axis=0).astype(BF16)

    def rep(x):
        return jnp.concatenate([x[0:n_q]] * GQA + [x[n_q:2 * n_q]] * GQA, axis=0)

    msel = msel_ref[...]
    n_blk = past // SLC_BLOCK
    lhs = jnp.concatenate([q_all, rep(1.0 - msel[:, 0:LANES]).astype(BF16)], axis=1)
    s_pages = [_nn(lhs, jnp.concatenate([pages[p][0].astype(BF16), ebig_ref[:, p * PAGE:(p + 1) * PAGE]], axis=0))
               for p in range(n_pages)]
    kvn = kvn_ref[...]
    k_tail = pad_rows(kvn[:, 256:384])
    v_tail = pad_rows(kvn[:, 384:512])
    ti = lax.broadcasted_iota(jnp.int32, (n_rows, LANES), 1)
    ok_tail = (rep(jnp.broadcast_to(msel[:, n_blk:n_blk + 1], (2 * n_q, LANES))) > 0.5) & (ti <= qi)
    s_tail = jnp.where(ok_tail, _nt(q_all, k_tail), NEG)
    m_run = s_tail
    for s_p in s_pages:
        m_run = jnp.maximum(m_run, s_p)
    m = jnp.max(m_run, axis=-1, keepdims=True)
    p_tail = jnp.exp(s_tail - m)
    l_run = p_tail
    pv = _nn(p_tail.astype(BF16), v_tail)
    for p, s_p in enumerate(s_pages):
        p_p = jnp.exp(s_p - m)
        l_run = l_run + p_p
        pv = pv + _nt(p_p.astype(BF16), pages[p][1].astype(BF16))
    o_slc = pv / jnp.sum(l_run, axis=-1, keepdims=True)

    wk = win_ref[0].astype(BF16)
    wv = win_ref[1].astype(BF16)
    wn = wkvn_ref[...]
    wk_tail = pad_rows(wn[:, 0:128])
    wv_tail = pad_rows(wn[:, 128:256])
    mi = lax.broadcasted_iota(jnp.int32, (n_rows, WINDOW), 1)
    sw_main = jnp.where(mi > qi, _nn(q_all, wk), NEG)
    sw_tail = jnp.where(ti <= qi, _nt(q_all, wk_tail), NEG)
    mw = jnp.maximum(jnp.max(sw_main, axis=-1, keepdims=True), jnp.max(sw_tail, axis=-1, keepdims=True))
    pw_main = jnp.exp(sw_main - mw)
    pw_tail = jnp.exp(sw_tail - mw)
    lw = jnp.sum(pw_main, axis=-1, keepdims=True) + jnp.sum(pw_tail, axis=-1, keepdims=True)
    o_win = (_nt(pw_main.astype(BF16), wv) + _nn(pw_tail.astype(BF16), wv_tail)) / lw

    gt = g_ref[...]
    ocmp = ocmp_ref[...]
    half = GQA * n_q
    for c in range(GQA):
        def pick(o):
            return jnp.where(lo8, o[c * n_q:(c + 1) * n_q], o[half + c * n_q:half + (c + 1) * n_q])

        def gate(br):
            return jnp.where(lo8, gt[:, br * 8 + c:br * 8 + c + 1], gt[:, br * 8 + GQA + c:br * 8 + GQA + c + 1])

        o_ref[:, c * LANES:(c + 1) * LANES] = (gate(0) * ocmp[:, c * LANES:(c + 1) * LANES]
                                               + gate(1) * pick(o_slc) + gate(2) * pick(o_win))


def _sample_attend(pt, cache, q_s, msel, expand, kv_new, win_state, wkv_new, gate, o_cmp):
    n_seq, n_q, _ = q_s.shape
    n_pages = pt.shape[0] // n_seq
    past = n_pages * PAGE
    n_sp = msel.shape[-1]
    seq3 = lambda b, pt: (b, 0, 0)
    kern = functools.partial(_smp_attn_kernel, n_pages)
    return pl.pallas_call(
        kern,
        grid_spec=pltpu.PrefetchScalarGridSpec(
            num_scalar_prefetch=1, grid=(n_seq,),
            in_specs=_page_specs(n_pages, 1)
            + [pl.BlockSpec((None, n_q, 512), seq3), pl.BlockSpec((None, N_KVH * n_q, n_sp), seq3),
               pl.BlockSpec(expand.shape, lambda b, pt: (0, 0)),
               pl.BlockSpec((None, n_q, 512), seq3),
               pl.BlockSpec((None, 2, LANES, WINDOW), lambda b, pt: (b, 0, 0, 0)),
               pl.BlockSpec((None, n_q, 256), seq3), pl.BlockSpec((None, n_q, 128), seq3),
               pl.BlockSpec((None, n_q, 512), seq3)],
            out_specs=pl.BlockSpec((None, n_q, 512), seq3)),
        out_shape=jax.ShapeDtypeStruct((n_seq, n_q, 512), F32),
        compiler_params=pltpu.CompilerParams(dimension_semantics=("arbitrary",), vmem_limit_bytes=VMEM_LIMIT),
        name="nsa_sample_attn",
    )(pt, *([cache] * n_pages), q_s, msel, expand, kv_new, win_state, wkv_new, gate, o_cmp)


def _ffn_kernel(x_ref, oa_ref, u_ref, v_ref, ws_ref, bs_ref, woa_ref, wob_ref, g2_ref, w1_ref, w2_ref, gf_ref,
                y_ref, h_scr, hn_scr, ob_scr, acc_scr):
    j = pl.program_id(1)
    tm = x_ref.shape[0]

    @pl.when(j == 0)
    def _():
        lo = lax.broadcasted_iota(jnp.int32, (CHUNK, LANES), 1) < HEAD_DIM
        for ch in range(tm // CHUNK):
            rows = slice(ch * CHUNK, (ch + 1) * CHUNK)
            cols = []
            for c in range(4):
                vc = v_ref[rows, c * LANES:(c + 1) * LANES].astype(BF16)
                cols.append(jnp.where(lo, _nn(ws_ref[2 * c], vc), _nn(ws_ref[2 * c + 1], vc)))
            s = jnp.concatenate(cols, axis=1) + bs_ref[...]
            ob_scr[rows, :] = (u_ref[rows, :] * s).astype(BF16)
        h = x_ref[...] + _nn(oa_ref[...].astype(BF16), woa_ref[...]) + _nn(ob_scr[...], wob_ref[...])
        h_scr[...] = h
        hn_scr[...] = _rms(h, g2_ref[...]).astype(BF16)
        acc_scr[...] = jnp.zeros(acc_scr.shape, F32)

    f = jnp.maximum(_nn(hn_scr[...], w1_ref[...]), 0.0)
    acc_scr[...] += _nn((f * f).astype(BF16), w2_ref[...])

    @pl.when(j == pl.num_programs(1) - 1)
    def _():
        y_ref[...] = _rms(h_scr[...] + acc_scr[...], gf_ref[...])


def _out_ffn(x, o_a, u, v, ws, bs, woa, wob, g2, w1, w2, gf, tm=512, tf=1024):
    n = x.shape[0]
    row = lambda i, j: (i, 0)
    const = lambda i, j: (0, 0)
    return pl.pallas_call(
        _ffn_kernel,
        grid=(n // tm, D_FF // tf),
        in_specs=[pl.BlockSpec((tm, D_MODEL), row), pl.BlockSpec((tm, 512), row), pl.BlockSpec((tm, 512), row),
                  pl.BlockSpec((tm, 512), row), pl.BlockSpec((8, CHUNK, CHUNK), lambda i, j: (0, 0, 0)),
                  pl.BlockSpec((CHUNK, 512), const), pl.BlockSpec((512, D_MODEL), const),
                  pl.BlockSpec((512, D_MODEL), const), pl.BlockSpec((1, D_MODEL), const),
                  pl.BlockSpec((D_MODEL, tf), lambda i, j: (0, j)), pl.BlockSpec((tf, D_MODEL), lambda i, j: (j, 0)),
                  pl.BlockSpec((1, D_MODEL), const)],
        out_specs=pl.BlockSpec((tm, D_MODEL), row),
        out_shape=jax.ShapeDtypeStruct((n, D_MODEL), F32),
        scratch_shapes=[pltpu.VMEM((tm, D_MODEL), F32), pltpu.VMEM((tm, D_MODEL), BF16),
                        pltpu.VMEM((tm, 512), BF16), pltpu.VMEM((tm, D_MODEL), F32)],
        compiler_params=pltpu.CompilerParams(dimension_semantics=("arbitrary", "arbitrary"),
                                             vmem_limit_bytes=VMEM_LIMIT),
        name="nsa_out_ffn",
    )(x, o_a, u, v, ws, bs, woa, wob, g2, w1, w2, gf)


def _head_perm():
    j = np.arange(512)
    return ((j // 128) + 4 * ((j % 128) // 64)) * 64 + (j % 64)


def _cover_t(n_c, n_s, n_c_pad, n_s_pad):
    ci = np.arange(n_c)[:, None] * CMP_STRIDE
    sj = np.arange(n_s)[None, :] * SLC_BLOCK
    cover = np.clip(np.minimum(ci + CMP_BLOCK, sj + SLC_BLOCK) - np.maximum(ci, sj), 0, None) / CMP_BLOCK
    out = np.zeros((n_s_pad, n_c_pad), np.float32)
    out[:n_s, :n_c] = cover.T
    return jnp.asarray(out)


def _cmp_weights(cmp_w1, cmp_b1, cmp_w2, cmp_b2, cmp_pos):
    z = jnp.zeros((CMP_HIDDEN, HEAD_DIM), F32)
    w1 = [jnp.concatenate([cmp_w1[i, :1024], cmp_w1[i, 1024:]], axis=1).astype(BF16) for i in range(2)]
    pos = [cmp_pos[i].reshape(2, 1024) for i in range(2)]
    b1 = [cmp_b1[i].reshape(1, CMP_HIDDEN) for i in range(2)]
    w2a = [jnp.concatenate([cmp_w2[i], z], axis=1).astype(BF16) for i in range(2)]
    w2b = [jnp.concatenate([z, cmp_w2[i]], axis=1).astype(BF16) for i in range(2)]
    b2 = [jnp.concatenate([cmp_b2[i], cmp_b2[i]]).reshape(1, LANES) for i in range(2)]
    return (w1[0], w1[1], pos[0], pos[1], b1[0], b1[1], w2a[0], w2b[0], w2a[1], w2b[1], b2[0], b2[1])


def kernel(x_prompt, x_sample, cache_kv, state_win_kv, page_table, ln1_g, w_in, cmp_w1, cmp_b1, cmp_w2, cmp_b2,
           cmp_pos, ln_v_g, ln_v_b, w_s, b_s, w_out, ln2_g, w_ff1, w_ff2, ln_f_g):
    batch, seq, _ = x_prompt.shape
    n_seq, n_q, _ = x_sample.shape
    n_pages = page_table.shape[1]
    past = n_pages * PAGE
    perm = _head_perm()

    wi = w_in[0]
    w_all = jnp.concatenate(
        [wi[:, 0:512][:, perm], wi[:, 512:1304], jnp.zeros((D_MODEL, C_U - C_GATE - 24), F32), wi[:, 1304:2328]],
        axis=1).astype(BF16)
    g1 = ln1_g[0].reshape(1, D_MODEL)
    lvg = ln_v_g[0].reshape(1, 512)
    lvb = ln_v_b[0].reshape(1, 512)
    cw = _cmp_weights(cmp_w1[0], cmp_b1[0], cmp_w2[0], cmp_b2[0], cmp_pos[0])
    tril = jnp.tril(jnp.ones((CHUNK, CHUNK), F32))
    ws_p = (w_s[0] * tril).astype(BF16)
    bs_p = jnp.repeat(b_s[0].T, HEAD_DIM, axis=1)
    reps = CHUNK // n_q
    ws_s = jnp.einsum("ab,gij->gaibj", jnp.eye(reps, dtype=F32), (w_s[0] * tril)[:, :n_q, :n_q])
    ws_s = ws_s.reshape(8, CHUNK, CHUNK).astype(BF16)
    bs_s = jnp.tile(jnp.repeat(b_s[0].T[:n_q], HEAD_DIM, axis=1), (reps, 1))
    woa = w_out[0][:512][perm].astype(BF16)
    wob = w_out[0][512:].astype(BF16)
    g2 = ln2_g[0].reshape(1, D_MODEL)
    gf = ln_f_g.reshape(1, D_MODEL)
    w1 = w_ff1[0].astype(BF16)
    w2 = w_ff2[0].astype(BF16)

    xp = x_prompt.reshape(batch * seq, D_MODEL)
    q_p, kv_p, wkv_p, kvw_p, gate_p, u_p, v_p = _project(xp, g1, w_all, lvg, lvb)
    kc_p, vc_p = _compress_prompt(kv_p, cw, batch, seq)
    n_seg_p = seq // CMP_STRIDE
    cov_p = _cover_t(n_seg_p - 1, seq // SLC_BLOCK, n_seg_p, seq // SLC_BLOCK)
    oa_p = _nsa_prompt(q_p, gate_p, kc_p, vc_p, kvw_p, cov_p, batch, seq)
    y_p = _out_ffn(xp, oa_p, u_p, v_p, ws_p, bs_p, woa, wob, g2, w1, w2, gf)

    xs = x_sample.reshape(n_seq * n_q, D_MODEL)
    q_s, kv_s, wkv_s, _, gate_s, u_s, v_s = _project(xs, g1, w_all, lvg, lvb)
    cache = jnp.transpose(cache_kv[0], (0, 2, 3, 4, 1)).reshape(-1, 4, LANES, PAGE)
    win_state = jnp.transpose(state_win_kv[0], (0, 2, 3, 4, 1)).reshape(n_seq, 2, LANES, -1)
    pt = page_table.reshape(-1)
    n_seg_s = past // CMP_STRIDE
    n_s = past // SLC_BLOCK + 1
    n_sp = -(-n_s // LANES) * LANES
    cov_s = _cover_t(n_seg_s, n_s, n_seg_s, n_sp)
    q_s3 = q_s.reshape(n_seq, n_q, 512)
    wbd = (_blockdiag_w1(cmp_w1[0, 0]), _blockdiag_w1(cmp_w1[0, 1]))
    ocmp_s, msel = _sample_compress(pt, cache, q_s3, wbd, cw, cov_s)
    ebig_t = jnp.transpose(_block_bias(past))
    oa_s = _sample_attend(pt, cache, q_s3, msel, ebig_t, kv_s.reshape(n_seq, n_q, 512), win_state,
                          wkv_s.reshape(n_seq, n_q, 256), gate_s.reshape(n_seq, n_q, 128), ocmp_s)
    y_s = _out_ffn(xs, oa_s.reshape(n_seq * n_q, 512), u_s, v_s, ws_s, bs_s, woa, wob, g2, w1, w2, gf)

    keep = min(WINDOW, seq)
    new_win_p = wkv_p.reshape(batch, seq, 2, N_KVH, HEAD_DIM)[:, seq - keep:]
    new_win_s = jnp.concatenate([state_win_kv[0], wkv_s.reshape(n_seq, n_q, 2, N_KVH, HEAD_DIM)], axis=1)[:, n_q:]
    return (y_p.reshape(batch, seq, D_MODEL),
            y_s.reshape(n_seq, n_q, D_MODEL),
            kv_p.reshape(1, batch, seq, 4, N_KVH, HEAD_DIM),
            kv_s.reshape(1, n_seq, n_q, 4, N_KVH, HEAD_DIM),
            new_win_p[None],
            new_win_s[None],
            v_s.reshape(1, n_seq, n_q, 512))
```

```python
import functools

import numpy as np
import jax
import jax.numpy as jnp
from jax import lax
from jax.experimental import pallas as pl
from jax.experimental.pallas import tpu as pltpu

F32 = jnp.float32
BF16 = jnp.bfloat16

D_MODEL = 1024
HEAD_DIM = 64
N_HEADS = 8
N_KVH = 2
GQA = 4
CMP_BLOCK = 32
CMP_STRIDE = 16
CMP_HIDDEN = 128
SLC_BLOCK = 64
N_SELECT = 16
WINDOW = 512
CHUNK = 128
D_FF = 4096
PAGE = 128
EPS = 1e-6
NEG = -1e30
FORCE_SCORE = 1e9
LANES = 128
VMEM_LIMIT = 56 * 1024 * 1024
SLC_TILE = 512

C_Q, C_KV, C_WIN, C_GATE, C_U, C_V, C_END = 0, 512, 1024, 1280, 1408, 1920, 2432


def _nn(a, b):
    return jnp.dot(a, b, preferred_element_type=F32)


def _nt(a, b):
    return lax.dot_general(a, b, (((1,), (1,)), ((), ())), preferred_element_type=F32)


def _gelu(x):
    return 0.5 * x * (1.0 + jnp.tanh(0.7978845608028654 * (x + 0.044715 * (x * x * x))))


def _rms(x, g):
    return x * lax.rsqrt(jnp.mean(x * x, axis=-1, keepdims=True) + EPS) * g


def _softmax_rows(s, mask):
    s = jnp.where(mask, s, NEG)
    m = jnp.max(s, axis=-1, keepdims=True)
    p = jnp.where(mask, jnp.exp(s - m), 0.0)
    l = jnp.sum(p, axis=-1, keepdims=True)
    return p / jnp.maximum(l, 1e-30)


def _proj_kernel(x_ref, g1_ref, w_ref, lvg_ref, lvb_ref,
                 q_ref, kv_ref, wkv_ref, kvw_ref, gate_ref, u_ref, v_ref, kvt_ref, wkvt_ref):
    x = x_ref[...]
    h = _rms(x, g1_ref[...]).astype(BF16)

    def z(a, b):
        return _nn(h, w_ref[:, a:b])

    q_ref[...] = z(C_Q, C_KV) * (HEAD_DIM ** -0.5)
    kv = z(C_KV, C_WIN)
    kv_ref[...] = kv
    kvt_ref[...] = kv.T
    wkv = z(C_WIN, C_GATE)
    wkv_ref[...] = wkv
    wkvt_ref[...] = wkv.T
    kvw_ref[:, 0:256] = kv[:, 256:512].astype(BF16)
    kvw_ref[:, 256:512] = wkv.astype(BF16)
    gate_ref[...] = jax.nn.sigmoid(z(C_GATE, C_U))
    u_ref[...] = _gelu(z(C_U, C_V))
    zv = _gelu(z(C_V, C_END))
    mu = jnp.mean(zv, axis=-1, keepdims=True)
    var = jnp.mean(jnp.square(zv - mu), axis=-1, keepdims=True)
    v_ref[...] = (zv - mu) * lax.rsqrt(var + EPS) * lvg_ref[...] + lvb_ref[...]


def _project(x, g1, w_all, lvg, lvb, seq, tm=512):
    n = x.shape[0]
    per = seq // tm
    row = lambda i: (i, 0)
    const = lambda i: (0, 0)
    colmajor = lambda i: (i // per, 0, i % per)
    widths = (512, 512, 256, 512, 128, 512, 512)
    dtypes = (F32, F32, F32, BF16, F32, F32, F32)
    return pl.pallas_call(
        _proj_kernel,
        grid=(n // tm,),
        in_specs=[pl.BlockSpec((tm, D_MODEL), row), pl.BlockSpec((1, D_MODEL), const),
                  pl.BlockSpec((D_MODEL, C_END), const), pl.BlockSpec((1, 512), const),
                  pl.BlockSpec((1, 512), const)],
        out_specs=[pl.BlockSpec((tm, w), row) for w in widths]
        + [pl.BlockSpec((None, 512, tm), colmajor), pl.BlockSpec((None, 256, tm), colmajor)],
        out_shape=[jax.ShapeDtypeStruct((n, w), d) for w, d in zip(widths, dtypes)]
        + [jax.ShapeDtypeStruct((n // seq, 512, seq), F32), jax.ShapeDtypeStruct((n // seq, 256, seq), F32)],
        compiler_params=pltpu.CompilerParams(dimension_semantics=("arbitrary",), vmem_limit_bytes=VMEM_LIMIT),
        name="nsa_proj",
    )(x, g1, w_all, lvg, lvb)


def _compress_tail(h0, h1, w1_ref, pos_ref, b1_ref, w2a_ref, w2b_ref, b2_ref, n_seg):
    w1 = w1_ref[...]
    pos = pos_ref[...]
    p0 = jnp.broadcast_to(pos[0:1], (8, 1024)).astype(BF16)
    p1 = jnp.broadcast_to(pos[1:2], (8, 1024)).astype(BF16)
    c1 = b1_ref[...] + _nn(p0, w1[:, 0:128])[0:1] + _nn(p1, w1[:, 128:256])[0:1]

    def hidden(h):
        nxt = pltpu.roll(h[:, 128:256], n_seg - 1, 0)
        return _gelu(h[:, 0:128] + nxt + c1).astype(BF16)

    return _nn(hidden(h0), w2a_ref[...]) + _nn(hidden(h1), w2b_ref[...]) + b2_ref[...]


def _compress_core(lhs0, lhs1, w1_ref, pos_ref, b1_ref, w2a_ref, w2b_ref, b2_ref, n_seg):
    w1 = w1_ref[...]
    return _compress_tail(_nn(lhs0, w1), _nn(lhs1, w1), w1_ref, pos_ref, b1_ref, w2a_ref, w2b_ref, b2_ref, n_seg)


def _cmp_prompt_kernel(k_ref, v_ref, w1k_ref, w1v_ref, posk_ref, posv_ref, b1k_ref, b1v_ref,
                       w2ak_ref, w2bk_ref, w2av_ref, w2bv_ref, b2k_ref, b2v_ref,
                       kc_ref, vc_ref, u_scr):
    n_seg = k_ref.shape[0] // CMP_STRIDE
    lo = lax.broadcasted_iota(jnp.int32, (n_seg, LANES), 1) < HEAD_DIM

    def run(x_ref, w1, pos, b1, w2a, w2b, b2, out_ref):
        for pr in range(CMP_STRIDE // 2):
            a = x_ref[pl.ds(2 * pr, n_seg, stride=CMP_STRIDE), :]
            b = x_ref[pl.ds(2 * pr + 1, n_seg, stride=CMP_STRIDE), :]
            u_scr[0, :, pr * LANES:(pr + 1) * LANES] = jnp.where(lo, a, pltpu.roll(b, HEAD_DIM, 1)).astype(BF16)
            u_scr[1, :, pr * LANES:(pr + 1) * LANES] = jnp.where(lo, pltpu.roll(a, HEAD_DIM, 1), b).astype(BF16)
        out_ref[0] = _compress_core(u_scr[0], u_scr[1], w1, pos, b1, w2a, w2b, b2, n_seg)

    run(k_ref, w1k_ref, posk_ref, b1k_ref, w2ak_ref, w2bk_ref, b2k_ref, kc_ref)
    run(v_ref, w1v_ref, posv_ref, b1v_ref, w2av_ref, w2bv_ref, b2v_ref, vc_ref)


def _cmp_weight_specs(const):
    return ([pl.BlockSpec((1024, 256), const)] * 2 + [pl.BlockSpec((2, 1024), const)] * 2
            + [pl.BlockSpec((1, 128), const)] * 2 + [pl.BlockSpec((128, 128), const)] * 4
            + [pl.BlockSpec((1, 128), const)] * 2)


def _compress_prompt(kv, cw, batch, seq):
    n_seg = seq // CMP_STRIDE
    const = lambda b: (0, 0)
    return pl.pallas_call(
        _cmp_prompt_kernel,
        grid=(batch,),
        in_specs=[pl.BlockSpec((seq, LANES), lambda b: (b, 0)), pl.BlockSpec((seq, LANES), lambda b: (b, 1))]
        + _cmp_weight_specs(const),
        out_specs=[pl.BlockSpec((1, n_seg, LANES), lambda b: (b, 0, 0))] * 2,
        out_shape=[jax.ShapeDtypeStruct((batch, n_seg, LANES), F32)] * 2,
        scratch_shapes=[pltpu.VMEM((2, n_seg, 1024), BF16)],
        compiler_params=pltpu.CompilerParams(dimension_semantics=("arbitrary",), vmem_limit_bytes=VMEM_LIMIT),
        name="nsa_cmp_prompt",
    )(kv, kv, *cw)


def _rank_row(score, blk, j, rj):
    lower = jnp.where(blk < j, 1.0, 0.0)
    beat = jnp.where(score > rj, 1.0, jnp.where(score == rj, lower, 0.0))
    return jnp.sum(beat, axis=0, keepdims=True)


def _scores_t(imp_t, blk, t):
    cur = t >> 6
    forced = (blk == 0) | (blk == cur) | (blk == cur - 1)
    visible = blk * SLC_BLOCK <= t
    return jnp.where(forced, FORCE_SCORE, jnp.where(visible, imp_t, -jnp.inf)), visible


def _importance_t(cov_t, psum):
    hi = psum.astype(BF16)
    lo_ = (psum - hi.astype(F32)).astype(BF16)
    c = cov_t.astype(BF16)
    return _nt(c, hi) + _nt(c, lo_)


def _nsa_prompt_kernel(q_ref, g_ref, kc_ref, vc_ref, kvw_ref, covt_ref, ebig_ref, gexp_ref, o_ref,
                       mrun_scr, acc_scr, s_scr):
    qb = pl.program_id(1)
    t0 = qb * 128
    n_rows = N_HEADS * 128
    half = GQA * 128
    lo = lax.broadcasted_iota(jnp.int32, (128, LANES), 1) < HEAD_DIM
    t_all = t0 + (lax.broadcasted_iota(jnp.int32, (n_rows, 1), 0) & 127)
    q = q_ref[...]
    qm = jnp.concatenate(
        [jnp.where(lo if k == 0 else jnp.logical_not(lo), q[:, g * LANES:(g + 1) * LANES], 0.0)
         for k in range(N_KVH) for g in range(GQA)], axis=0).astype(BF16)

    def head(x, k, g):
        return x[(k * GQA + g) * 128:(k * GQA + g + 1) * 128]

    kc = kc_ref[0].astype(BF16)
    vc = vc_ref[0].astype(BF16)
    n_c = kc.shape[0]
    n_iota = lax.broadcasted_iota(jnp.int32, (n_rows, n_c), 1)
    p_c = _softmax_rows(_nt(qm, kc), n_iota * CMP_STRIDE + (CMP_BLOCK - 1) <= t_all)
    o_cmp = _nn(p_c.astype(BF16), vc)
    psum = [sum(head(p_c, k, g) for g in range(GQA)) for k in range(N_KVH)]

    n_s = covt_ref.shape[0]
    blk = lax.broadcasted_iota(jnp.int32, (n_s, 128), 0)
    t_lane = t0 + lax.broadcasted_iota(jnp.int32, (n_s, 128), 1)
    nsel = []
    for k in range(N_KVH):
        score, visible = _scores_t(_importance_t(covt_ref[...], psum[k]), blk, t_lane)
        cnt = jnp.concatenate([_rank_row(score, blk, j, score[j:j + 1, :]) for j in range(n_s)], axis=0)
        m_t = jnp.where((cnt < N_SELECT) & visible, 1.0, 0.0)
        m_t = jnp.concatenate([m_t, jnp.zeros((128 - n_s, 128), F32)], axis=0)
        nsel.append((1.0 - m_t.T).astype(BF16))

    lhs = jnp.concatenate([qm, jnp.concatenate([nsel[0]] * GQA + [nsel[1]] * GQA, axis=0)], axis=1)
    mrun_scr[...] = jnp.full(mrun_scr.shape, NEG, F32)
    acc_scr[...] = jnp.zeros(acc_scr.shape, F32)
    tk = SLC_TILE
    n_full = qb // (tk // 128)

    def max_tile(k0, causal):
        rhs = jnp.concatenate([kvw_ref[pl.ds(k0, tk), 0:128], ebig_ref[pl.ds(k0, tk), :]], axis=1)
        s = _nt(lhs, rhs)
        if causal:
            s = jnp.where(k0 + lax.broadcasted_iota(jnp.int32, (n_rows, tk), 1) <= t_all, s, NEG)
        s_scr[:, pl.ds(k0, tk)] = s
        m = s[:, 0:LANES]
        for j in range(1, tk // LANES):
            m = jnp.maximum(m, s[:, j * LANES:(j + 1) * LANES])
        mrun_scr[...] = jnp.maximum(mrun_scr[...], m)

    def loop1(kt, carry):
        max_tile(pl.multiple_of(kt * tk, tk), False)
        return carry

    lax.fori_loop(0, n_full, loop1, 0)
    max_tile(pl.multiple_of(n_full * tk, tk), True)
    mb = jnp.broadcast_to(jnp.max(mrun_scr[...], axis=-1, keepdims=True), (n_rows, LANES))

    def loop2(kt, carry):
        k0 = pl.multiple_of(kt * tk, tk)
        v_t = kvw_ref[pl.ds(k0, tk), 128:256]
        lov = lax.broadcasted_iota(jnp.int32, (tk, LANES), 1) < HEAD_DIM
        p = jnp.exp(s_scr[:, pl.ds(k0, tk)] - jnp.concatenate([mb] * (tk // LANES), axis=1)).astype(BF16)
        acc_scr[0] += _nn(p[0:half], jnp.where(lov, v_t, 1.0).astype(BF16))
        acc_scr[1] += _nn(p[half:n_rows], jnp.where(lov, 1.0, v_t).astype(BF16))
        return carry

    lax.fori_loop(0, n_full + 1, loop2, 0)

    s0 = pl.multiple_of(jnp.maximum(t0 - WINDOW, 0), 128)
    wk = kvw_ref[pl.ds(s0, WINDOW + 128), 256:384]
    wv = kvw_ref[pl.ds(s0, WINDOW + 128), 384:512]
    low = lax.broadcasted_iota(jnp.int32, (WINDOW + 128, LANES), 1) < HEAD_DIM
    kpos = s0 + lax.broadcasted_iota(jnp.int32, (n_rows, WINDOW + 128), 1)
    s_w = jnp.where((kpos <= t_all) & (kpos > t_all - WINDOW), _nt(qm, wk), NEG)
    p_w = jnp.exp(s_w - jnp.max(s_w, axis=-1, keepdims=True)).astype(BF16)
    win = [_nn(p_w[0:half], jnp.where(low, wv, 1.0).astype(BF16)),
           _nn(p_w[half:n_rows], jnp.where(low, 1.0, wv).astype(BF16))]

    def normalised(n0, n1):
        return jnp.where(lo, n0, n1) / pltpu.roll(jnp.where(lo, n1, n0), HEAD_DIM, 1)

    gt = g_ref[...]
    g_hi = gt.astype(BF16)
    g_lo = (gt - g_hi.astype(F32)).astype(BF16)
    gexp = _nn(g_hi, gexp_ref[...]) + _nn(g_lo, gexp_ref[...])
    for c in range(GQA):
        cols = slice(c * LANES, (c + 1) * LANES)
        rows = slice(c * 128, (c + 1) * 128)
        o_c = jnp.where(lo, head(o_cmp, 0, c), head(o_cmp, 1, c))
        o_s = normalised(acc_scr[0, rows, :], acc_scr[1, rows, :])
        o_w = normalised(win[0][rows], win[1][rows])
        o_ref[:, cols] = (gexp[:, cols] * o_c + gexp[:, 512 + c * LANES:512 + (c + 1) * LANES] * o_s
                          + gexp[:, 1024 + c * LANES:1024 + (c + 1) * LANES] * o_w).astype(BF16)


def _block_bias(seq):
    m = (np.arange(seq)[:, None] // SLC_BLOCK == np.arange(LANES)[None, :]).astype(np.float32) * NEG
    return jnp.asarray(m).astype(BF16)


def _gate_expand():
    m = np.zeros((LANES, 3 * 512), np.float32)
    col = np.arange(512)
    head = col // LANES + GQA * ((col % LANES) // HEAD_DIM)
    for br in range(3):
        m[br * N_HEADS + head, br * 512 + col] = 1.0
    return jnp.asarray(m).astype(BF16)


def _nsa_prompt(q, gate, kc, vc, kvw, cov_t, batch, seq):
    n_qb = seq // 128
    n_seg = kc.shape[1]
    row = lambda b, i: (b * n_qb + i, 0)
    const = lambda b, i: (0, 0)
    return pl.pallas_call(
        _nsa_prompt_kernel,
        grid=(batch, n_qb),
        in_specs=[pl.BlockSpec((128, 512), row), pl.BlockSpec((128, 128), row),
                  pl.BlockSpec((1, n_seg, LANES), lambda b, i: (b, 0, 0)),
                  pl.BlockSpec((1, n_seg, LANES), lambda b, i: (b, 0, 0)),
                  pl.BlockSpec((seq, 512), lambda b, i: (b, 0)),
                  pl.BlockSpec(cov_t.shape, const), pl.BlockSpec((seq, LANES), const),
                  pl.BlockSpec((LANES, 3 * 512), const)],
        out_specs=pl.BlockSpec((128, 512), row),
        out_shape=jax.ShapeDtypeStruct((batch * seq, 512), BF16),
        scratch_shapes=[pltpu.VMEM((N_HEADS * 128, LANES), F32), pltpu.VMEM((N_KVH, GQA * 128, LANES), F32),
                        pltpu.VMEM((N_HEADS * 128, seq), F32)],
        compiler_params=pltpu.CompilerParams(dimension_semantics=("arbitrary", "arbitrary"),
                                             vmem_limit_bytes=VMEM_LIMIT),
        name="nsa_prompt_attn",
    )(q, gate, kc, vc, kvw, cov_t, _block_bias(seq), _gate_expand())


def _stack_queries(q, lo):
    rows = [jnp.where(lo if k == 0 else jnp.logical_not(lo), q[:, g * LANES:(g + 1) * LANES], 0.0)
            for k in range(N_KVH) for g in range(GQA)]
    return jnp.concatenate(rows, axis=0).astype(BF16)


def _smp_cmp_kernel(n_pages, pt_ref, *refs):
    pages = refs[:n_pages]
    (q_ref, perm_ref, wbdk_ref, wbdv_ref, w1k_ref, w1v_ref, posk_ref, posv_ref, b1k_ref, b1v_ref, w2ak_ref,
     w2bk_ref, w2av_ref, w2bv_ref, b2k_ref, b2v_ref, covt_ref, ocmp_ref, msel_ref, u_scr) = refs[n_pages:]
    seg_pp = PAGE // CMP_STRIDE
    n_seg = n_pages * seg_pp
    past = n_pages * PAGE

    perm = perm_ref[...]
    for pp in range(n_pages // 2):
        ra = _nt(perm, jnp.concatenate([pages[2 * pp][0], pages[2 * pp][1]], axis=0).astype(BF16))
        rb = _nt(perm, jnp.concatenate([pages[2 * pp + 1][0], pages[2 * pp + 1][1]], axis=0).astype(BF16))
        rows = slice(pp * 2 * seg_pp, (pp + 1) * 2 * seg_pp)
        for s in range(CMP_STRIDE):
            piece = jnp.concatenate([ra[s * seg_pp:(s + 1) * seg_pp], rb[s * seg_pp:(s + 1) * seg_pp]], axis=0)
            piece = piece.astype(BF16)
            u_scr[0, rows, s * LANES:(s + 1) * LANES] = piece[:, 0:LANES]
            u_scr[1, rows, s * LANES:(s + 1) * LANES] = piece[:, LANES:2 * LANES]

    hk = _nn(u_scr[0], wbdk_ref[...])
    hv = _nn(u_scr[1], wbdv_ref[...])
    kc = _compress_tail(hk[:, 0:256], hk[:, 256:512], w1k_ref, posk_ref, b1k_ref, w2ak_ref, w2bk_ref, b2k_ref, n_seg)
    vc = _compress_tail(hv[:, 0:256], hv[:, 256:512], w1v_ref, posv_ref, b1v_ref, w2av_ref, w2bv_ref, b2v_ref, n_seg)

    q = q_ref[...]
    n_q = q.shape[0]
    lo8 = lax.broadcasted_iota(jnp.int32, (n_q, LANES), 1) < HEAD_DIM
    q_all = _stack_queries(q, lo8)
    rows = lax.broadcasted_iota(jnp.int32, (N_HEADS * n_q, n_seg), 0)
    t_row = past + (rows & (n_q - 1))
    n_iota = lax.broadcasted_iota(jnp.int32, (N_HEADS * n_q, n_seg), 1)
    p = _softmax_rows(_nt(q_all, kc.astype(BF16)), n_iota * CMP_STRIDE + (CMP_BLOCK - 1) <= t_row)
    o = _nn(p.astype(BF16), vc.astype(BF16))
    half = GQA * n_q
    for c in range(GQA):
        ocmp_ref[:, c * LANES:(c + 1) * LANES] = jnp.where(lo8, o[c * n_q:(c + 1) * n_q],
                                                         o[half + c * n_q:half + (c + 1) * n_q])
    psum = [sum(p[k * half + g * n_q:k * half + (g + 1) * n_q] for g in range(GQA)) for k in range(N_KVH)]
    psum = jnp.concatenate(psum + [jnp.zeros((LANES - N_KVH * n_q, n_seg), F32)], axis=0)

    n_sp = covt_ref.shape[0]
    n_s = past // SLC_BLOCK + 1
    rows_used = -(-n_s // 8) * 8
    n_r = N_KVH * n_q
    blk_t = lax.broadcasted_iota(jnp.int32, (n_sp, LANES), 0)
    t_lane = past + (lax.broadcasted_iota(jnp.int32, (n_sp, LANES), 1) & (n_q - 1))
    score_t, visible_t = _scores_t(_importance_t(covt_ref[...], psum), blk_t, t_lane)
    score = score_t.T[0:n_r]
    visible = jnp.where(visible_t, 1.0, 0.0).T[0:n_r] > 0.5
    ii = lax.broadcasted_iota(jnp.int32, (rows_used, n_sp), 0)
    jj = lax.broadcasted_iota(jnp.int32, (rows_used, n_sp), 1)
    lower = jnp.where(ii < jj, 1.0, 0.0)
    cnt = []
    for r in range(n_r):
        col = score_t[0:rows_used, r:r + 1]
        row = score[r:r + 1, :]
        beat = jnp.where(col > row, 1.0, jnp.where(col == row, lower, 0.0))
        cnt.append(jnp.sum(beat, axis=0, keepdims=True))
    cnt = jnp.concatenate(cnt, axis=0)
    msel_ref[...] = jnp.where((cnt < N_SELECT) & visible, 1.0, 0.0)


def _page_specs(n_pages, half):
    def spec(p):
        return pl.BlockSpec((None, 2, LANES, PAGE), lambda b, pt: (pt[b * n_pages + p], half, 0, 0))
    return [spec(p) for p in range(n_pages)]


def _segment_perm():
    r = np.arange(PAGE)
    m = np.zeros((PAGE, PAGE), np.float32)
    m[r, (r % (PAGE // CMP_STRIDE)) * CMP_STRIDE + r // (PAGE // CMP_STRIDE)] = 1.0
    return jnp.asarray(m).astype(BF16)


def _blockdiag_w1(w1):
    w = jnp.transpose(w1.reshape(2, CMP_STRIDE, HEAD_DIM, CMP_HIDDEN), (1, 2, 0, 3))
    w = w.reshape(CMP_STRIDE, HEAD_DIM, 2 * CMP_HIDDEN)
    z = jnp.zeros_like(w)
    out = jnp.concatenate([jnp.concatenate([w, z], axis=2), jnp.concatenate([z, w], axis=2)], axis=1)
    return out.reshape(CMP_STRIDE * LANES, 4 * CMP_HIDDEN).astype(BF16)


def _sample_compress(pt, cache, q_s, wbd, cw, cov_t):
    n_seq, n_q, _ = q_s.shape
    n_pages = pt.shape[0] // n_seq
    n_seg = n_pages * (PAGE // CMP_STRIDE)
    n_sp = cov_t.shape[0]
    const = lambda b, pt: (0, 0)
    kern = functools.partial(_smp_cmp_kernel, n_pages)
    return pl.pallas_call(
        kern,
        grid_spec=pltpu.PrefetchScalarGridSpec(
            num_scalar_prefetch=1, grid=(n_seq,),
            in_specs=_page_specs(n_pages, 0)
            + [pl.BlockSpec((None, n_q, 512), lambda b, pt: (b, 0, 0)), pl.BlockSpec((PAGE, PAGE), const)]
            + [pl.BlockSpec((CMP_STRIDE * LANES, 4 * CMP_HIDDEN), const)] * 2
            + _cmp_weight_specs(const) + [pl.BlockSpec(cov_t.shape, const)],
            out_specs=[pl.BlockSpec((None, n_q, 512), lambda b, pt: (b, 0, 0)),
                       pl.BlockSpec((None, N_KVH * n_q, n_sp), lambda b, pt: (b, 0, 0))],
            scratch_shapes=[pltpu.VMEM((2, n_seg, CMP_STRIDE * LANES), BF16)]),
        out_shape=[jax.ShapeDtypeStruct((n_seq, n_q, 512), F32),
                   jax.ShapeDtypeStruct((n_seq, N_KVH * n_q, n_sp), F32)],
        compiler_params=pltpu.CompilerParams(dimension_semantics=("arbitrary",), vmem_limit_bytes=VMEM_LIMIT),
        name="nsa_sample_cmp",
    )(pt, *([cache] * n_pages), q_s, _segment_perm(), *wbd, *cw, cov_t)


def _smp_attn_kernel(n_pages, pt_ref, *refs):
    pages = refs[:n_pages]
    (q_ref, msel_ref, ebig_ref, kvn_ref, win_ref, wkvn_ref, g_ref, ocmp_ref, o_ref) = refs[n_pages:]
    past = n_pages * PAGE
    q = q_ref[...]
    n_q = q.shape[0]
    n_rows = N_HEADS * n_q
    lo8 = lax.broadcasted_iota(jnp.int32, (n_q, LANES), 1) < HEAD_DIM
    q_all = _stack_queries(q, lo8)
    qi = lax.broadcasted_iota(jnp.int32, (n_rows, 1), 0) & (n_q - 1)

    def pad_rows(x):
        return jnp.concatenate([x, jnp.zeros((LANES - n_q, LANES), F32)], axis=0).astype(BF16)

    def rep(x):
        return jnp.concatenate([x[0:n_q]] * GQA + [x[n_q:2 * n_q]] * GQA, axis=0)

    msel = msel_ref[...]
    n_blk = past // SLC_BLOCK
    lhs = jnp.concatenate([q_all, rep(1.0 - msel[:, 0:LANES]).astype(BF16)], axis=1)
    s_pages = [_nn(lhs, jnp.concatenate([pages[p][0].astype(BF16), ebig_ref[:, p * PAGE:(p + 1) * PAGE]], axis=0))
               for p in range(n_pages)]
    kvn = kvn_ref[...]
    k_tail = pad_rows(kvn[:, 256:384])
    v_tail = pad_rows(kvn[:, 384:512])
    ti = lax.broadcasted_iota(jnp.int32, (n_rows, LANES), 1)
    ok_tail = (rep(jnp.broadcast_to(msel[:, n_blk:n_blk + 1], (2 * n_q, LANES))) > 0.5) & (ti <= qi)
    s_tail = jnp.where(ok_tail, _nt(q_all, k_tail), NEG)
    m_run = s_tail
    for s_p in s_pages:
        m_run = jnp.maximum(m_run, s_p)
    m = jnp.max(m_run, axis=-1, keepdims=True)
    p_tail = jnp.exp(s_tail - m)
    l_run = p_tail
    pv = _nn(p_tail.astype(BF16), v_tail)
    for p, s_p in enumerate(s_pages):
        p_p = jnp.exp(s_p - m)
        l_run = l_run + p_p
        pv = pv + _nt(p_p.astype(BF16), pages[p][1].astype(BF16))
    o_slc = pv / jnp.sum(l_run, axis=-1, keepdims=True)

    wk = win_ref[0].astype(BF16)
    wv = win_ref[1].astype(BF16)
    wn = wkvn_ref[...]
    wk_tail = pad_rows(wn[:, 0:128])
    wv_tail = pad_rows(wn[:, 128:256])
    mi = lax.broadcasted_iota(jnp.int32, (n_rows, WINDOW), 1)
    sw_main = jnp.where(mi > qi, _nn(q_all, wk), NEG)
    sw_tail = jnp.where(ti <= qi, _nt(q_all, wk_tail), NEG)
    mw = jnp.maximum(jnp.max(sw_main, axis=-1, keepdims=True), jnp.max(sw_tail, axis=-1, keepdims=True))
    pw_main = jnp.exp(sw_main - mw)
    pw_tail = jnp.exp(sw_tail - mw)
    lw = jnp.sum(pw_main, axis=-1, keepdims=True) + jnp.sum(pw_tail, axis=-1, keepdims=True)
    o_win = (_nt(pw_main.astype(BF16), wv) + _nn(pw_tail.astype(BF16), wv_tail)) / lw

    gt = g_ref[...]
    ocmp = ocmp_ref[...]
    half = GQA * n_q
    for c in range(GQA):
        def pick(o):
            return jnp.where(lo8, o[c * n_q:(c + 1) * n_q], o[half + c * n_q:half + (c + 1) * n_q])

        def gate(br):
            return jnp.where(lo8, gt[:, br * 8 + c:br * 8 + c + 1], gt[:, br * 8 + GQA + c:br * 8 + GQA + c + 1])

        o_ref[:, c * LANES:(c + 1) * LANES] = (gate(0) * ocmp[:, c * LANES:(c + 1) * LANES]
                                               + gate(1) * pick(o_slc) + gate(2) * pick(o_win))


def _sample_attend(pt, cache, q_s, msel, expand, kv_new, win_state, wkv_new, gate, o_cmp):
    n_seq, n_q, _ = q_s.shape
    n_pages = pt.shape[0] // n_seq
    past = n_pages * PAGE
    n_sp = msel.shape[-1]
    seq3 = lambda b, pt: (b, 0, 0)
    kern = functools.partial(_smp_attn_kernel, n_pages)
    return pl.pallas_call(
        kern,
        grid_spec=pltpu.PrefetchScalarGridSpec(
            num_scalar_prefetch=1, grid=(n_seq,),
            in_specs=_page_specs(n_pages, 1)
            + [pl.BlockSpec((None, n_q, 512), seq3), pl.BlockSpec((None, N_KVH * n_q, n_sp), seq3),
               pl.BlockSpec(expand.shape, lambda b, pt: (0, 0)),
               pl.BlockSpec((None, n_q, 512), seq3),
               pl.BlockSpec((None, 2, LANES, WINDOW), lambda b, pt: (b, 0, 0, 0)),
               pl.BlockSpec((None, n_q, 256), seq3), pl.BlockSpec((None, n_q, 128), seq3),
               pl.BlockSpec((None, n_q, 512), seq3)],
            out_specs=pl.BlockSpec((None, n_q, 512), seq3)),
        out_shape=jax.ShapeDtypeStruct((n_seq, n_q, 512), F32),
        compiler_params=pltpu.CompilerParams(dimension_semantics=("arbitrary",), vmem_limit_bytes=VMEM_LIMIT),
        name="nsa_sample_attn",
    )(pt, *([cache] * n_pages), q_s, msel, expand, kv_new, win_state, wkv_new, gate, o_cmp)


def _ffn_kernel(x_ref, oa_ref, u_ref, v_ref, ws_ref, bs_ref, woa_ref, wob_ref, g2_ref, w1_ref, w2_ref, gf_ref,
                y_ref, h_scr, hn_scr, ob_scr, acc_scr):
    j = pl.program_id(1)
    tm = x_ref.shape[0]

    @pl.when(j == 0)
    def _():
        lo = lax.broadcasted_iota(jnp.int32, (CHUNK, LANES), 1) < HEAD_DIM
        for ch in range(tm // CHUNK):
            rows = slice(ch * CHUNK, (ch + 1) * CHUNK)
            cols = []
            for c in range(4):
                vc = v_ref[rows, c * LANES:(c + 1) * LANES].astype(BF16)
                cols.append(jnp.where(lo, _nn(ws_ref[2 * c], vc), _nn(ws_ref[2 * c + 1], vc)))
            s = jnp.concatenate(cols, axis=1) + bs_ref[...]
            ob_scr[rows, :] = (u_ref[rows, :] * s).astype(BF16)
        h = x_ref[...] + _nn(oa_ref[...].astype(BF16), woa_ref[...]) + _nn(ob_scr[...], wob_ref[...])
        h_scr[...] = h
        hn_scr[...] = _rms(h, g2_ref[...]).astype(BF16)
        acc_scr[...] = jnp.zeros(acc_scr.shape, F32)

    f = jnp.maximum(_nn(hn_scr[...], w1_ref[...]), 0.0)
    acc_scr[...] += _nn((f * f).astype(BF16), w2_ref[...])

    @pl.when(j == pl.num_programs(1) - 1)
    def _():
        y_ref[...] = _rms(h_scr[...] + acc_scr[...], gf_ref[...])


def _out_ffn(x, o_a, u, v, ws, bs, woa, wob, g2, w1, w2, gf, tm=512, tf=2048):
    n = x.shape[0]
    row = lambda i, j: (i, 0)
    const = lambda i, j: (0, 0)
    return pl.pallas_call(
        _ffn_kernel,
        grid=(n // tm, D_FF // tf),
        in_specs=[pl.BlockSpec((tm, D_MODEL), row), pl.BlockSpec((tm, 512), row), pl.BlockSpec((tm, 512), row),
                  pl.BlockSpec((tm, 512), row), pl.BlockSpec((8, CHUNK, CHUNK), lambda i, j: (0, 0, 0)),
                  pl.BlockSpec((CHUNK, 512), const), pl.BlockSpec((512, D_MODEL), const),
                  pl.BlockSpec((512, D_MODEL), const), pl.BlockSpec((1, D_MODEL), const),
                  pl.BlockSpec((D_MODEL, tf), lambda i, j: (0, j)), pl.BlockSpec((tf, D_MODEL), lambda i, j: (j, 0)),
                  pl.BlockSpec((1, D_MODEL), const)],
        out_specs=pl.BlockSpec((tm, D_MODEL), row),
        out_shape=jax.ShapeDtypeStruct((n, D_MODEL), F32),
        scratch_shapes=[pltpu.VMEM((tm, D_MODEL), F32), pltpu.VMEM((tm, D_MODEL), BF16),
                        pltpu.VMEM((tm, 512), BF16), pltpu.VMEM((tm, D_MODEL), F32)],
        compiler_params=pltpu.CompilerParams(dimension_semantics=("arbitrary", "arbitrary"),
                                             vmem_limit_bytes=VMEM_LIMIT),
        name="nsa_out_ffn",
    )(x, o_a, u, v, ws, bs, woa, wob, g2, w1, w2, gf)


def _head_perm():
    j = np.arange(512)
    return ((j // 128) + 4 * ((j % 128) // 64)) * 64 + (j % 64)


def _cover_t(n_c, n_s, n_c_pad, n_s_pad):
    ci = np.arange(n_c)[:, None] * CMP_STRIDE
    sj = np.arange(n_s)[None, :] * SLC_BLOCK
    cover = np.clip(np.minimum(ci + CMP_BLOCK, sj + SLC_BLOCK) - np.maximum(ci, sj), 0, None) / CMP_BLOCK
    out = np.zeros((n_s_pad, n_c_pad), np.float32)
    out[:n_s, :n_c] = cover.T
    return jnp.asarray(out)


def _cmp_weights(cmp_w1, cmp_b1, cmp_w2, cmp_b2, cmp_pos):
    z = jnp.zeros((CMP_HIDDEN, HEAD_DIM), F32)
    w1 = [jnp.concatenate([cmp_w1[i, :1024], cmp_w1[i, 1024:]], axis=1).astype(BF16) for i in range(2)]
    pos = [cmp_pos[i].reshape(2, 1024) for i in range(2)]
    b1 = [cmp_b1[i].reshape(1, CMP_HIDDEN) for i in range(2)]
    w2a = [jnp.concatenate([cmp_w2[i], z], axis=1).astype(BF16) for i in range(2)]
    w2b = [jnp.concatenate([z, cmp_w2[i]], axis=1).astype(BF16) for i in range(2)]
    b2 = [jnp.concatenate([cmp_b2[i], cmp_b2[i]]).reshape(1, LANES) for i in range(2)]
    return (w1[0], w1[1], pos[0], pos[1], b1[0], b1[1], w2a[0], w2b[0], w2a[1], w2b[1], b2[0], b2[1])


def kernel(x_prompt, x_sample, cache_kv, state_win_kv, page_table, ln1_g, w_in, cmp_w1, cmp_b1, cmp_w2, cmp_b2,
           cmp_pos, ln_v_g, ln_v_b, w_s, b_s, w_out, ln2_g, w_ff1, w_ff2, ln_f_g):
    batch, seq, _ = x_prompt.shape
    n_seq, n_q, _ = x_sample.shape
    n_pages = page_table.shape[1]
    past = n_pages * PAGE
    perm = _head_perm()

    wi = w_in[0]
    w_all = jnp.concatenate(
        [wi[:, 0:512][:, perm], wi[:, 512:1304], jnp.zeros((D_MODEL, C_U - C_GATE - 24), F32), wi[:, 1304:2328]],
        axis=1).astype(BF16)
    g1 = ln1_g[0].reshape(1, D_MODEL)
    lvg = ln_v_g[0].reshape(1, 512)
    lvb = ln_v_b[0].reshape(1, 512)
    cw = _cmp_weights(cmp_w1[0], cmp_b1[0], cmp_w2[0], cmp_b2[0], cmp_pos[0])
    tril = jnp.tril(jnp.ones((CHUNK, CHUNK), F32))
    ws_p = (w_s[0] * tril).astype(BF16)
    bs_p = jnp.repeat(b_s[0].T, HEAD_DIM, axis=1)
    reps = CHUNK // n_q
    ws_s = jnp.einsum("ab,gij->gaibj", jnp.eye(reps, dtype=F32), (w_s[0] * tril)[:, :n_q, :n_q])
    ws_s = ws_s.reshape(8, CHUNK, CHUNK).astype(BF16)
    bs_s = jnp.tile(jnp.repeat(b_s[0].T[:n_q], HEAD_DIM, axis=1), (reps, 1))
    woa = w_out[0][:512][perm].astype(BF16)
    wob = w_out[0][512:].astype(BF16)
    g2 = ln2_g[0].reshape(1, D_MODEL)
    gf = ln_f_g.reshape(1, D_MODEL)
    w1 = w_ff1[0].astype(BF16)
    w2 = w_ff2[0].astype(BF16)

    xp = x_prompt.reshape(batch * seq, D_MODEL)
    q_p, kv_p, _, kvw_p, gate_p, u_p, v_p, kvt_p, wkvt_p = _project(xp, g1, w_all, lvg, lvb, seq)
    kc_p, vc_p = _compress_prompt(kv_p, cw, batch, seq)
    n_seg_p = seq // CMP_STRIDE
    cov_p = _cover_t(n_seg_p - 1, seq // SLC_BLOCK, n_seg_p, seq // SLC_BLOCK)
    oa_p = _nsa_prompt(q_p, gate_p, kc_p, vc_p, kvw_p, cov_p, batch, seq)
    y_p = _out_ffn(xp, oa_p, u_p, v_p, ws_p, bs_p, woa, wob, g2, w1, w2, gf)

    xs = x_sample.reshape(n_seq * n_q, D_MODEL)
    q_s, kv_s, wkv_s, _, gate_s, u_s, v_s, _, _ = _project(xs, g1, w_all, lvg, lvb, n_seq * n_q)
    cache = jnp.transpose(cache_kv[0], (0, 2, 3, 4, 1)).reshape(-1, 4, LANES, PAGE)
    win_state = jnp.transpose(state_win_kv[0], (0, 2, 3, 4, 1)).reshape(n_seq, 2, LANES, -1)
    pt = page_table.reshape(-1)
    n_seg_s = past // CMP_STRIDE
    n_s = past // SLC_BLOCK + 1
    n_sp = -(-n_s // LANES) * LANES
    cov_s = _cover_t(n_seg_s, n_s, n_seg_s, n_sp)
    q_s3 = q_s.reshape(n_seq, n_q, 512)
    wbd = (_blockdiag_w1(cmp_w1[0, 0]), _blockdiag_w1(cmp_w1[0, 1]))
    ocmp_s, msel = _sample_compress(pt, cache, q_s3, wbd, cw, cov_s)
    ebig_t = jnp.transpose(_block_bias(past))
    oa_s = _sample_attend(pt, cache, q_s3, msel, ebig_t, kv_s.reshape(n_seq, n_q, 512), win_state,
                          wkv_s.reshape(n_seq, n_q, 256), gate_s.reshape(n_seq, n_q, 128), ocmp_s)
    y_s = _out_ffn(xs, oa_s.reshape(n_seq * n_q, 512), u_s, v_s, ws_s, bs_s, woa, wob, g2, w1, w2, gf)

    keep = min(WINDOW, seq)
    new_kv_p = jnp.transpose(kvt_p.reshape(batch, 4, N_KVH, HEAD_DIM, seq), (0, 4, 1, 2, 3))
    new_win_p = jnp.transpose(wkvt_p[:, :, seq - keep:].reshape(batch, 2, N_KVH, HEAD_DIM, keep), (0, 4, 1, 2, 3))
    new_win_s = jnp.concatenate([state_win_kv[0], wkv_s.reshape(n_seq, n_q, 2, N_KVH, HEAD_DIM)], axis=1)[:, n_q:]
    return (y_p.reshape(batch, seq, D_MODEL),
            y_s.reshape(n_seq, n_q, D_MODEL),
            new_kv_p[None],
            kv_s.reshape(1, n_seq, n_q, 4, N_KVH, HEAD_DIM),
            new_win_p[None],
            new_win_s[None],
            v_s.reshape(1, n_seq, n_q, 512))
```

```python
import functools

import numpy as np
import jax
import jax.numpy as jnp
from jax import lax
from jax.experimental import pallas as pl
from jax.experimental.pallas import tpu as pltpu

F32 = jnp.float32
BF16 = jnp.bfloat16

D_MODEL = 1024
HEAD_DIM = 64
N_HEADS = 8
N_KVH = 2
GQA = 4
CMP_BLOCK = 32
CMP_STRIDE = 16
CMP_HIDDEN = 128
SLC_BLOCK = 64
N_SELECT = 16
WINDOW = 512
CHUNK = 128
D_FF = 4096
PAGE = 128
EPS = 1e-6
NEG = -1e30
FORCE_SCORE = 1e9
LANES = 128
VMEM_LIMIT = 56 * 1024 * 1024
SLC_TILE = 512

C_Q, C_KV, C_WIN, C_GATE, C_U, C_V, C_END = 0, 512, 1024, 1280, 1408, 1920, 2432


def _nn(a, b):
    return jnp.dot(a, b, preferred_element_type=F32)


def _nt(a, b):
    return lax.dot_general(a, b, (((1,), (1,)), ((), ())), preferred_element_type=F32)


def _gelu(x):
    return 0.5 * x * (1.0 + jnp.tanh(0.7978845608028654 * (x + 0.044715 * (x * x * x))))


def _rms(x, g):
    return x * lax.rsqrt(jnp.mean(x * x, axis=-1, keepdims=True) + EPS) * g


def _softmax_rows(s, mask):
    s = jnp.where(mask, s, NEG)
    m = jnp.max(s, axis=-1, keepdims=True)
    p = jnp.where(mask, jnp.exp(s - m), 0.0)
    l = jnp.sum(p, axis=-1, keepdims=True)
    return p / jnp.maximum(l, 1e-30)


def _proj_kernel(x_ref, g1_ref, w_ref, lvg_ref, lvb_ref,
                 q_ref, kv_ref, wkv_ref, kvw_ref, gate_ref, u_ref, v_ref, kvt_ref, wkvt_ref):
    x = x_ref[...]
    h = _rms(x, g1_ref[...]).astype(BF16)

    def z(a, b):
        return _nn(h, w_ref[:, a:b])

    q_ref[...] = z(C_Q, C_KV) * (HEAD_DIM ** -0.5)
    kv = z(C_KV, C_WIN)
    kv_ref[...] = kv
    kvt_ref[...] = kv.T
    wkv = z(C_WIN, C_GATE)
    wkv_ref[...] = wkv
    wkvt_ref[...] = wkv.T
    kvw_ref[:, 0:256] = kv[:, 256:512].astype(BF16)
    kvw_ref[:, 256:512] = wkv.astype(BF16)
    gate_ref[...] = jax.nn.sigmoid(z(C_GATE, C_U))
    u_ref[...] = _gelu(z(C_U, C_V))
    zv = _gelu(z(C_V, C_END))
    mu = jnp.mean(zv, axis=-1, keepdims=True)
    var = jnp.mean(jnp.square(zv - mu), axis=-1, keepdims=True)
    v_ref[...] = (zv - mu) * lax.rsqrt(var + EPS) * lvg_ref[...] + lvb_ref[...]


def _project(x, g1, w_all, lvg, lvb, seq, tm=512):
    n = x.shape[0]
    per = seq // tm
    row = lambda i: (i, 0)
    const = lambda i: (0, 0)
    colmajor = lambda i: (i // per, 0, i % per)
    widths = (512, 512, 256, 512, 128, 512, 512)
    dtypes = (F32, F32, F32, BF16, F32, F32, F32)
    return pl.pallas_call(
        _proj_kernel,
        grid=(n // tm,),
        in_specs=[pl.BlockSpec((tm, D_MODEL), row), pl.BlockSpec((1, D_MODEL), const),
                  pl.BlockSpec((D_MODEL, C_END), const), pl.BlockSpec((1, 512), const),
                  pl.BlockSpec((1, 512), const)],
        out_specs=[pl.BlockSpec((tm, w), row) for w in widths]
        + [pl.BlockSpec((None, 512, tm), colmajor), pl.BlockSpec((None, 256, tm), colmajor)],
        out_shape=[jax.ShapeDtypeStruct((n, w), d) for w, d in zip(widths, dtypes)]
        + [jax.ShapeDtypeStruct((n // seq, 512, seq), F32), jax.ShapeDtypeStruct((n // seq, 256, seq), F32)],
        compiler_params=pltpu.CompilerParams(dimension_semantics=("arbitrary",), vmem_limit_bytes=VMEM_LIMIT),
        name="nsa_proj",
    )(x, g1, w_all, lvg, lvb)


def _compress_tail(h0, h1, w1_ref, pos_ref, b1_ref, w2a_ref, w2b_ref, b2_ref, n_seg):
    w1 = w1_ref[...]
    pos = pos_ref[...]
    p0 = jnp.broadcast_to(pos[0:1], (8, 1024)).astype(BF16)
    p1 = jnp.broadcast_to(pos[1:2], (8, 1024)).astype(BF16)
    c1 = b1_ref[...] + _nn(p0, w1[:, 0:128])[0:1] + _nn(p1, w1[:, 128:256])[0:1]

    def hidden(h):
        nxt = pltpu.roll(h[:, 128:256], n_seg - 1, 0)
        return _gelu(h[:, 0:128] + nxt + c1).astype(BF16)

    return _nn(hidden(h0), w2a_ref[...]) + _nn(hidden(h1), w2b_ref[...]) + b2_ref[...]


def _compress_core(lhs0, lhs1, w1_ref, pos_ref, b1_ref, w2a_ref, w2b_ref, b2_ref, n_seg):
    w1 = w1_ref[...]
    return _compress_tail(_nn(lhs0, w1), _nn(lhs1, w1), w1_ref, pos_ref, b1_ref, w2a_ref, w2b_ref, b2_ref, n_seg)


def _cmp_prompt_kernel(k_ref, v_ref, w1k_ref, w1v_ref, posk_ref, posv_ref, b1k_ref, b1v_ref,
                       w2ak_ref, w2bk_ref, w2av_ref, w2bv_ref, b2k_ref, b2v_ref,
                       kc_ref, vc_ref, u_scr):
    n_seg = k_ref.shape[0] // CMP_STRIDE
    lo = lax.broadcasted_iota(jnp.int32, (n_seg, LANES), 1) < HEAD_DIM

    def run(x_ref, w1, pos, b1, w2a, w2b, b2, out_ref):
        for pr in range(CMP_STRIDE // 2):
            a = x_ref[pl.ds(2 * pr, n_seg, stride=CMP_STRIDE), :]
            b = x_ref[pl.ds(2 * pr + 1, n_seg, stride=CMP_STRIDE), :]
            u_scr[0, :, pr * LANES:(pr + 1) * LANES] = jnp.where(lo, a, pltpu.roll(b, HEAD_DIM, 1)).astype(BF16)
            u_scr[1, :, pr * LANES:(pr + 1) * LANES] = jnp.where(lo, pltpu.roll(a, HEAD_DIM, 1), b).astype(BF16)
        out_ref[0] = _compress_core(u_scr[0], u_scr[1], w1, pos, b1, w2a, w2b, b2, n_seg)

    run(k_ref, w1k_ref, posk_ref, b1k_ref, w2ak_ref, w2bk_ref, b2k_ref, kc_ref)
    run(v_ref, w1v_ref, posv_ref, b1v_ref, w2av_ref, w2bv_ref, b2v_ref, vc_ref)


def _cmp_weight_specs(const):
    return ([pl.BlockSpec((1024, 256), const)] * 2 + [pl.BlockSpec((2, 1024), const)] * 2
            + [pl.BlockSpec((1, 128), const)] * 2 + [pl.BlockSpec((128, 128), const)] * 4
            + [pl.BlockSpec((1, 128), const)] * 2)


def _compress_prompt(kv, cw, batch, seq):
    n_seg = seq // CMP_STRIDE
    const = lambda b: (0, 0)
    return pl.pallas_call(
        _cmp_prompt_kernel,
        grid=(batch,),
        in_specs=[pl.BlockSpec((seq, LANES), lambda b: (b, 0)), pl.BlockSpec((seq, LANES), lambda b: (b, 1))]
        + _cmp_weight_specs(const),
        out_specs=[pl.BlockSpec((1, n_seg, LANES), lambda b: (b, 0, 0))] * 2,
        out_shape=[jax.ShapeDtypeStruct((batch, n_seg, LANES), F32)] * 2,
        scratch_shapes=[pltpu.VMEM((2, n_seg, 1024), BF16)],
        compiler_params=pltpu.CompilerParams(dimension_semantics=("arbitrary",), vmem_limit_bytes=VMEM_LIMIT),
        name="nsa_cmp_prompt",
    )(kv, kv, *cw)


def _rank_row(score, blk, j, rj):
    lower = jnp.where(blk < j, 1.0, 0.0)
    beat = jnp.where(score > rj, 1.0, jnp.where(score == rj, lower, 0.0))
    return jnp.sum(beat, axis=0, keepdims=True)


def _scores_t(imp_t, blk, t):
    cur = t >> 6
    forced = (blk == 0) | (blk == cur) | (blk == cur - 1)
    visible = blk * SLC_BLOCK <= t
    return jnp.where(forced, FORCE_SCORE, jnp.where(visible, imp_t, -jnp.inf)), visible


def _importance_t(cov_t, psum):
    hi = psum.astype(BF16)
    lo_ = (psum - hi.astype(F32)).astype(BF16)
    c = cov_t.astype(BF16)
    return _nt(c, hi) + _nt(c, lo_)


def _nsa_prompt_kernel(q_ref, g_ref, kc_ref, vc_ref, kvw_ref, covt_ref, ebig_ref, gexp_ref, o_ref,
                       mrun_scr, acc_scr, s_scr):
    qb = pl.program_id(1)
    t0 = qb * 128
    n_rows = N_HEADS * 128
    half = GQA * 128
    lo = lax.broadcasted_iota(jnp.int32, (128, LANES), 1) < HEAD_DIM
    t_all = t0 + (lax.broadcasted_iota(jnp.int32, (n_rows, 1), 0) & 127)
    q = q_ref[...]
    qm = jnp.concatenate(
        [jnp.where(lo if k == 0 else jnp.logical_not(lo), q[:, g * LANES:(g + 1) * LANES], 0.0)
         for k in range(N_KVH) for g in range(GQA)], axis=0).astype(BF16)

    def head(x, k, g):
        return x[(k * GQA + g) * 128:(k * GQA + g + 1) * 128]

    kc = kc_ref[0].astype(BF16)
    vc = vc_ref[0].astype(BF16)
    n_c = kc.shape[0]
    n_iota = lax.broadcasted_iota(jnp.int32, (n_rows, n_c), 1)
    p_c = _softmax_rows(_nt(qm, kc), n_iota * CMP_STRIDE + (CMP_BLOCK - 1) <= t_all)
    o_cmp = _nn(p_c.astype(BF16), vc)
    psum = [sum(head(p_c, k, g) for g in range(GQA)) for k in range(N_KVH)]

    n_s = covt_ref.shape[0]
    blk = lax.broadcasted_iota(jnp.int32, (n_s, 128), 0)
    t_lane = t0 + lax.broadcasted_iota(jnp.int32, (n_s, 128), 1)
    nsel = []
    for k in range(N_KVH):
        score, visible = _scores_t(_importance_t(covt_ref[...], psum[k]), blk, t_lane)
        cnt = jnp.concatenate([_rank_row(score, blk, j, score[j:j + 1, :]) for j in range(n_s)], axis=0)
        m_t = jnp.where((cnt < N_SELECT) & visible, 1.0, 0.0)
        m_t = jnp.concatenate([m_t, jnp.zeros((128 - n_s, 128), F32)], axis=0)
        nsel.append((1.0 - m_t.T).astype(BF16))

    lhs = jnp.concatenate([qm, jnp.concatenate([nsel[0]] * GQA + [nsel[1]] * GQA, axis=0)], axis=1)
    mrun_scr[...] = jnp.full(mrun_scr.shape, NEG, F32)
    acc_scr[...] = jnp.zeros(acc_scr.shape, F32)
    tk = SLC_TILE
    n_full = qb // (tk // 128)

    def max_tile(k0, causal):
        rhs = jnp.concatenate([kvw_ref[pl.ds(k0, tk), 0:128], ebig_ref[pl.ds(k0, tk), :]], axis=1)
        s = _nt(lhs, rhs)
        if causal:
            s = jnp.where(k0 + lax.broadcasted_iota(jnp.int32, (n_rows, tk), 1) <= t_all, s, NEG)
        s_scr[:, pl.ds(k0, tk)] = s
        m = s[:, 0:LANES]
        for j in range(1, tk // LANES):
            m = jnp.maximum(m, s[:, j * LANES:(j + 1) * LANES])
        mrun_scr[...] = jnp.maximum(mrun_scr[...], m)

    def loop1(kt, carry):
        max_tile(pl.multiple_of(kt * tk, tk), False)
        return carry

    lax.fori_loop(0, n_full, loop1, 0)
    max_tile(pl.multiple_of(n_full * tk, tk), True)
    mb = jnp.broadcast_to(jnp.max(mrun_scr[...], axis=-1, keepdims=True), (n_rows, LANES))

    def loop2(kt, carry):
        k0 = pl.multiple_of(kt * tk, tk)
        v_t = kvw_ref[pl.ds(k0, tk), 128:256]
        lov = lax.broadcasted_iota(jnp.int32, (tk, LANES), 1) < HEAD_DIM
        p = jnp.exp(s_scr[:, pl.ds(k0, tk)] - jnp.concatenate([mb] * (tk // LANES), axis=1)).astype(BF16)
        acc_scr[0] += _nn(p[0:half], jnp.where(lov, v_t, 1.0).astype(BF16))
        acc_scr[1] += _nn(p[half:n_rows], jnp.where(lov, 1.0, v_t).astype(BF16))
        return carry

    lax.fori_loop(0, n_full + 1, loop2, 0)

    s0 = pl.multiple_of(jnp.maximum(t0 - WINDOW, 0), 128)
    wk = kvw_ref[pl.ds(s0, WINDOW + 128), 256:384]
    wv = kvw_ref[pl.ds(s0, WINDOW + 128), 384:512]
    low = lax.broadcasted_iota(jnp.int32, (WINDOW + 128, LANES), 1) < HEAD_DIM
    kpos = s0 + lax.broadcasted_iota(jnp.int32, (n_rows, WINDOW + 128), 1)
    s_w = jnp.where((kpos <= t_all) & (kpos > t_all - WINDOW), _nt(qm, wk), NEG)
    p_w = jnp.exp(s_w - jnp.max(s_w, axis=-1, keepdims=True)).astype(BF16)
    win = [_nn(p_w[0:half], jnp.where(low, wv, 1.0).astype(BF16)),
           _nn(p_w[half:n_rows], jnp.where(low, 1.0, wv).astype(BF16))]

    def normalised(n0, n1):
        return jnp.where(lo, n0, n1) / pltpu.roll(jnp.where(lo, n1, n0), HEAD_DIM, 1)

    gt = g_ref[...]
    g_hi = gt.astype(BF16)
    g_lo = (gt - g_hi.astype(F32)).astype(BF16)
    gexp = _nn(g_hi, gexp_ref[...]) + _nn(g_lo, gexp_ref[...])
    for c in range(GQA):
        cols = slice(c * LANES, (c + 1) * LANES)
        rows = slice(c * 128, (c + 1) * 128)
        o_c = jnp.where(lo, head(o_cmp, 0, c), head(o_cmp, 1, c))
        o_s = normalised(acc_scr[0, rows, :], acc_scr[1, rows, :])
        o_w = normalised(win[0][rows], win[1][rows])
        o_ref[:, cols] = (gexp[:, cols] * o_c + gexp[:, 512 + c * LANES:512 + (c + 1) * LANES] * o_s
                          + gexp[:, 1024 + c * LANES:1024 + (c + 1) * LANES] * o_w).astype(BF16)


def _block_bias(seq):
    m = (np.arange(seq)[:, None] // SLC_BLOCK == np.arange(LANES)[None, :]).astype(np.float32) * NEG
    return jnp.asarray(m).astype(BF16)


def _gate_expand():
    m = np.zeros((LANES, 3 * 512), np.float32)
    col = np.arange(512)
    head = col // LANES + GQA * ((col % LANES) // HEAD_DIM)
    for br in range(3):
        m[br * N_HEADS + head, br * 512 + col] = 1.0
    return jnp.asarray(m).astype(BF16)


def _nsa_prompt(q, gate, kc, vc, kvw, cov_t, batch, seq):
    n_qb = seq // 128
    n_seg = kc.shape[1]
    row = lambda b, i: (b * n_qb + i, 0)
    const = lambda b, i: (0, 0)
    return pl.pallas_call(
        _nsa_prompt_kernel,
        grid=(batch, n_qb),
        in_specs=[pl.BlockSpec((128, 512), row), pl.BlockSpec((128, 128), row),
                  pl.BlockSpec((1, n_seg, LANES), lambda b, i: (b, 0, 0)),
                  pl.BlockSpec((1, n_seg, LANES), lambda b, i: (b, 0, 0)),
                  pl.BlockSpec((seq, 512), lambda b, i: (b, 0)),
                  pl.BlockSpec(cov_t.shape, const), pl.BlockSpec((seq, LANES), const),
                  pl.BlockSpec((LANES, 3 * 512), const)],
        out_specs=pl.BlockSpec((128, 512), row),
        out_shape=jax.ShapeDtypeStruct((batch * seq, 512), BF16),
        scratch_shapes=[pltpu.VMEM((N_HEADS * 128, LANES), F32), pltpu.VMEM((N_KVH, GQA * 128, LANES), F32),
                        pltpu.VMEM((N_HEADS * 128, seq), F32)],
        compiler_params=pltpu.CompilerParams(dimension_semantics=("arbitrary", "arbitrary"),
                                             vmem_limit_bytes=VMEM_LIMIT),
        name="nsa_prompt_attn",
    )(q, gate, kc, vc, kvw, cov_t, _block_bias(seq), _gate_expand())


def _stack_queries(q, lo):
    rows = [jnp.where(lo if k == 0 else jnp.logical_not(lo), q[:, g * LANES:(g + 1) * LANES], 0.0)
            for k in range(N_KVH) for g in range(GQA)]
    return jnp.concatenate(rows, axis=0).astype(BF16)


def _fetch_pages(pt_ref, cache_ref, buf, sem, n_pages, half):
    b = pl.program_id(0)
    slot = lax.rem(b, 2)

    def copies(seq, dst_slot):
        return [pltpu.make_async_copy(cache_ref.at[pt_ref[seq * n_pages + p], pl.ds(2 * half, 2)],
                                      buf.at[dst_slot, p], sem.at[dst_slot]) for p in range(n_pages)]

    @pl.when(b == 0)
    def _():
        for c in copies(0, 0):
            c.start()

    @pl.when(b + 1 < pl.num_programs(0))
    def _():
        for c in copies(b + 1, 1 - slot):
            c.start()

    for c in copies(b, slot):
        c.wait()
    return [buf.at[slot, p] for p in range(n_pages)]


def _smp_cmp_kernel(n_pages, pt_ref, cache_ref, *refs):
    (q_ref, perm_ref, wbdk_ref, wbdv_ref, w1k_ref, w1v_ref, posk_ref, posv_ref, b1k_ref, b1v_ref, w2ak_ref,
     w2bk_ref, w2av_ref, w2bv_ref, b2k_ref, b2v_ref, covt_ref, ocmp_ref, msel_ref, u_scr, page_buf, page_sem) = refs
    pages = _fetch_pages(pt_ref, cache_ref, page_buf, page_sem, n_pages, 0)
    seg_pp = PAGE // CMP_STRIDE
    n_seg = n_pages * seg_pp
    past = n_pages * PAGE

    perm = perm_ref[...]
    for pp in range(n_pages // 2):
        ra = _nt(perm, jnp.concatenate([pages[2 * pp][0], pages[2 * pp][1]], axis=0).astype(BF16))
        rb = _nt(perm, jnp.concatenate([pages[2 * pp + 1][0], pages[2 * pp + 1][1]], axis=0).astype(BF16))
        rows = slice(pp * 2 * seg_pp, (pp + 1) * 2 * seg_pp)
        for s in range(CMP_STRIDE):
            piece = jnp.concatenate([ra[s * seg_pp:(s + 1) * seg_pp], rb[s * seg_pp:(s + 1) * seg_pp]], axis=0)
            piece = piece.astype(BF16)
            u_scr[0, rows, s * LANES:(s + 1) * LANES] = piece[:, 0:LANES]
            u_scr[1, rows, s * LANES:(s + 1) * LANES] = piece[:, LANES:2 * LANES]

    hk = _nn(u_scr[0], wbdk_ref[...])
    hv = _nn(u_scr[1], wbdv_ref[...])
    kc = _compress_tail(hk[:, 0:256], hk[:, 256:512], w1k_ref, posk_ref, b1k_ref, w2ak_ref, w2bk_ref, b2k_ref, n_seg)
    vc = _compress_tail(hv[:, 0:256], hv[:, 256:512], w1v_ref, posv_ref, b1v_ref, w2av_ref, w2bv_ref, b2v_ref, n_seg)

    q = q_ref[...]
    n_q = q.shape[0]
    lo8 = lax.broadcasted_iota(jnp.int32, (n_q, LANES), 1) < HEAD_DIM
    q_all = _stack_queries(q, lo8)
    rows = lax.broadcasted_iota(jnp.int32, (N_HEADS * n_q, n_seg), 0)
    t_row = past + (rows & (n_q - 1))
    n_iota = lax.broadcasted_iota(jnp.int32, (N_HEADS * n_q, n_seg), 1)
    p = _softmax_rows(_nt(q_all, kc.astype(BF16)), n_iota * CMP_STRIDE + (CMP_BLOCK - 1) <= t_row)
    o = _nn(p.astype(BF16), vc.astype(BF16))
    half = GQA * n_q
    for c in range(GQA):
        ocmp_ref[:, c * LANES:(c + 1) * LANES] = jnp.where(lo8, o[c * n_q:(c + 1) * n_q],
                                                         o[half + c * n_q:half + (c + 1) * n_q])
    psum = [sum(p[k * half + g * n_q:k * half + (g + 1) * n_q] for g in range(GQA)) for k in range(N_KVH)]
    psum = jnp.concatenate(psum + [jnp.zeros((LANES - N_KVH * n_q, n_seg), F32)], axis=0)

    n_sp = covt_ref.shape[0]
    n_s = past // SLC_BLOCK + 1
    rows_used = -(-n_s // 8) * 8
    n_r = N_KVH * n_q
    blk_t = lax.broadcasted_iota(jnp.int32, (n_sp, LANES), 0)
    t_lane = past + (lax.broadcasted_iota(jnp.int32, (n_sp, LANES), 1) & (n_q - 1))
    score_t, visible_t = _scores_t(_importance_t(covt_ref[...], psum), blk_t, t_lane)
    score = score_t.T[0:n_r]
    visible = jnp.where(visible_t, 1.0, 0.0).T[0:n_r] > 0.5
    ii = lax.broadcasted_iota(jnp.int32, (rows_used, n_sp), 0)
    jj = lax.broadcasted_iota(jnp.int32, (rows_used, n_sp), 1)
    lower = jnp.where(ii < jj, 1.0, 0.0)
    cnt = []
    for r in range(n_r):
        col = score_t[0:rows_used, r:r + 1]
        row = score[r:r + 1, :]
        beat = jnp.where(col > row, 1.0, jnp.where(col == row, lower, 0.0))
        cnt.append(jnp.sum(beat, axis=0, keepdims=True))
    cnt = jnp.concatenate(cnt, axis=0)
    msel_ref[...] = jnp.where((cnt < N_SELECT) & visible, 1.0, 0.0)


def _page_scratch(n_pages):
    return [pltpu.VMEM((2, n_pages, 2, LANES, PAGE), F32), pltpu.SemaphoreType.DMA((2,))]


def _segment_perm():
    r = np.arange(PAGE)
    m = np.zeros((PAGE, PAGE), np.float32)
    m[r, (r % (PAGE // CMP_STRIDE)) * CMP_STRIDE + r // (PAGE // CMP_STRIDE)] = 1.0
    return jnp.asarray(m).astype(BF16)


def _blockdiag_w1(w1):
    w = jnp.transpose(w1.reshape(2, CMP_STRIDE, HEAD_DIM, CMP_HIDDEN), (1, 2, 0, 3))
    w = w.reshape(CMP_STRIDE, HEAD_DIM, 2 * CMP_HIDDEN)
    z = jnp.zeros_like(w)
    out = jnp.concatenate([jnp.concatenate([w, z], axis=2), jnp.concatenate([z, w], axis=2)], axis=1)
    return out.reshape(CMP_STRIDE * LANES, 4 * CMP_HIDDEN).astype(BF16)


def _sample_compress(pt, cache, q_s, wbd, cw, cov_t):
    n_seq, n_q, _ = q_s.shape
    n_pages = pt.shape[0] // n_seq
    n_seg = n_pages * (PAGE // CMP_STRIDE)
    n_sp = cov_t.shape[0]
    const = lambda b, pt: (0, 0)
    kern = functools.partial(_smp_cmp_kernel, n_pages)
    return pl.pallas_call(
        kern,
        grid_spec=pltpu.PrefetchScalarGridSpec(
            num_scalar_prefetch=1, grid=(n_seq,),
            in_specs=[pl.BlockSpec(memory_space=pl.ANY)]
            + [pl.BlockSpec((None, n_q, 512), lambda b, pt: (b, 0, 0)), pl.BlockSpec((PAGE, PAGE), const)]
            + [pl.BlockSpec((CMP_STRIDE * LANES, 4 * CMP_HIDDEN), const)] * 2
            + _cmp_weight_specs(const) + [pl.BlockSpec(cov_t.shape, const)],
            out_specs=[pl.BlockSpec((None, n_q, 512), lambda b, pt: (b, 0, 0)),
                       pl.BlockSpec((None, N_KVH * n_q, n_sp), lambda b, pt: (b, 0, 0))],
            scratch_shapes=[pltpu.VMEM((2, n_seg, CMP_STRIDE * LANES), BF16)] + _page_scratch(n_pages)),
        out_shape=[jax.ShapeDtypeStruct((n_seq, n_q, 512), F32),
                   jax.ShapeDtypeStruct((n_seq, N_KVH * n_q, n_sp), F32)],
        compiler_params=pltpu.CompilerParams(dimension_semantics=("arbitrary",), vmem_limit_bytes=VMEM_LIMIT),
        name="nsa_sample_cmp",
    )(pt, cache, q_s, _segment_perm(), *wbd, *cw, cov_t)


def _smp_attn_kernel(n_pages, pt_ref, cache_ref, *refs):
    (q_ref, msel_ref, ebig_ref, kvn_ref, win_ref, wkvn_ref, g_ref, ocmp_ref, o_ref, page_buf, page_sem) = refs
    pages = _fetch_pages(pt_ref, cache_ref, page_buf, page_sem, n_pages, 1)
    past = n_pages * PAGE
    q = q_ref[...]
    n_q = q.shape[0]
    n_rows = N_HEADS * n_q
    lo8 = lax.broadcasted_iota(jnp.int32, (n_q, LANES), 1) < HEAD_DIM
    q_all = _stack_queries(q, lo8)
    qi = lax.broadcasted_iota(jnp.int32, (n_rows, 1), 0) & (n_q - 1)

    def pad_rows(x):
        return jnp.concatenate([x, jnp.zeros((LANES - n_q, LANES), F32)], axis=0).astype(BF16)

    def rep(x):
        return jnp.concatenate([x[0:n_q]] * GQA + [x[n_q:2 * n_q]] * GQA, axis=0)

    msel = msel_ref[...]
    n_blk = past // SLC_BLOCK
    lhs = jnp.concatenate([q_all, rep(1.0 - msel[:, 0:LANES]).astype(BF16)], axis=1)
    s_pages = [_nn(lhs, jnp.concatenate([pages[p][0].astype(BF16), ebig_ref[:, p * PAGE:(p + 1) * PAGE]], axis=0))
               for p in range(n_pages)]
    kvn = kvn_ref[...]
    k_tail = pad_rows(kvn[:, 256:384])
    v_tail = pad_rows(kvn[:, 384:512])
    ti = lax.broadcasted_iota(jnp.int32, (n_rows, LANES), 1)
    ok_tail = (rep(jnp.broadcast_to(msel[:, n_blk:n_blk + 1], (2 * n_q, LANES))) > 0.5) & (ti <= qi)
    s_tail = jnp.where(ok_tail, _nt(q_all, k_tail), NEG)
    m_run = s_tail
    for s_p in s_pages:
        m_run = jnp.maximum(m_run, s_p)
    m = jnp.max(m_run, axis=-1, keepdims=True)
    p_tail = jnp.exp(s_tail - m)
    l_run = p_tail
    pv = _nn(p_tail.astype(BF16), v_tail)
    for p, s_p in enumerate(s_pages):
        p_p = jnp.exp(s_p - m)
        l_run = l_run + p_p
        pv = pv + _nt(p_p.astype(BF16), pages[p][1].astype(BF16))
    o_slc = pv / jnp.sum(l_run, axis=-1, keepdims=True)

    wk = win_ref[0].astype(BF16)
    wv = win_ref[1].astype(BF16)
    wn = wkvn_ref[...]
    wk_tail = pad_rows(wn[:, 0:128])
    wv_tail = pad_rows(wn[:, 128:256])
    mi = lax.broadcasted_iota(jnp.int32, (n_rows, WINDOW), 1)
    sw_main = jnp.where(mi > qi, _nn(q_all, wk), NEG)
    sw_tail = jnp.where(ti <= qi, _nt(q_all, wk_tail), NEG)
    mw = jnp.maximum(jnp.max(sw_main, axis=-1, keepdims=True), jnp.max(sw_tail, axis=-1, keepdims=True))
    pw_main = jnp.exp(sw_main - mw)
    pw_tail = jnp.exp(sw_tail - mw)
    lw = jnp.sum(pw_main, axis=-1, keepdims=True) + jnp.sum(pw_tail, axis=-1, keepdims=True)
    o_win = (_nt(pw_main.astype(BF16), wv) + _nn(pw_tail.astype(BF16), wv_tail)) / lw

    gt = g_ref[...]
    ocmp = ocmp_ref[...]
    half = GQA * n_q
    for c in range(GQA):
        def pick(o):
            return jnp.where(lo8, o[c * n_q:(c + 1) * n_q], o[half + c * n_q:half + (c + 1) * n_q])

        def gate(br):
            return jnp.where(lo8, gt[:, br * 8 + c:br * 8 + c + 1], gt[:, br * 8 + GQA + c:br * 8 + GQA + c + 1])

        o_ref[:, c * LANES:(c + 1) * LANES] = (gate(0) * ocmp[:, c * LANES:(c + 1) * LANES]
                                               + gate(1) * pick(o_slc) + gate(2) * pick(o_win))


def _sample_attend(pt, cache, q_s, msel, expand, kv_new, win_state, wkv_new, gate, o_cmp):
    n_seq, n_q, _ = q_s.shape
    n_pages = pt.shape[0] // n_seq
    past = n_pages * PAGE
    n_sp = msel.shape[-1]
    seq3 = lambda b, pt: (b, 0, 0)
    kern = functools.partial(_smp_attn_kernel, n_pages)
    return pl.pallas_call(
        kern,
        grid_spec=pltpu.PrefetchScalarGridSpec(
            num_scalar_prefetch=1, grid=(n_seq,),
            in_specs=[pl.BlockSpec(memory_space=pl.ANY)]
            + [pl.BlockSpec((None, n_q, 512), seq3), pl.BlockSpec((None, N_KVH * n_q, n_sp), seq3),
               pl.BlockSpec(expand.shape, lambda b, pt: (0, 0)),
               pl.BlockSpec((None, n_q, 512), seq3),
               pl.BlockSpec((None, 2, LANES, WINDOW), lambda b, pt: (b, 0, 0, 0)),
               pl.BlockSpec((None, n_q, 256), seq3), pl.BlockSpec((None, n_q, 128), seq3),
               pl.BlockSpec((None, n_q, 512), seq3)],
            out_specs=pl.BlockSpec((None, n_q, 512), seq3),
            scratch_shapes=_page_scratch(n_pages)),
        out_shape=jax.ShapeDtypeStruct((n_seq, n_q, 512), F32),
        compiler_params=pltpu.CompilerParams(dimension_semantics=("arbitrary",), vmem_limit_bytes=VMEM_LIMIT),
        name="nsa_sample_attn",
    )(pt, cache, q_s, msel, expand, kv_new, win_state, wkv_new, gate, o_cmp)


def _ffn_kernel(x_ref, oa_ref, u_ref, v_ref, ws_ref, bs_ref, woa_ref, wob_ref, g2_ref, w1_ref, w2_ref, gf_ref,
                y_ref, h_scr, hn_scr, ob_scr, acc_scr):
    j = pl.program_id(1)
    tm = x_ref.shape[0]

    @pl.when(j == 0)
    def _():
        lo = lax.broadcasted_iota(jnp.int32, (CHUNK, LANES), 1) < HEAD_DIM
        for ch in range(tm // CHUNK):
            rows = slice(ch * CHUNK, (ch + 1) * CHUNK)
            cols = []
            for c in range(4):
                vc = v_ref[rows, c * LANES:(c + 1) * LANES].astype(BF16)
                cols.append(jnp.where(lo, _nn(ws_ref[2 * c], vc), _nn(ws_ref[2 * c + 1], vc)))
            s = jnp.concatenate(cols, axis=1) + bs_ref[...]
            ob_scr[rows, :] = (u_ref[rows, :] * s).astype(BF16)
        h = x_ref[...] + _nn(oa_ref[...].astype(BF16), woa_ref[...]) + _nn(ob_scr[...], wob_ref[...])
        h_scr[...] = h
        hn_scr[...] = _rms(h, g2_ref[...]).astype(BF16)
        acc_scr[...] = jnp.zeros(acc_scr.shape, F32)

    f = jnp.maximum(_nn(hn_scr[...], w1_ref[...]), 0.0)
    acc_scr[...] += _nn((f * f).astype(BF16), w2_ref[...])

    @pl.when(j == pl.num_programs(1) - 1)
    def _():
        y_ref[...] = _rms(h_scr[...] + acc_scr[...], gf_ref[...])


def _out_ffn(x, o_a, u, v, ws, bs, woa, wob, g2, w1, w2, gf, tm=512, tf=2048):
    n = x.shape[0]
    row = lambda i, j: (i, 0)
    const = lambda i, j: (0, 0)
    return pl.pallas_call(
        _ffn_kernel,
        grid=(n // tm, D_FF // tf),
        in_specs=[pl.BlockSpec((tm, D_MODEL), row), pl.BlockSpec((tm, 512), row), pl.BlockSpec((tm, 512), row),
                  pl.BlockSpec((tm, 512), row), pl.BlockSpec((8, CHUNK, CHUNK), lambda i, j: (0, 0, 0)),
                  pl.BlockSpec((CHUNK, 512), const), pl.BlockSpec((512, D_MODEL), const),
                  pl.BlockSpec((512, D_MODEL), const), pl.BlockSpec((1, D_MODEL), const),
                  pl.BlockSpec((D_MODEL, tf), lambda i, j: (0, j)), pl.BlockSpec((tf, D_MODEL), lambda i, j: (j, 0)),
                  pl.BlockSpec((1, D_MODEL), const)],
        out_specs=pl.BlockSpec((tm, D_MODEL), row),
        out_shape=jax.ShapeDtypeStruct((n, D_MODEL), F32),
        scratch_shapes=[pltpu.VMEM((tm, D_MODEL), F32), pltpu.VMEM((tm, D_MODEL), BF16),
                        pltpu.VMEM((tm, 512), BF16), pltpu.VMEM((tm, D_MODEL), F32)],
        compiler_params=pltpu.CompilerParams(dimension_semantics=("arbitrary", "arbitrary"),
                                             vmem_limit_bytes=VMEM_LIMIT),
        name="nsa_out_ffn",
    )(x, o_a, u, v, ws, bs, woa, wob, g2, w1, w2, gf)


def _head_perm():
    j = np.arange(512)
    return ((j // 128) + 4 * ((j % 128) // 64)) * 64 + (j % 64)


def _cover_t(n_c, n_s, n_c_pad, n_s_pad):
    ci = np.arange(n_c)[:, None] * CMP_STRIDE
    sj = np.arange(n_s)[None, :] * SLC_BLOCK
    cover = np.clip(np.minimum(ci + CMP_BLOCK, sj + SLC_BLOCK) - np.maximum(ci, sj), 0, None) / CMP_BLOCK
    out = np.zeros((n_s_pad, n_c_pad), np.float32)
    out[:n_s, :n_c] = cover.T
    return jnp.asarray(out)


def _cmp_weights(cmp_w1, cmp_b1, cmp_w2, cmp_b2, cmp_pos):
    z = jnp.zeros((CMP_HIDDEN, HEAD_DIM), F32)
    w1 = [jnp.concatenate([cmp_w1[i, :1024], cmp_w1[i, 1024:]], axis=1).astype(BF16) for i in range(2)]
    pos = [cmp_pos[i].reshape(2, 1024) for i in range(2)]
    b1 = [cmp_b1[i].reshape(1, CMP_HIDDEN) for i in range(2)]
    w2a = [jnp.concatenate([cmp_w2[i], z], axis=1).astype(BF16) for i in range(2)]
    w2b = [jnp.concatenate([z, cmp_w2[i]], axis=1).astype(BF16) for i in range(2)]
    b2 = [jnp.concatenate([cmp_b2[i], cmp_b2[i]]).reshape(1, LANES) for i in range(2)]
    return (w1[0], w1[1], pos[0], pos[1], b1[0], b1[1], w2a[0], w2b[0], w2a[1], w2b[1], b2[0], b2[1])


def kernel(x_prompt, x_sample, cache_kv, state_win_kv, page_table, ln1_g, w_in, cmp_w1, cmp_b1, cmp_w2, cmp_b2,
           cmp_pos, ln_v_g, ln_v_b, w_s, b_s, w_out, ln2_g, w_ff1, w_ff2, ln_f_g):
    batch, seq, _ = x_prompt.shape
    n_seq, n_q, _ = x_sample.shape
    n_pages = page_table.shape[1]
    past = n_pages * PAGE
    perm = _head_perm()

    wi = w_in[0]
    w_all = jnp.concatenate(
        [wi[:, 0:512][:, perm], wi[:, 512:1304], jnp.zeros((D_MODEL, C_U - C_GATE - 24), F32), wi[:, 1304:2328]],
        axis=1).astype(BF16)
    g1 = ln1_g[0].reshape(1, D_MODEL)
    lvg = ln_v_g[0].reshape(1, 512)
    lvb = ln_v_b[0].reshape(1, 512)
    cw = _cmp_weights(cmp_w1[0], cmp_b1[0], cmp_w2[0], cmp_b2[0], cmp_pos[0])
    tril = jnp.tril(jnp.ones((CHUNK, CHUNK), F32))
    ws_p = (w_s[0] * tril).astype(BF16)
    bs_p = jnp.repeat(b_s[0].T, HEAD_DIM, axis=1)
    reps = CHUNK // n_q
    ws_s = jnp.einsum("ab,gij->gaibj", jnp.eye(reps, dtype=F32), (w_s[0] * tril)[:, :n_q, :n_q])
    ws_s = ws_s.reshape(8, CHUNK, CHUNK).astype(BF16)
    bs_s = jnp.tile(jnp.repeat(b_s[0].T[:n_q], HEAD_DIM, axis=1), (reps, 1))
    woa = w_out[0][:512][perm].astype(BF16)
    wob = w_out[0][512:].astype(BF16)
    g2 = ln2_g[0].reshape(1, D_MODEL)
    gf = ln_f_g.reshape(1, D_MODEL)
    w1 = w_ff1[0].astype(BF16)
    w2 = w_ff2[0].astype(BF16)

    xp = x_prompt.reshape(batch * seq, D_MODEL)
    q_p, kv_p, _, kvw_p, gate_p, u_p, v_p, kvt_p, wkvt_p = _project(xp, g1, w_all, lvg, lvb, seq)
    kc_p, vc_p = _compress_prompt(kv_p, cw, batch, seq)
    n_seg_p = seq // CMP_STRIDE
    cov_p = _cover_t(n_seg_p - 1, seq // SLC_BLOCK, n_seg_p, seq // SLC_BLOCK)
    oa_p = _nsa_prompt(q_p, gate_p, kc_p, vc_p, kvw_p, cov_p, batch, seq)
    y_p = _out_ffn(xp, oa_p, u_p, v_p, ws_p, bs_p, woa, wob, g2, w1, w2, gf)

    xs = x_sample.reshape(n_seq * n_q, D_MODEL)
    q_s, kv_s, wkv_s, _, gate_s, u_s, v_s, _, _ = _project(xs, g1, w_all, lvg, lvb, n_seq * n_q)
    cache = jnp.transpose(cache_kv[0], (0, 2, 3, 4, 1)).reshape(-1, 4, LANES, PAGE)
    win_state = jnp.transpose(state_win_kv[0], (0, 2, 3, 4, 1)).reshape(n_seq, 2, LANES, -1)
    pt = page_table.reshape(-1)
    n_seg_s = past // CMP_STRIDE
    n_s = past // SLC_BLOCK + 1
    n_sp = -(-n_s // LANES) * LANES
    cov_s = _cover_t(n_seg_s, n_s, n_seg_s, n_sp)
    q_s3 = q_s.reshape(n_seq, n_q, 512)
    wbd = (_blockdiag_w1(cmp_w1[0, 0]), _blockdiag_w1(cmp_w1[0, 1]))
    ocmp_s, msel = _sample_compress(pt, cache, q_s3, wbd, cw, cov_s)
    ebig_t = jnp.transpose(_block_bias(past))
    oa_s = _sample_attend(pt, cache, q_s3, msel, ebig_t, kv_s.reshape(n_seq, n_q, 512), win_state,
                          wkv_s.reshape(n_seq, n_q, 256), gate_s.reshape(n_seq, n_q, 128), ocmp_s)
    y_s = _out_ffn(xs, oa_s.reshape(n_seq * n_q, 512), u_s, v_s, ws_s, bs_s, woa, wob, g2, w1, w2, gf)

    keep = min(WINDOW, seq)
    new_kv_p = jnp.transpose(kvt_p.reshape(batch, 4, N_KVH, HEAD_DIM, seq), (0, 4, 1, 2, 3))
    new_win_p = jnp.transpose(wkvt_p[:, :, seq - keep:].reshape(batch, 2, N_KVH, HEAD_DIM, keep), (0, 4, 1, 2, 3))
    new_win_s = jnp.concatenate([state_win_kv[0], wkv_s.reshape(n_seq, n_q, 2, N_KVH, HEAD_DIM)], axis=1)[:, n_q:]
    return (y_p.reshape(batch, seq, D_MODEL),
            y_s.reshape(n_seq, n_q, D_MODEL),
            new_kv_p[None],
            kv_s.reshape(1, n_seq, n_q, 4, N_KVH, HEAD_DIM),
            new_win_p[None],
            new_win_s[None],
            v_s.reshape(1, n_seq, n_q, 512))
```

```python
import functools

import numpy as np
import jax
import jax.numpy as jnp
from jax import lax
from jax.experimental import pallas as pl
from jax.experimental.pallas import tpu as pltpu

F32 = jnp.float32
BF16 = jnp.bfloat16

D_MODEL = 1024
HEAD_DIM = 64
N_HEADS = 8
N_KVH = 2
GQA = 4
CMP_BLOCK = 32
CMP_STRIDE = 16
CMP_HIDDEN = 128
SLC_BLOCK = 64
N_SELECT = 16
WINDOW = 512
CHUNK = 128
D_FF = 4096
PAGE = 128
EPS = 1e-6
NEG = -1e30
LOG2E = 1.4426950408889634
FORCE_SCORE = 1e9
LANES = 128
VMEM_LIMIT = 56 * 1024 * 1024
SLC_TILE = 512

C_Q, C_KV, C_WIN, C_GATE, C_U, C_V, C_END = 0, 512, 1024, 1280, 1408, 1920, 2432


def _nn(a, b):
    return jnp.dot(a, b, preferred_element_type=F32)


def _nt(a, b):
    return lax.dot_general(a, b, (((1,), (1,)), ((), ())), preferred_element_type=F32)


def _gelu(x):
    return 0.5 * x * (1.0 + jnp.tanh(0.7978845608028654 * (x + 0.044715 * (x * x * x))))


def _rms(x, g):
    return x * lax.rsqrt(jnp.mean(x * x, axis=-1, keepdims=True) + EPS) * g


def _softmax_rows(s, mask):
    s = jnp.where(mask, s, NEG)
    m = jnp.max(s, axis=-1, keepdims=True)
    p = jnp.where(mask, jnp.exp2(s - m), 0.0)
    l = jnp.sum(p, axis=-1, keepdims=True)
    return p / jnp.maximum(l, 1e-30)


def _proj_kernel(x_ref, g1_ref, w_ref, lvg_ref, lvb_ref,
                 q_ref, kv_ref, wkv_ref, kvw_ref, gate_ref, u_ref, v_ref, kvt_ref, wkvt_ref):
    x = x_ref[...]
    h = _rms(x, g1_ref[...]).astype(BF16)

    def z(a, b):
        return _nn(h, w_ref[:, a:b])

    q_ref[...] = z(C_Q, C_KV) * (HEAD_DIM ** -0.5 * LOG2E)
    kv = z(C_KV, C_WIN)
    kv_ref[...] = kv
    kvt_ref[...] = kv.T
    wkv = z(C_WIN, C_GATE)
    wkv_ref[...] = wkv
    wkvt_ref[...] = wkv.T
    kvw_ref[:, 0:256] = kv[:, 256:512].astype(BF16)
    kvw_ref[:, 256:512] = wkv.astype(BF16)
    gate_ref[...] = jax.nn.sigmoid(z(C_GATE, C_U))
    u_ref[...] = _gelu(z(C_U, C_V))
    zv = _gelu(z(C_V, C_END))
    mu = jnp.mean(zv, axis=-1, keepdims=True)
    var = jnp.mean(jnp.square(zv - mu), axis=-1, keepdims=True)
    v_ref[...] = (zv - mu) * lax.rsqrt(var + EPS) * lvg_ref[...] + lvb_ref[...]


def _project(x, g1, w_all, lvg, lvb, seq, tm=512):
    n = x.shape[0]
    per = seq // tm
    row = lambda i: (i, 0)
    const = lambda i: (0, 0)
    colmajor = lambda i: (i // per, 0, i % per)
    widths = (512, 512, 256, 512, 128, 512, 512)
    dtypes = (F32, F32, F32, BF16, F32, F32, F32)
    return pl.pallas_call(
        _proj_kernel,
        grid=(n // tm,),
        in_specs=[pl.BlockSpec((tm, D_MODEL), row), pl.BlockSpec((1, D_MODEL), const),
                  pl.BlockSpec((D_MODEL, C_END), const), pl.BlockSpec((1, 512), const),
                  pl.BlockSpec((1, 512), const)],
        out_specs=[pl.BlockSpec((tm, w), row) for w in widths]
        + [pl.BlockSpec((None, 512, tm), colmajor), pl.BlockSpec((None, 256, tm), colmajor)],
        out_shape=[jax.ShapeDtypeStruct((n, w), d) for w, d in zip(widths, dtypes)]
        + [jax.ShapeDtypeStruct((n // seq, 512, seq), F32), jax.ShapeDtypeStruct((n // seq, 256, seq), F32)],
        compiler_params=pltpu.CompilerParams(dimension_semantics=("arbitrary",), vmem_limit_bytes=VMEM_LIMIT),
        name="nsa_proj",
    )(x, g1, w_all, lvg, lvb)


def _compress_tail(h0, h1, w1_ref, pos_ref, b1_ref, w2a_ref, w2b_ref, b2_ref, n_seg):
    w1 = w1_ref[...]
    pos = pos_ref[...]
    p0 = jnp.broadcast_to(pos[0:1], (8, 1024)).astype(BF16)
    p1 = jnp.broadcast_to(pos[1:2], (8, 1024)).astype(BF16)
    c1 = b1_ref[...] + _nn(p0, w1[:, 0:128])[0:1] + _nn(p1, w1[:, 128:256])[0:1]

    def hidden(h):
        nxt = pltpu.roll(h[:, 128:256], n_seg - 1, 0)
        return _gelu(h[:, 0:128] + nxt + c1).astype(BF16)

    return _nn(hidden(h0), w2a_ref[...]) + _nn(hidden(h1), w2b_ref[...]) + b2_ref[...]


def _compress_core(lhs0, lhs1, w1_ref, pos_ref, b1_ref, w2a_ref, w2b_ref, b2_ref, n_seg):
    w1 = w1_ref[...]
    return _compress_tail(_nn(lhs0, w1), _nn(lhs1, w1), w1_ref, pos_ref, b1_ref, w2a_ref, w2b_ref, b2_ref, n_seg)


def _cmp_prompt_kernel(k_ref, v_ref, w1k_ref, w1v_ref, posk_ref, posv_ref, b1k_ref, b1v_ref,
                       w2ak_ref, w2bk_ref, w2av_ref, w2bv_ref, b2k_ref, b2v_ref,
                       kc_ref, vc_ref, u_scr):
    n_seg = k_ref.shape[0] // CMP_STRIDE
    lo = lax.broadcasted_iota(jnp.int32, (n_seg, LANES), 1) < HEAD_DIM

    def run(x_ref, w1, pos, b1, w2a, w2b, b2, out_ref):
        for pr in range(CMP_STRIDE // 2):
            a = x_ref[pl.ds(2 * pr, n_seg, stride=CMP_STRIDE), :]
            b = x_ref[pl.ds(2 * pr + 1, n_seg, stride=CMP_STRIDE), :]
            u_scr[0, :, pr * LANES:(pr + 1) * LANES] = jnp.where(lo, a, pltpu.roll(b, HEAD_DIM, 1)).astype(BF16)
            u_scr[1, :, pr * LANES:(pr + 1) * LANES] = jnp.where(lo, pltpu.roll(a, HEAD_DIM, 1), b).astype(BF16)
        out_ref[0] = _compress_core(u_scr[0], u_scr[1], w1, pos, b1, w2a, w2b, b2, n_seg)

    run(k_ref, w1k_ref, posk_ref, b1k_ref, w2ak_ref, w2bk_ref, b2k_ref, kc_ref)
    run(v_ref, w1v_ref, posv_ref, b1v_ref, w2av_ref, w2bv_ref, b2v_ref, vc_ref)


def _cmp_weight_specs(const):
    return ([pl.BlockSpec((1024, 256), const)] * 2 + [pl.BlockSpec((2, 1024), const)] * 2
            + [pl.BlockSpec((1, 128), const)] * 2 + [pl.BlockSpec((128, 128), const)] * 4
            + [pl.BlockSpec((1, 128), const)] * 2)


def _compress_prompt(kv, cw, batch, seq):
    n_seg = seq // CMP_STRIDE
    const = lambda b: (0, 0)
    return pl.pallas_call(
        _cmp_prompt_kernel,
        grid=(batch,),
        in_specs=[pl.BlockSpec((seq, LANES), lambda b: (b, 0)), pl.BlockSpec((seq, LANES), lambda b: (b, 1))]
        + _cmp_weight_specs(const),
        out_specs=[pl.BlockSpec((1, n_seg, LANES), lambda b: (b, 0, 0))] * 2,
        out_shape=[jax.ShapeDtypeStruct((batch, n_seg, LANES), F32)] * 2,
        scratch_shapes=[pltpu.VMEM((2, n_seg, 1024), BF16)],
        compiler_params=pltpu.CompilerParams(dimension_semantics=("arbitrary",), vmem_limit_bytes=VMEM_LIMIT),
        name="nsa_cmp_prompt",
    )(kv, kv, *cw)


def _rank_row(score, blk, j, rj):
    lower = jnp.where(blk < j, 1.0, 0.0)
    beat = jnp.where(score > rj, 1.0, jnp.where(score == rj, lower, 0.0))
    return jnp.sum(beat, axis=0, keepdims=True)


def _scores_t(imp_t, blk, t):
    cur = t >> 6
    forced = (blk == 0) | (blk == cur) | (blk == cur - 1)
    visible = blk * SLC_BLOCK <= t
    return jnp.where(forced, FORCE_SCORE, jnp.where(visible, imp_t, -jnp.inf)), visible


def _importance_t(cov_t, psum):
    hi = psum.astype(BF16)
    lo_ = (psum - hi.astype(F32)).astype(BF16)
    c = cov_t.astype(BF16)
    return _nt(c, hi) + _nt(c, lo_)


def _nsa_prompt_kernel(q_ref, g_ref, kc_ref, vc_ref, kvw_ref, covt_ref, ebig_ref, gexp_ref, o_ref,
                       mrun_scr, acc_scr, s_scr):
    qb = pl.program_id(1)
    t0 = qb * 128
    n_rows = N_HEADS * 128
    half = GQA * 128
    lo = lax.broadcasted_iota(jnp.int32, (128, LANES), 1) < HEAD_DIM
    t_all = t0 + (lax.broadcasted_iota(jnp.int32, (n_rows, 1), 0) & 127)
    q = q_ref[...]
    qm = jnp.concatenate(
        [jnp.where(lo if k == 0 else jnp.logical_not(lo), q[:, g * LANES:(g + 1) * LANES], 0.0)
         for k in range(N_KVH) for g in range(GQA)], axis=0).astype(BF16)

    def head(x, k, g):
        return x[(k * GQA + g) * 128:(k * GQA + g + 1) * 128]

    kc = kc_ref[0].astype(BF16)
    vc = vc_ref[0].astype(BF16)
    n_c = kc.shape[0]
    t_q = t0 + lax.broadcasted_iota(jnp.int32, (128, 1), 0)

    def all_heads(bias):
        return jnp.concatenate([bias] * N_HEADS, axis=0)

    s0 = pl.multiple_of(jnp.maximum(t0 - WINDOW, 0), 128)
    wk = kvw_ref[pl.ds(s0, WINDOW + 128), 256:384]
    wv = kvw_ref[pl.ds(s0, WINDOW + 128), 384:512]
    low = lax.broadcasted_iota(jnp.int32, (WINDOW + 128, LANES), 1) < HEAD_DIM
    kpos = s0 + lax.broadcasted_iota(jnp.int32, (128, WINDOW + 128), 1)
    s_w = _nt(qm, wk) + all_heads(jnp.where((kpos <= t_q) & (kpos > t_q - WINDOW), 0.0, NEG))
    p_w = jnp.exp2(s_w - jnp.max(s_w, axis=-1, keepdims=True)).astype(BF16)
    win = [_nn(p_w[0:half], jnp.where(low, wv, 1.0).astype(BF16)),
           _nn(p_w[half:n_rows], jnp.where(low, 1.0, wv).astype(BF16))]

    n_iota = lax.broadcasted_iota(jnp.int32, (128, n_c), 1)
    s_c = _nt(qm, kc) + all_heads(jnp.where(n_iota * CMP_STRIDE + (CMP_BLOCK - 1) <= t_q, 0.0, NEG))
    e_c = jnp.exp2(s_c - jnp.max(s_c, axis=-1, keepdims=True))
    row_ok = jnp.where(t_all >= CMP_BLOCK - 1, 1.0, 0.0)
    p_c = e_c * (row_ok / jnp.sum(e_c, axis=-1, keepdims=True))
    o_cmp = _nn(p_c.astype(BF16), vc)
    psum = [sum(head(p_c, k, g) for g in range(GQA)) for k in range(N_KVH)]

    n_s = covt_ref.shape[0]
    blk = lax.broadcasted_iota(jnp.int32, (n_s, 128), 0)
    t_lane = t0 + lax.broadcasted_iota(jnp.int32, (n_s, 128), 1)
    nsel = []
    for k in range(N_KVH):
        score, visible = _scores_t(_importance_t(covt_ref[...], psum[k]), blk, t_lane)
        cnt = jnp.concatenate([_rank_row(score, blk, j, score[j:j + 1, :]) for j in range(n_s)], axis=0)
        m_t = jnp.where((cnt < N_SELECT) & visible, 1.0, 0.0)
        m_t = jnp.concatenate([m_t, jnp.zeros((128 - n_s, 128), F32)], axis=0)
        nsel.append((1.0 - m_t.T).astype(BF16))

    lhs = jnp.concatenate([qm, jnp.concatenate([nsel[0]] * GQA + [nsel[1]] * GQA, axis=0)], axis=1)
    mrun_scr[...] = jnp.full(mrun_scr.shape, NEG, F32)
    acc_scr[...] = jnp.zeros(acc_scr.shape, F32)
    tk = SLC_TILE
    n_full = qb // (tk // 128)

    def max_tile(k0, causal):
        rhs = jnp.concatenate([kvw_ref[pl.ds(k0, tk), 0:128], ebig_ref[pl.ds(k0, tk), :]], axis=1)
        s = _nt(lhs, rhs)
        if causal:
            s = s + all_heads(jnp.where(k0 + lax.broadcasted_iota(jnp.int32, (128, tk), 1) <= t_q, 0.0, NEG))
        s_scr[:, pl.ds(k0, tk)] = s
        m = s[:, 0:LANES]
        for j in range(1, tk // LANES):
            m = jnp.maximum(m, s[:, j * LANES:(j + 1) * LANES])
        mrun_scr[...] = jnp.maximum(mrun_scr[...], m)

    def loop1(kt, carry):
        max_tile(pl.multiple_of(kt * tk, tk), False)
        return carry

    max_tile(pl.multiple_of(n_full * tk, tk), True)
    lax.fori_loop(0, n_full, loop1, 0)
    mb =jnp.broadcast_to(jnp.max(mrun_scr[...], axis=-1, keepdims=True), (n_rows, LANES))

    def loop2(kt, carry):
        k0 = pl.multiple_of(kt * tk, tk)
        v_t = kvw_ref[pl.ds(k0, tk), 128:256]
        lov = lax.broadcasted_iota(jnp.int32, (tk, LANES), 1) < HEAD_DIM
        p = jnp.exp2(s_scr[:, pl.ds(k0, tk)] - jnp.concatenate([mb] * (tk // LANES), axis=1)).astype(BF16)
        acc_scr[0] += _nn(p[0:half], jnp.where(lov, v_t, 1.0).astype(BF16))
        acc_scr[1] += _nn(p[half:n_rows], jnp.where(lov, 1.0, v_t).astype(BF16))
        return carry

    lax.fori_loop(0, n_full + 1, loop2, 0)

    def normalised(n0, n1):
        return jnp.where(lo, n0, n1) / pltpu.roll(jnp.where(lo, n1, n0), HEAD_DIM, 1)

    gt = g_ref[...]
    g_hi = gt.astype(BF16)
    g_lo = (gt - g_hi.astype(F32)).astype(BF16)
    gexp = _nn(g_hi, gexp_ref[...]) + _nn(g_lo, gexp_ref[...])
    for c in range(GQA):
        cols = slice(c * LANES, (c + 1) * LANES)
        rows = slice(c * 128, (c + 1) * 128)
        o_c = jnp.where(lo, head(o_cmp, 0, c), head(o_cmp, 1, c))
        o_s = normalised(acc_scr[0, rows, :], acc_scr[1, rows, :])
        o_w = normalised(win[0][rows], win[1][rows])
        o_ref[:, cols] = (gexp[:, cols] * o_c + gexp[:, 512 + c * LANES:512 + (c + 1) * LANES] * o_s
                          + gexp[:, 1024 + c * LANES:1024 + (c + 1) * LANES] * o_w).astype(BF16)


def _block_bias(seq):
    m = (np.arange(seq)[:, None] // SLC_BLOCK == np.arange(LANES)[None, :]).astype(np.float32) * NEG
    return jnp.asarray(m).astype(BF16)


def _gate_expand():
    m = np.zeros((LANES, 3 * 512), np.float32)
    col = np.arange(512)
    head = col // LANES + GQA * ((col % LANES) // HEAD_DIM)
    for br in range(3):
        m[br * N_HEADS + head, br * 512 + col] = 1.0
    return jnp.asarray(m).astype(BF16)


def _nsa_prompt(q, gate, kc, vc, kvw, cov_t, batch, seq):
    n_qb = seq // 128
    n_seg = kc.shape[1]
    row = lambda b, i: (b * n_qb + i, 0)
    const = lambda b, i: (0, 0)
    return pl.pallas_call(
        _nsa_prompt_kernel,
        grid=(batch, n_qb),
        in_specs=[pl.BlockSpec((128, 512), row), pl.BlockSpec((128, 128), row),
                  pl.BlockSpec((1, n_seg, LANES), lambda b, i: (b, 0, 0)),
                  pl.BlockSpec((1, n_seg, LANES), lambda b, i: (b, 0, 0)),
                  pl.BlockSpec((seq, 512), lambda b, i: (b, 0)),
                  pl.BlockSpec(cov_t.shape, const), pl.BlockSpec((seq, LANES), const),
                  pl.BlockSpec((LANES, 3 * 512), const)],
        out_specs=pl.BlockSpec((128, 512), row),
        out_shape=jax.ShapeDtypeStruct((batch * seq, 512), BF16),
        scratch_shapes=[pltpu.VMEM((N_HEADS * 128, LANES), F32), pltpu.VMEM((N_KVH, GQA * 128, LANES), F32),
                        pltpu.VMEM((N_HEADS * 128, seq), F32)],
        compiler_params=pltpu.CompilerParams(dimension_semantics=("arbitrary", "arbitrary"),
                                             vmem_limit_bytes=VMEM_LIMIT),
        name="nsa_prompt_attn",
    )(q, gate, kc, vc, kvw, cov_t, _block_bias(seq), _gate_expand())


def _stack_queries(q, lo):
    rows = [jnp.where(lo if k == 0 else jnp.logical_not(lo), q[:, g * LANES:(g + 1) * LANES], 0.0)
            for k in range(N_KVH) for g in range(GQA)]
    return jnp.concatenate(rows, axis=0).astype(BF16)


def _fetch_pages(pt_ref, cache_ref, buf, sem, n_pages, half):
    b = pl.program_id(0)
    slot = lax.rem(b, 2)

    def copies(seq, dst_slot):
        return [pltpu.make_async_copy(cache_ref.at[pt_ref[seq * n_pages + p], pl.ds(2 * half, 2)],
                                      buf.at[dst_slot, p], sem.at[dst_slot]) for p in range(n_pages)]

    @pl.when(b == 0)
    def _():
        for c in copies(0, 0):
            c.start()

    @pl.when(b + 1 < pl.num_programs(0))
    def _():
        for c in copies(b + 1, 1 - slot):
            c.start()

    for c in copies(b, slot):
        c.wait()
    return [buf.at[slot, p] for p in range(n_pages)]


def _smp_cmp_kernel(n_pages, pt_ref, cache_ref, *refs):
    (q_ref, perm_ref, wbdk_ref, wbdv_ref, w1k_ref, w1v_ref, posk_ref, posv_ref, b1k_ref, b1v_ref, w2ak_ref,
     w2bk_ref, w2av_ref, w2bv_ref, b2k_ref, b2v_ref, covt_ref, ocmp_ref, msel_ref, u_scr, page_buf, page_sem) = refs
    pages = _fetch_pages(pt_ref, cache_ref, page_buf, page_sem, n_pages, 0)
    seg_pp = PAGE // CMP_STRIDE
    n_seg = n_pages * seg_pp
    past = n_pages * PAGE

    perm = perm_ref[...]
    for pp in range(n_pages // 2):
        ra = _nt(perm, jnp.concatenate([pages[2 * pp][0], pages[2 * pp][1]], axis=0).astype(BF16))
        rb = _nt(perm, jnp.concatenate([pages[2 * pp + 1][0], pages[2 * pp + 1][1]], axis=0).astype(BF16))
        rows = slice(pp * 2 * seg_pp, (pp + 1) * 2 * seg_pp)
        for s in range(CMP_STRIDE):
            piece = jnp.concatenate([ra[s * seg_pp:(s + 1) * seg_pp], rb[s * seg_pp:(s + 1) * seg_pp]], axis=0)
            piece = piece.astype(BF16)
            u_scr[0, rows, s * LANES:(s + 1) * LANES] = piece[:, 0:LANES]
            u_scr[1, rows, s * LANES:(s + 1) * LANES] = piece[:, LANES:2 * LANES]

    hk = _nn(u_scr[0], wbdk_ref[...])
    hv = _nn(u_scr[1], wbdv_ref[...])
    kc = _compress_tail(hk[:, 0:256], hk[:, 256:512], w1k_ref, posk_ref, b1k_ref, w2ak_ref, w2bk_ref, b2k_ref, n_seg)
    vc = _compress_tail(hv[:, 0:256], hv[:, 256:512], w1v_ref, posv_ref, b1v_ref, w2av_ref, w2bv_ref, b2v_ref, n_seg)

    q = q_ref[...]
    n_q = q.shape[0]
    lo8 = lax.broadcasted_iota(jnp.int32, (n_q, LANES), 1) < HEAD_DIM
    q_all = _stack_queries(q, lo8)
    rows = lax.broadcasted_iota(jnp.int32, (N_HEADS * n_q, n_seg), 0)
    t_row = past + (rows & (n_q - 1))
    n_iota = lax.broadcasted_iota(jnp.int32, (N_HEADS * n_q, n_seg), 1)
    p = _softmax_rows(_nt(q_all, kc.astype(BF16)), n_iota * CMP_STRIDE + (CMP_BLOCK - 1) <= t_row)
    o = _nn(p.astype(BF16), vc.astype(BF16))
    half = GQA * n_q
    for c in range(GQA):
        ocmp_ref[:, c * LANES:(c + 1) * LANES] = jnp.where(lo8, o[c * n_q:(c + 1) * n_q],
                                                         o[half + c * n_q:half + (c + 1) * n_q])
    psum = [sum(p[k * half + g * n_q:k * half + (g + 1) * n_q] for g in range(GQA)) for k in range(N_KVH)]
    psum = jnp.concatenate(psum + [jnp.zeros((LANES - N_KVH * n_q, n_seg), F32)], axis=0)

    n_sp = covt_ref.shape[0]
    n_s = past // SLC_BLOCK + 1
    rows_used = -(-n_s // 8) * 8
    n_r = N_KVH * n_q
    blk_t = lax.broadcasted_iota(jnp.int32, (n_sp, LANES), 0)
    t_lane = past + (lax.broadcasted_iota(jnp.int32, (n_sp, LANES), 1) & (n_q - 1))
    score_t, visible_t = _scores_t(_importance_t(covt_ref[...], psum), blk_t, t_lane)
    score = score_t.T[0:n_r]
    visible = jnp.where(visible_t, 1.0, 0.0).T[0:n_r] > 0.5
    ii = lax.broadcasted_iota(jnp.int32, (rows_used, n_sp), 0)
    jj = lax.broadcasted_iota(jnp.int32, (rows_used, n_sp), 1)
    lower = jnp.where(ii < jj, 1.0, 0.0)
    cnt = []
    for r in range(n_r):
        col = score_t[0:rows_used, r:r + 1]
        row = score[r:r + 1, :]
        beat = jnp.where(col > row, 1.0, jnp.where(col == row, lower, 0.0))
        cnt.append(jnp.sum(beat, axis=0, keepdims=True))
    cnt = jnp.concatenate(cnt, axis=0)
    msel_ref[...] = jnp.where((cnt < N_SELECT) & visible, 1.0, 0.0)


def _page_scratch(n_pages):
    return [pltpu.VMEM((2, n_pages, 2, LANES, PAGE), F32), pltpu.SemaphoreType.DMA((2,))]


def _segment_perm():
    r = np.arange(PAGE)
    m = np.zeros((PAGE, PAGE), np.float32)
    m[r, (r % (PAGE // CMP_STRIDE)) * CMP_STRIDE + r // (PAGE // CMP_STRIDE)] = 1.0
    return jnp.asarray(m).astype(BF16)


def _blockdiag_w1(w1):
    w = jnp.transpose(w1.reshape(2, CMP_STRIDE, HEAD_DIM, CMP_HIDDEN), (1, 2, 0, 3))
    w = w.reshape(CMP_STRIDE, HEAD_DIM, 2 * CMP_HIDDEN)
    z = jnp.zeros_like(w)
    out = jnp.concatenate([jnp.concatenate([w, z], axis=2), jnp.concatenate([z, w], axis=2)], axis=1)
    return out.reshape(CMP_STRIDE * LANES, 4 * CMP_HIDDEN).astype(BF16)


def _sample_compress(pt, cache, q_s, wbd, cw, cov_t):
    n_seq, n_q, _ = q_s.shape
    n_pages = pt.shape[0] // n_seq
    n_seg = n_pages * (PAGE // CMP_STRIDE)
    n_sp = cov_t.shape[0]
    const = lambda b, pt: (0, 0)
    kern = functools.partial(_smp_cmp_kernel, n_pages)
    return pl.pallas_call(
        kern,
        grid_spec=pltpu.PrefetchScalarGridSpec(
            num_scalar_prefetch=1, grid=(n_seq,),
            in_specs=[pl.BlockSpec(memory_space=pl.ANY)]
            + [pl.BlockSpec((None, n_q, 512), lambda b, pt: (b, 0, 0)), pl.BlockSpec((PAGE, PAGE), const)]
            + [pl.BlockSpec((CMP_STRIDE * LANES, 4 * CMP_HIDDEN), const)] * 2
            + _cmp_weight_specs(const) + [pl.BlockSpec(cov_t.shape, const)],
            out_specs=[pl.BlockSpec((None, n_q, 512), lambda b, pt: (b, 0, 0)),
                       pl.BlockSpec((None, N_KVH * n_q, n_sp), lambda b, pt: (b, 0, 0))],
            scratch_shapes=[pltpu.VMEM((2, n_seg, CMP_STRIDE * LANES), BF16)] + _page_scratch(n_pages)),
        out_shape=[jax.ShapeDtypeStruct((n_seq, n_q, 512), F32),
                   jax.ShapeDtypeStruct((n_seq, N_KVH * n_q, n_sp), F32)],
        compiler_params=pltpu.CompilerParams(dimension_semantics=("arbitrary",), vmem_limit_bytes=VMEM_LIMIT),
        name="nsa_sample_cmp",
    )(pt, cache, q_s, _segment_perm(), *wbd, *cw, cov_t)


def _smp_attn_kernel(n_pages, pt_ref, cache_ref, *refs):
    (q_ref, msel_ref, ebig_ref, kvn_ref, win_ref, wkvn_ref, g_ref, ocmp_ref, o_ref, page_buf, page_sem) = refs
    pages = _fetch_pages(pt_ref, cache_ref, page_buf, page_sem, n_pages, 1)
    past = n_pages * PAGE
    q = q_ref[...]
    n_q = q.shape[0]
    n_rows = N_HEADS * n_q
    lo8 = lax.broadcasted_iota(jnp.int32, (n_q, LANES), 1) < HEAD_DIM
    q_all = _stack_queries(q, lo8)
    qi = lax.broadcasted_iota(jnp.int32, (n_rows, 1), 0) & (n_q - 1)

    def pad_rows(x):
        return jnp.concatenate([x, jnp.zeros((LANES - n_q, LANES), F32)], axis=0).astype(BF16)

    def rep(x):
        return jnp.concatenate([x[0:n_q]] * GQA + [x[n_q:2 * n_q]] * GQA, axis=0)

    msel = msel_ref[...]
    n_blk = past // SLC_BLOCK
    lhs = jnp.concatenate([q_all, rep(1.0 - msel[:, 0:LANES]).astype(BF16)], axis=1)
    s_pages = [_nn(lhs, jnp.concatenate([pages[p][0].astype(BF16), ebig_ref[:, p * PAGE:(p + 1) * PAGE]], axis=0))
               for p in range(n_pages)]
    kvn = kvn_ref[...]
    k_tail = pad_rows(kvn[:, 256:384])
    v_tail = pad_rows(kvn[:, 384:512])
    ti = lax.broadcasted_iota(jnp.int32, (n_rows, LANES), 1)
    ok_tail = (rep(jnp.broadcast_to(msel[:, n_blk:n_blk + 1], (2 * n_q, LANES))) > 0.5) & (ti <= qi)
    s_tail = jnp.where(ok_tail, _nt(q_all, k_tail), NEG)
    m_run = s_tail
    for s_p in s_pages:
        m_run = jnp.maximum(m_run, s_p)
    m = jnp.max(m_run, axis=-1, keepdims=True)
    p_tail = jnp.exp2(s_tail - m)
    l_run = p_tail
    pv = _nn(p_tail.astype(BF16), v_tail)
    for p, s_p in enumerate(s_pages):
        p_p = jnp.exp2(s_p - m)
        l_run = l_run + p_p
        pv = pv + _nt(p_p.astype(BF16), pages[p][1].astype(BF16))
    o_slc = pv / jnp.sum(l_run, axis=-1, keepdims=True)

    wk = win_ref[0].astype(BF16)
    wv = win_ref[1].astype(BF16)
    wn = wkvn_ref[...]
    wk_tail = pad_rows(wn[:, 0:128])
    wv_tail = pad_rows(wn[:, 128:256])
    mi = lax.broadcasted_iota(jnp.int32, (n_rows, WINDOW), 1)
    sw_main = jnp.where(mi > qi, _nn(q_all, wk), NEG)
    sw_tail = jnp.where(ti <= qi, _nt(q_all, wk_tail), NEG)
    mw = jnp.maximum(jnp.max(sw_main, axis=-1, keepdims=True), jnp.max(sw_tail, axis=-1, keepdims=True))
    pw_main = jnp.exp2(sw_main - mw)
    pw_tail = jnp.exp2(sw_tail - mw)
    lw = jnp.sum(pw_main, axis=-1, keepdims=True) + jnp.sum(pw_tail, axis=-1, keepdims=True)
    o_win = (_nt(pw_main.astype(BF16), wv) + _nn(pw_tail.astype(BF16), wv_tail)) / lw

    gt = g_ref[...]
    ocmp = ocmp_ref[...]
    half = GQA * n_q
    for c in range(GQA):
        def pick(o):
            return jnp.where(lo8, o[c * n_q:(c + 1) * n_q], o[half + c * n_q:half + (c + 1) * n_q])

        def gate(br):
            return jnp.where(lo8, gt[:, br * 8 + c:br * 8 + c + 1], gt[:, br * 8 + GQA + c:br * 8 + GQA + c + 1])

        o_ref[:, c * LANES:(c + 1) * LANES] = (gate(0) * ocmp[:, c * LANES:(c + 1) * LANES]
                                               + gate(1) * pick(o_slc) + gate(2) * pick(o_win))


def _sample_attend(pt, cache, q_s, msel, expand, kv_new, win_state, wkv_new, gate, o_cmp):
    n_seq, n_q, _ = q_s.shape
    n_pages = pt.shape[0] // n_seq
    past = n_pages * PAGE
    n_sp = msel.shape[-1]
    seq3 = lambda b, pt: (b, 0, 0)
    kern = functools.partial(_smp_attn_kernel, n_pages)
    return pl.pallas_call(
        kern,
        grid_spec=pltpu.PrefetchScalarGridSpec(
            num_scalar_prefetch=1, grid=(n_seq,),
            in_specs=[pl.BlockSpec(memory_space=pl.ANY)]
            + [pl.BlockSpec((None, n_q, 512), seq3), pl.BlockSpec((None, N_KVH * n_q, n_sp), seq3),
               pl.BlockSpec(expand.shape, lambda b, pt: (0, 0)),
               pl.BlockSpec((None, n_q, 512), seq3),
               pl.BlockSpec((None, 2, LANES, WINDOW), lambda b, pt: (b, 0, 0, 0)),
               pl.BlockSpec((None, n_q, 256), seq3), pl.BlockSpec((None, n_q, 128), seq3),
               pl.BlockSpec((None, n_q, 512), seq3)],
            out_specs=pl.BlockSpec((None, n_q, 512), seq3),
            scratch_shapes=_page_scratch(n_pages)),
        out_shape=jax.ShapeDtypeStruct((n_seq, n_q, 512), F32),
        compiler_params=pltpu.CompilerParams(dimension_semantics=("arbitrary",), vmem_limit_bytes=VMEM_LIMIT),
        name="nsa_sample_attn",
    )(pt, cache, q_s, msel, expand, kv_new, win_state, wkv_new, gate, o_cmp)


def _ffn_kernel(x_ref, oa_ref, u_ref, v_ref, ws_ref, bs_ref, woa_ref, wob_ref, g2_ref, w1_ref, w2_ref, gf_ref,
                y_ref, h_scr, hn_scr, ob_scr, acc_scr):
    j = pl.program_id(1)
    tm = x_ref.shape[0]

    @pl.when(j == 0)
    def _():
        lo = lax.broadcasted_iota(jnp.int32, (CHUNK, LANES), 1) < HEAD_DIM
        for ch in range(tm // CHUNK):
            rows = slice(ch * CHUNK, (ch + 1) * CHUNK)
            cols = []
            for c in range(4):
                vc = v_ref[rows, c * LANES:(c + 1) * LANES].astype(BF16)
                cols.append(jnp.where(lo, _nn(ws_ref[2 * c], vc), _nn(ws_ref[2 * c + 1], vc)))
            s = jnp.concatenate(cols, axis=1) + bs_ref[...]
            ob_scr[rows, :] = (u_ref[rows, :] * s).astype(BF16)
        h = x_ref[...] + _nn(oa_ref[...].astype(BF16), woa_ref[...]) + _nn(ob_scr[...], wob_ref[...])
        h_scr[...] = h
        hn_scr[...] = _rms(h, g2_ref[...]).astype(BF16)
        acc_scr[...] = jnp.zeros(acc_scr.shape, F32)

    f = jnp.maximum(_nn(hn_scr[...], w1_ref[...]), 0.0)
    acc_scr[...] += _nn((f * f).astype(BF16), w2_ref[...])

    @pl.when(j == pl.num_programs(1) - 1)
    def _():
        y_ref[...] = _rms(h_scr[...] + acc_scr[...], gf_ref[...])


def _out_ffn(x, o_a, u, v, ws, bs, woa, wob, g2, w1, w2, gf, tm=512, tf=2048):
    n = x.shape[0]
    row = lambda i, j: (i, 0)
    const = lambda i, j: (0, 0)
    return pl.pallas_call(
        _ffn_kernel,
        grid=(n // tm, D_FF // tf),
        in_specs=[pl.BlockSpec((tm, D_MODEL), row), pl.BlockSpec((tm, 512), row), pl.BlockSpec((tm, 512), row),
                  pl.BlockSpec((tm, 512), row), pl.BlockSpec((8, CHUNK, CHUNK), lambda i, j: (0, 0, 0)),
                  pl.BlockSpec((CHUNK, 512), const), pl.BlockSpec((512, D_MODEL), const),
                  pl.BlockSpec((512, D_MODEL), const), pl.BlockSpec((1, D_MODEL), const),
                  pl.BlockSpec((D_MODEL, tf), lambda i, j: (0, j)), pl.BlockSpec((tf, D_MODEL), lambda i, j: (j, 0)),
                  pl.BlockSpec((1, D_MODEL), const)],
        out_specs=pl.BlockSpec((tm, D_MODEL), row),
        out_shape=jax.ShapeDtypeStruct((n, D_MODEL), F32),
        scratch_shapes=[pltpu.VMEM((tm, D_MODEL), F32), pltpu.VMEM((tm, D_MODEL), BF16),
                        pltpu.VMEM((tm, 512), BF16), pltpu.VMEM((tm, D_MODEL), F32)],
        compiler_params=pltpu.CompilerParams(dimension_semantics=("arbitrary", "arbitrary"),
                                             vmem_limit_bytes=VMEM_LIMIT),
        name="nsa_out_ffn",
    )(x, o_a, u, v, ws, bs, woa, wob, g2, w1, w2, gf)


def _head_perm():
    j = np.arange(512)
    return ((j // 128) + 4 * ((j % 128) // 64)) * 64 + (j % 64)


def _cover_t(n_c, n_s, n_c_pad, n_s_pad):
    ci = np.arange(n_c)[:, None] * CMP_STRIDE
    sj = np.arange(n_s)[None, :] * SLC_BLOCK
    cover = np.clip(np.minimum(ci + CMP_BLOCK, sj + SLC_BLOCK) - np.maximum(ci, sj), 0, None) / CMP_BLOCK
    out = np.zeros((n_s_pad, n_c_pad), np.float32)
    out[:n_s, :n_c] = cover.T
    return jnp.asarray(out)


def _cmp_weights(cmp_w1, cmp_b1, cmp_w2, cmp_b2, cmp_pos):
    z = jnp.zeros((CMP_HIDDEN, HEAD_DIM), F32)
    w1 = [jnp.concatenate([cmp_w1[i, :1024], cmp_w1[i, 1024:]], axis=1).astype(BF16) for i in range(2)]
    pos = [cmp_pos[i].reshape(2, 1024) for i in range(2)]
    b1 = [cmp_b1[i].reshape(1, CMP_HIDDEN) for i in range(2)]
    w2a = [jnp.concatenate([cmp_w2[i], z], axis=1).astype(BF16) for i in range(2)]
    w2b = [jnp.concatenate([z, cmp_w2[i]], axis=1).astype(BF16) for i in range(2)]
    b2 = [jnp.concatenate([cmp_b2[i], cmp_b2[i]]).reshape(1, LANES) for i in range(2)]
    return (w1[0], w1[1], pos[0], pos[1], b1[0], b1[1], w2a[0], w2b[0], w2a[1], w2b[1], b2[0], b2[1])


def kernel(x_prompt, x_sample, cache_kv, state_win_kv, page_table, ln1_g, w_in, cmp_w1, cmp_b1, cmp_w2, cmp_b2,
           cmp_pos, ln_v_g, ln_v_b, w_s, b_s, w_out, ln2_g, w_ff1, w_ff2, ln_f_g):
    batch, seq, _ = x_prompt.shape
    n_seq, n_q, _ = x_sample.shape
    n_pages = page_table.shape[1]
    past = n_pages * PAGE
    perm = _head_perm()

    wi = w_in[0]
    w_all = jnp.concatenate(
        [wi[:, 0:512][:, perm], wi[:, 512:1304], jnp.zeros((D_MODEL, C_U - C_GATE - 24), F32), wi[:, 1304:2328]],
        axis=1).astype(BF16)
    g1 = ln1_g[0].reshape(1, D_MODEL)
    lvg = ln_v_g[0].reshape(1, 512)
    lvb = ln_v_b[0].reshape(1, 512)
    cw = _cmp_weights(cmp_w1[0], cmp_b1[0], cmp_w2[0], cmp_b2[0], cmp_pos[0])
    tril = jnp.tril(jnp.ones((CHUNK, CHUNK), F32))
    ws_p = (w_s[0] * tril).astype(BF16)
    bs_p = jnp.repeat(b_s[0].T, HEAD_DIM, axis=1)
    reps = CHUNK // n_q
    ws_s = jnp.einsum("ab,gij->gaibj", jnp.eye(reps, dtype=F32), (w_s[0] * tril)[:, :n_q, :n_q])
    ws_s = ws_s.reshape(8, CHUNK, CHUNK).astype(BF16)
    bs_s = jnp.tile(jnp.repeat(b_s[0].T[:n_q], HEAD_DIM, axis=1), (reps, 1))
    woa = w_out[0][:512][perm].astype(BF16)
    wob = w_out[0][512:].astype(BF16)
    g2 = ln2_g[0].reshape(1, D_MODEL)
    gf = ln_f_g.reshape(1, D_MODEL)
    w1 = w_ff1[0].astype(BF16)
    w2 = w_ff2[0].astype(BF16)

    xp = x_prompt.reshape(batch * seq, D_MODEL)
    q_p, kv_p, _, kvw_p, gate_p, u_p, v_p, kvt_p, wkvt_p = _project(xp, g1, w_all, lvg, lvb, seq)
    kc_p, vc_p = _compress_prompt(kv_p, cw, batch, seq)
    n_seg_p = seq // CMP_STRIDE
    cov_p = _cover_t(n_seg_p - 1, seq // SLC_BLOCK, n_seg_p, seq // SLC_BLOCK)
    oa_p = _nsa_prompt(q_p, gate_p, kc_p, vc_p, kvw_p, cov_p, batch, seq)
    y_p = _out_ffn(xp, oa_p, u_p, v_p, ws_p, bs_p, woa, wob, g2, w1, w2, gf)

    xs = x_sample.reshape(n_seq * n_q, D_MODEL)
    q_s, kv_s, wkv_s, _, gate_s, u_s, v_s, _, _ = _project(xs, g1, w_all, lvg, lvb, n_seq * n_q)
    cache = jnp.transpose(cache_kv[0], (0, 2, 3, 4, 1)).reshape(-1, 4, LANES, PAGE)
    win_state = jnp.transpose(state_win_kv[0], (0, 2, 3, 4, 1)).reshape(n_seq, 2, LANES, -1)
    pt = page_table.reshape(-1)
    n_seg_s = past // CMP_STRIDE
    n_s = past // SLC_BLOCK + 1
    n_sp = -(-n_s // LANES) * LANES
    cov_s = _cover_t(n_seg_s, n_s, n_seg_s, n_sp)
    q_s3 = q_s.reshape(n_seq, n_q, 512)
    wbd = (_blockdiag_w1(cmp_w1[0, 0]), _blockdiag_w1(cmp_w1[0, 1]))
    ocmp_s, msel = _sample_compress(pt, cache, q_s3, wbd, cw, cov_s)
    ebig_t = jnp.transpose(_block_bias(past))
    oa_s = _sample_attend(pt, cache, q_s3, msel, ebig_t, kv_s.reshape(n_seq, n_q, 512), win_state,
                          wkv_s.reshape(n_seq, n_q, 256), gate_s.reshape(n_seq, n_q, 128), ocmp_s)
    y_s = _out_ffn(xs, oa_s.reshape(n_seq * n_q, 512), u_s, v_s, ws_s, bs_s, woa, wob, g2, w1, w2, gf)

    keep = min(WINDOW, seq)
    new_kv_p = jnp.transpose(kvt_p.reshape(batch, 4, N_KVH, HEAD_DIM, seq), (0, 4, 1, 2, 3))
    new_win_p = jnp.transpose(wkvt_p[:, :, seq - keep:].reshape(batch, 2, N_KVH, HEAD_DIM, keep), (0, 4, 1, 2, 3))
    new_win_s = jnp.concatenate([state_win_kv[0], wkv_s.reshape(n_seq, n_q, 2, N_KVH, HEAD_DIM)], axis=1)[:, n_q:]
    return (y_p.reshape(batch, seq, D_MODEL),
            y_s.reshape(n_seq, n_q, D_MODEL),
            new_kv_p[None],
            kv_s.reshape(1, n_seq, n_q, 4, N_KVH, HEAD_DIM),
            new_win_p[None],
            new_win_s[None],
            v_s.reshape(1, n_seq, n_q, 512))
```

```python
import functools

import numpy as np
import jax
import jax.numpy as jnp
from jax import lax
from jax.experimental import pallas as pl
from jax.experimental.pallas import tpu as pltpu

F32 = jnp.float32
BF16 = jnp.bfloat16

D_MODEL = 1024
HEAD_DIM = 64
N_HEADS = 8
N_KVH = 2
GQA = 4
CMP_BLOCK = 32
CMP_STRIDE = 16
CMP_HIDDEN = 128
SLC_BLOCK = 64
N_SELECT = 16
WINDOW = 512
CHUNK = 128
D_FF = 4096
PAGE = 128
EPS = 1e-6
NEG = -1e30
LOG2E = 1.4426950408889634
FORCE_SCORE = 1e9
LANES = 128
VMEM_LIMIT = 56 * 1024 * 1024
SLC_TILE = 512

C_Q, C_KV, C_WIN, C_GATE, C_U, C_V, C_END = 0, 512, 1024, 1280, 1408, 1920, 2432


def _nn(a, b):
    return jnp.dot(a, b, preferred_element_type=F32)


def _nt(a, b):
    return lax.dot_general(a, b, (((1,), (1,)), ((), ())), preferred_element_type=F32)


def _gelu(x):
    return 0.5 * x * (1.0 + jnp.tanh(0.7978845608028654 * (x + 0.044715 * (x * x * x))))


def _rms(x, g):
    return x * lax.rsqrt(jnp.mean(x * x, axis=-1, keepdims=True) + EPS) * g


def _softmax_rows(s, mask):
    s = jnp.where(mask, s, NEG)
    m = jnp.max(s, axis=-1, keepdims=True)
    p = jnp.where(mask, jnp.exp2(s - m), 0.0)
    l = jnp.sum(p, axis=-1, keepdims=True)
    return p / jnp.maximum(l, 1e-30)


def _proj_kernel(x_ref, g1_ref, w_ref, lvg_ref, lvb_ref,
                 q_ref, kv_ref, wkv_ref, kvw_ref, gate_ref, u_ref, v_ref, kvt_ref, wkvt_ref):
    x = x_ref[...]
    h = _rms(x, g1_ref[...]).astype(BF16)

    def z(a, b):
        return _nn(h, w_ref[:, a:b])

    q_ref[...] = z(C_Q, C_KV) * (HEAD_DIM ** -0.5 * LOG2E)
    kv = z(C_KV, C_WIN)
    kv_ref[...] = kv
    kvt_ref[...] = kv.T
    wkv = z(C_WIN, C_GATE)
    wkv_ref[...] = wkv
    wkvt_ref[...] = wkv.T
    kvw_ref[:, 0:256] = kv[:, 256:512].astype(BF16)
    kvw_ref[:, 256:512] = wkv.astype(BF16)
    gate_ref[...] = jax.nn.sigmoid(z(C_GATE, C_U))
    u_ref[...] = _gelu(z(C_U, C_V))
    zv = _gelu(z(C_V, C_END))
    mu = jnp.mean(zv, axis=-1, keepdims=True)
    var = jnp.mean(jnp.square(zv - mu), axis=-1, keepdims=True)
    v_ref[...] = (zv - mu) * lax.rsqrt(var + EPS) * lvg_ref[...] + lvb_ref[...]


def _project(x, g1, w_all, lvg, lvb, seq, tm=512):
    n = x.shape[0]
    per = seq // tm
    row = lambda i: (i, 0)
    const = lambda i: (0, 0)
    colmajor = lambda i: (i // per, 0, i % per)
    widths = (512, 512, 256, 512, 128, 512, 512)
    dtypes = (F32, F32, F32, BF16, F32, F32, F32)
    return pl.pallas_call(
        _proj_kernel,
        grid=(n // tm,),
        in_specs=[pl.BlockSpec((tm, D_MODEL), row), pl.BlockSpec((1, D_MODEL), const),
                  pl.BlockSpec((D_MODEL, C_END), const), pl.BlockSpec((1, 512), const),
                  pl.BlockSpec((1, 512), const)],
        out_specs=[pl.BlockSpec((tm, w), row) for w in widths]
        + [pl.BlockSpec((None, 512, tm), colmajor), pl.BlockSpec((None, 256, tm), colmajor)],
        out_shape=[jax.ShapeDtypeStruct((n, w), d) for w, d in zip(widths, dtypes)]
        + [jax.ShapeDtypeStruct((n // seq, 512, seq), F32), jax.ShapeDtypeStruct((n // seq, 256, seq), F32)],
        compiler_params=pltpu.CompilerParams(dimension_semantics=("arbitrary",), vmem_limit_bytes=VMEM_LIMIT),
        name="nsa_proj",
    )(x, g1, w_all, lvg, lvb)


def _compress_tail(h0, h1, w1_ref, pos_ref, b1_ref, w2a_ref, w2b_ref, b2_ref, n_seg):
    w1 = w1_ref[...]
    pos = pos_ref[...]
    p0 = jnp.broadcast_to(pos[0:1], (8, 1024)).astype(BF16)
    p1 = jnp.broadcast_to(pos[1:2], (8, 1024)).astype(BF16)
    c1 = b1_ref[...] + _nn(p0, w1[:, 0:128])[0:1] + _nn(p1, w1[:, 128:256])[0:1]

    def hidden(h):
        nxt = pltpu.roll(h[:, 128:256], n_seg - 1, 0)
        return _gelu(h[:, 0:128] + nxt + c1).astype(BF16)

    return _nn(hidden(h0), w2a_ref[...]) + _nn(hidden(h1), w2b_ref[...]) + b2_ref[...]


def _compress_core(lhs0, lhs1, w1_ref, pos_ref, b1_ref, w2a_ref, w2b_ref, b2_ref, n_seg):
    w1 = w1_ref[...]
    return _compress_tail(_nn(lhs0, w1), _nn(lhs1, w1), w1_ref, pos_ref, b1_ref, w2a_ref, w2b_ref, b2_ref, n_seg)


def _cmp_prompt_kernel(k_ref, v_ref, w1k_ref, w1v_ref, posk_ref, posv_ref, b1k_ref, b1v_ref,
                       w2ak_ref, w2bk_ref, w2av_ref, w2bv_ref, b2k_ref, b2v_ref,
                       kc_ref, vc_ref, u_scr):
    n_seg = k_ref.shape[0] // CMP_STRIDE
    lo = lax.broadcasted_iota(jnp.int32, (n_seg, LANES), 1) < HEAD_DIM

    def run(x_ref, w1, pos, b1, w2a, w2b, b2, out_ref):
        for pr in range(CMP_STRIDE // 2):
            a = x_ref[pl.ds(2 * pr, n_seg, stride=CMP_STRIDE), :]
            b = x_ref[pl.ds(2 * pr + 1, n_seg, stride=CMP_STRIDE), :]
            u_scr[0, :, pr * LANES:(pr + 1) * LANES] = jnp.where(lo, a, pltpu.roll(b, HEAD_DIM, 1)).astype(BF16)
            u_scr[1, :, pr * LANES:(pr + 1) * LANES] = jnp.where(lo, pltpu.roll(a, HEAD_DIM, 1), b).astype(BF16)
        out_ref[0] = _compress_core(u_scr[0], u_scr[1], w1, pos, b1, w2a, w2b, b2, n_seg)

    run(k_ref, w1k_ref, posk_ref, b1k_ref, w2ak_ref, w2bk_ref, b2k_ref, kc_ref)
    run(v_ref, w1v_ref, posv_ref, b1v_ref, w2av_ref, w2bv_ref, b2v_ref, vc_ref)


def _cmp_weight_specs(const):
    return ([pl.BlockSpec((1024, 256), const)] * 2 + [pl.BlockSpec((2, 1024), const)] * 2
            + [pl.BlockSpec((1, 128), const)] * 2 + [pl.BlockSpec((128, 128), const)] * 4
            + [pl.BlockSpec((1, 128), const)] * 2)


def _compress_prompt(kv, cw, batch, seq):
    n_seg = seq // CMP_STRIDE
    const = lambda b: (0, 0)
    return pl.pallas_call(
        _cmp_prompt_kernel,
        grid=(batch,),
        in_specs=[pl.BlockSpec((seq, LANES), lambda b: (b, 0)), pl.BlockSpec((seq, LANES), lambda b: (b, 1))]
        + _cmp_weight_specs(const),
        out_specs=[pl.BlockSpec((1, n_seg, LANES), lambda b: (b, 0, 0))] * 2,
        out_shape=[jax.ShapeDtypeStruct((batch, n_seg, LANES), F32)] * 2,
        scratch_shapes=[pltpu.VMEM((2, n_seg, 1024), BF16)],
        compiler_params=pltpu.CompilerParams(dimension_semantics=("arbitrary",), vmem_limit_bytes=VMEM_LIMIT),
        name="nsa_cmp_prompt",
    )(kv, kv, *cw)


def _rank_row(score, blk, j, rj):
    lower = jnp.where(blk < j, 1.0, 0.0)
    beat = jnp.where(score > rj, 1.0, jnp.where(score == rj, lower, 0.0))
    return jnp.sum(beat, axis=0, keepdims=True)


def _scores_t(imp_t, blk, t):
    cur = t >> 6
    forced = (blk == 0) | (blk == cur) | (blk == cur - 1)
    visible = blk * SLC_BLOCK <= t
    return jnp.where(forced, FORCE_SCORE, jnp.where(visible, imp_t, -jnp.inf)), visible


def _importance_t(cov_t, psum):
    hi = psum.astype(BF16)
    lo_ = (psum - hi.astype(F32)).astype(BF16)
    c = cov_t.astype(BF16)
    return _nt(c, hi) + _nt(c, lo_)


def _nsa_prompt_kernel(q_ref, g_ref, kc_ref, vc_ref, kvw_ref, covt_ref, ebig_ref, gexp_ref, o_ref,
                       mrun_scr, acc_scr, s_scr):
    qb = pl.program_id(1)
    t0 = qb * 128
    n_rows = N_HEADS * 128
    half = GQA * 128
    lo = lax.broadcasted_iota(jnp.int32, (128, LANES), 1) < HEAD_DIM
    t_all = t0 + (lax.broadcasted_iota(jnp.int32, (n_rows, 1), 0) & 127)
    q = q_ref[...]
    qm = jnp.concatenate(
        [jnp.where(lo if k == 0 else jnp.logical_not(lo), q[:, g * LANES:(g + 1) * LANES], 0.0)
         for k in range(N_KVH) for g in range(GQA)], axis=0).astype(BF16)

    def head(x, k, g):
        return x[(k * GQA + g) * 128:(k * GQA + g + 1) * 128]

    kc = kc_ref[0].astype(BF16)
    vc = vc_ref[0].astype(BF16)
    n_c = kc.shape[0]
    t_q = t0 + lax.broadcasted_iota(jnp.int32, (128, 1), 0)

    def all_heads(bias):
        return jnp.concatenate([bias] * N_HEADS, axis=0)

    s0 = pl.multiple_of(jnp.maximum(t0 - WINDOW, 0), 128)
    wk = kvw_ref[pl.ds(s0, WINDOW + 128), 256:384]
    wv = kvw_ref[pl.ds(s0, WINDOW + 128), 384:512]
    low = lax.broadcasted_iota(jnp.int32, (WINDOW + 128, LANES), 1) < HEAD_DIM
    kpos = s0 + lax.broadcasted_iota(jnp.int32, (128, WINDOW + 128), 1)
    s_w = _nt(qm, wk) + all_heads(jnp.where((kpos <= t_q) & (kpos > t_q - WINDOW), 0.0, NEG))
    p_w = jnp.exp2(s_w - jnp.max(s_w, axis=-1, keepdims=True)).astype(BF16)
    win = [_nn(p_w[0:half], jnp.where(low, wv, 1.0).astype(BF16)),
           _nn(p_w[half:n_rows], jnp.where(low, 1.0, wv).astype(BF16))]

    n_iota = lax.broadcasted_iota(jnp.int32, (128, n_c), 1)
    s_c = _nt(qm, kc) + all_heads(jnp.where(n_iota * CMP_STRIDE + (CMP_BLOCK - 1) <= t_q, 0.0, NEG))
    e_c = jnp.exp2(s_c - jnp.max(s_c, axis=-1, keepdims=True))
    row_ok = jnp.where(t_all >= CMP_BLOCK - 1, 1.0, 0.0)
    p_c = e_c * (row_ok / jnp.sum(e_c, axis=-1, keepdims=True))
    o_cmp = _nn(p_c.astype(BF16), vc)
    psum = [sum(head(p_c, k, g) for g in range(GQA)) for k in range(N_KVH)]

    n_s = covt_ref.shape[0]
    blk = lax.broadcasted_iota(jnp.int32, (n_s, 128), 0)
    t_lane = t0 + lax.broadcasted_iota(jnp.int32, (n_s, 128), 1)
    nsel = []
    for k in range(N_KVH):
        score, visible = _scores_t(_importance_t(covt_ref[...], psum[k]), blk, t_lane)
        cnt = jnp.concatenate([_rank_row(score, blk, j, score[j:j + 1, :]) for j in range(n_s)], axis=0)
        m_t = jnp.where((cnt < N_SELECT) & visible, 1.0, 0.0)
        m_t = jnp.concatenate([m_t, jnp.zeros((128 - n_s, 128), F32)], axis=0)
        nsel.append((1.0 - m_t.T).astype(BF16))

    lhs = jnp.concatenate([qm, jnp.concatenate([nsel[0]] * GQA + [nsel[1]] * GQA, axis=0)], axis=1)
    mrun_scr[...] = jnp.full(mrun_scr.shape, NEG, F32)
    acc_scr[...] = jnp.zeros(acc_scr.shape, F32)
    tk = SLC_TILE
    n_full = qb // (tk // 128)

    def max_tile(k0, causal):
        rhs = jnp.concatenate([kvw_ref[pl.ds(k0, tk), 0:128], ebig_ref[pl.ds(k0, tk), :]], axis=1)
        s = _nt(lhs, rhs)
        if causal:
            s = s + all_heads(jnp.where(k0 + lax.broadcasted_iota(jnp.int32, (128, tk), 1) <= t_q, 0.0, NEG))
        s_scr[:, pl.ds(k0, tk)] = s
        m = s[:, 0:LANES]
        for j in range(1, tk // LANES):
            m = jnp.maximum(m, s[:, j * LANES:(j + 1) * LANES])
        mrun_scr[...] = jnp.maximum(mrun_scr[...], m)

    def loop1(kt, carry):
        max_tile(pl.multiple_of(kt * tk, tk), False)
        return carry

    max_tile(pl.multiple_of(n_full * tk, tk), True)
    lax.fori_loop(0, n_full, loop1, 0)
    mb =jnp.broadcast_to(jnp.max(mrun_scr[...], axis=-1, keepdims=True), (n_rows, LANES))

    def loop2(kt, carry):
        k0 = pl.multiple_of(kt * tk, tk)
        v_t = kvw_ref[pl.ds(k0, tk), 128:256]
        lov = lax.broadcasted_iota(jnp.int32, (tk, LANES), 1) < HEAD_DIM
        p = jnp.exp2(s_scr[:, pl.ds(k0, tk)] - jnp.concatenate([mb] * (tk // LANES), axis=1)).astype(BF16)
        acc_scr[0] += _nn(p[0:half], jnp.where(lov, v_t, 1.0).astype(BF16))
        acc_scr[1] += _nn(p[half:n_rows], jnp.where(lov, 1.0, v_t).astype(BF16))
        return carry

    lax.fori_loop(0, n_full + 1, loop2, 0)

    def normalised(n0, n1):
        return jnp.where(lo, n0, n1) / pltpu.roll(jnp.where(lo, n1, n0), HEAD_DIM, 1)

    gt = g_ref[...]
    g_hi = gt.astype(BF16)
    g_lo = (gt - g_hi.astype(F32)).astype(BF16)
    gexp = _nn(g_hi, gexp_ref[...]) + _nn(g_lo, gexp_ref[...])
    for c in range(GQA):
        cols = slice(c * LANES, (c + 1) * LANES)
        rows = slice(c * 128, (c + 1) * 128)
        o_c = jnp.where(lo, head(o_cmp, 0, c), head(o_cmp, 1, c))
        o_s = normalised(acc_scr[0, rows, :], acc_scr[1, rows, :])
        o_w = normalised(win[0][rows], win[1][rows])
        o_ref[:, cols] = (gexp[:, cols] * o_c + gexp[:, 512 + c * LANES:512 + (c + 1) * LANES] * o_s
                          + gexp[:, 1024 + c * LANES:1024 + (c + 1) * LANES] * o_w).astype(BF16)


def _block_bias(seq):
    m = (np.arange(seq)[:, None] // SLC_BLOCK == np.arange(LANES)[None, :]).astype(np.float32) * NEG
    return jnp.asarray(m).astype(BF16)


def _gate_expand():
    m = np.zeros((LANES, 3 * 512), np.float32)
    col = np.arange(512)
    head = col // LANES + GQA * ((col % LANES) // HEAD_DIM)
    for br in range(3):
        m[br * N_HEADS + head, br * 512 + col] = 1.0
    return jnp.asarray(m).astype(BF16)


def _nsa_prompt(q, gate, kc, vc, kvw, cov_t, batch, seq):
    n_qb = seq // 128
    n_seg = kc.shape[1]
    row = lambda b, i: (b * n_qb + i, 0)
    const = lambda b, i: (0, 0)
    return pl.pallas_call(
        _nsa_prompt_kernel,
        grid=(batch, n_qb),
        in_specs=[pl.BlockSpec((128, 512), row), pl.BlockSpec((128, 128), row),
                  pl.BlockSpec((1, n_seg, LANES), lambda b, i: (b, 0, 0)),
                  pl.BlockSpec((1, n_seg, LANES), lambda b, i: (b, 0, 0)),
                  pl.BlockSpec((seq, 512), lambda b, i: (b, 0)),
                  pl.BlockSpec(cov_t.shape, const), pl.BlockSpec((seq, LANES), const),
                  pl.BlockSpec((LANES, 3 * 512), const)],
        out_specs=pl.BlockSpec((128, 512), row),
        out_shape=jax.ShapeDtypeStruct((batch * seq, 512), BF16),
        scratch_shapes=[pltpu.VMEM((N_HEADS * 128, LANES), F32), pltpu.VMEM((N_KVH, GQA * 128, LANES), F32),
                        pltpu.VMEM((N_HEADS * 128, seq), F32)],
        compiler_params=pltpu.CompilerParams(dimension_semantics=("arbitrary", "arbitrary"),
                                             vmem_limit_bytes=VMEM_LIMIT),
        name="nsa_prompt_attn",
    )(q, gate, kc, vc, kvw, cov_t, _block_bias(seq), _gate_expand())


def _stack_queries(q, lo):
    rows = [jnp.where(lo if k == 0 else jnp.logical_not(lo), q[:, g * LANES:(g + 1) * LANES], 0.0)
            for k in range(N_KVH) for g in range(GQA)]
    return jnp.concatenate(rows, axis=0).astype(BF16)


def _fetch_pages(pt_ref, cache_ref, buf, sem, n_pages, half, n_seq):
    b = pl.program_id(0)
    slot = lax.rem(b, 2)

    def copies(step, dst_slot):
        seq = jnp.minimum(step, n_seq - 1)
        return [pltpu.make_async_copy(cache_ref.at[pt_ref[seq * n_pages + p], pl.ds(2 * half, 2)],
                                      buf.at[dst_slot, p], sem.at[dst_slot]) for p in range(n_pages)]

    @pl.when(b == 0)
    def _():
        for c in copies(0, 0):
            c.start()

    @pl.when(b + 1 < pl.num_programs(0))
    def _():
        for c in copies(b + 1, 1 - slot):
            c.start()

    for c in copies(b, slot):
        c.wait()
    return [buf.at[slot, p] for p in range(n_pages)]


def _smp_cmp_kernel(n_pages, n_seq, pt_ref, cache_ref, *refs):
    (q_ref, perm_ref, wbdk_ref, wbdv_ref, w1k_ref, w1v_ref, posk_ref, posv_ref, b1k_ref, b1v_ref, w2ak_ref,
     w2bk_ref, w2av_ref, w2bv_ref, b2k_ref, b2v_ref, covt_ref, ocmp_ref, msel_ref,
     u_scr, kv_scr, page_buf, page_sem) = refs

    @pl.when(pl.program_id(0) == 0)
    def _():
        kv_scr[...] = jnp.zeros(kv_scr.shape, F32)

    pages = _fetch_pages(pt_ref, cache_ref, page_buf, page_sem, n_pages, 0, n_seq)
    seg_pp = PAGE // CMP_STRIDE
    n_seg = n_pages * seg_pp
    past = n_pages * PAGE

    perm = perm_ref[...]
    n_chunk = 4
    pairs_per = (n_pages // 2) // n_chunk

    def gather(ch):
        for pp in range(ch * pairs_per, (ch + 1) * pairs_per):
            ra = _nt(perm, jnp.concatenate([pages[2 * pp][0], pages[2 * pp][1]], axis=0).astype(BF16))
            rb = _nt(perm, jnp.concatenate([pages[2 * pp + 1][0], pages[2 * pp + 1][1]], axis=0).astype(BF16))
            rows = slice(pp * 2 * seg_pp, (pp + 1) * 2 * seg_pp)
            for s in range(CMP_STRIDE):
                piece = jnp.concatenate([ra[s * seg_pp:(s + 1) * seg_pp], rb[s * seg_pp:(s + 1) * seg_pp]], axis=0)
                piece = piece.astype(BF16)
                u_scr[0, rows, s * LANES:(s + 1) * LANES] = piece[:, 0:LANES]
                u_scr[1, rows, s * LANES:(s + 1) * LANES] = piece[:, LANES:2 * LANES]

    kc = kv_scr[0]
    vc = kv_scr[1]
    gather(0)

    q = q_ref[...]
    n_q = q.shape[0]
    lo8 = lax.broadcasted_iota(jnp.int32, (n_q, LANES), 1) < HEAD_DIM
    q_all = _stack_queries(q, lo8)
    t_row = past + (lax.broadcasted_iota(jnp.int32, (N_HEADS * n_q, n_seg), 0) & (n_q - 1))
    n_iota = lax.broadcasted_iota(jnp.int32, (N_HEADS * n_q, n_seg), 1)
    p = _softmax_rows(_nt(q_all, kc.astype(BF16)), n_iota * CMP_STRIDE + (CMP_BLOCK - 1) <= t_row)
    o = _nn(p.astype(BF16), vc.astype(BF16))
    half = GQA * n_q
    for c in range(GQA):
        ocmp_ref[:, c * LANES:(c + 1) * LANES] = jnp.where(lo8, o[c * n_q:(c + 1) * n_q],
                                                         o[half + c * n_q:half + (c + 1) * n_q])
    psum = [sum(p[k * half + g * n_q:k * half + (g + 1) * n_q] for g in range(GQA)) for k in range(N_KVH)]
    psum = jnp.concatenate(psum + [jnp.zeros((LANES - N_KVH * n_q, n_seg), F32)], axis=0)
    gather(1)

    n_sp = covt_ref.shape[0]
    n_s = past // SLC_BLOCK + 1
    rows_used = -(-n_s // 8) * 8
    n_r = N_KVH * n_q
    blk_t = lax.broadcasted_iota(jnp.int32, (n_sp, LANES), 0)
    t_lane = past + (lax.broadcasted_iota(jnp.int32, (n_sp, LANES), 1) & (n_q - 1))
    score_t, visible_t = _scores_t(_importance_t(covt_ref[...], psum), blk_t, t_lane)
    score = score_t.T[0:n_r]
    visible = jnp.where(visible_t, 1.0, 0.0).T[0:n_r] > 0.5
    gather(2)
    gather(3)

    ii = lax.broadcasted_iota(jnp.int32, (rows_used, n_sp), 0)
    jj = lax.broadcasted_iota(jnp.int32, (rows_used, n_sp), 1)
    lower = jnp.where(ii < jj, 1.0, 0.0)

    def rank_rows(r0, r1):
        out = []
        for r in range(r0, r1):
            col = score_t[0:rows_used, r:r + 1]
            row = score[r:r + 1, :]
            beat = jnp.where(col > row, 1.0, jnp.where(col == row, lower, 0.0))
            out.append(jnp.sum(beat, axis=0, keepdims=True))
        return out

    hk = _nn(u_scr[0], wbdk_ref[...])
    cnt = rank_rows(0, n_r // 2)
    hv = _nn(u_scr[1], wbdv_ref[...])
    cnt = jnp.concatenate(cnt + rank_rows(n_r // 2, n_r), axis=0)
    msel_ref[...] = jnp.where((cnt < N_SELECT) & visible, 1.0, 0.0)
    kv_scr[0] = _compress_tail(hk[:, 0:256], hk[:, 256:512], w1k_ref, posk_ref, b1k_ref, w2ak_ref, w2bk_ref, b2k_ref,
                               n_seg)
    kv_scr[1] = _compress_tail(hv[:, 0:256], hv[:, 256:512], w1v_ref, posv_ref, b1v_ref, w2av_ref, w2bv_ref, b2v_ref,
                               n_seg)


def _page_scratch(n_pages):
    return [pltpu.VMEM((2, n_pages, 2, LANES, PAGE), F32), pltpu.SemaphoreType.DMA((2,))]


def _segment_perm():
    r = np.arange(PAGE)
    m = np.zeros((PAGE, PAGE), np.float32)
    m[r, (r % (PAGE // CMP_STRIDE)) * CMP_STRIDE + r // (PAGE // CMP_STRIDE)] = 1.0
    return jnp.asarray(m).astype(BF16)


def _blockdiag_w1(w1):
    w = jnp.transpose(w1.reshape(2, CMP_STRIDE, HEAD_DIM, CMP_HIDDEN), (1, 2, 0, 3))
    w = w.reshape(CMP_STRIDE, HEAD_DIM, 2 * CMP_HIDDEN)
    z = jnp.zeros_like(w)
    out = jnp.concatenate([jnp.concatenate([w, z], axis=2), jnp.concatenate([z, w], axis=2)], axis=1)
    return out.reshape(CMP_STRIDE * LANES, 4 * CMP_HIDDEN).astype(BF16)


def _sample_compress(pt, cache, q_s, wbd, cw, cov_t):
    n_seq, n_q, _ = q_s.shape
    n_pages = pt.shape[0] // n_seq
    n_seg = n_pages * (PAGE // CMP_STRIDE)
    n_sp = cov_t.shape[0]
    const = lambda b, pt: (0, 0)
    kern = functools.partial(_smp_cmp_kernel, n_pages, n_seq)
    prev = lambda b, pt: (jnp.maximum(b - 1, 0), 0, 0)
    return pl.pallas_call(
        kern,
        grid_spec=pltpu.PrefetchScalarGridSpec(
            num_scalar_prefetch=1, grid=(n_seq + 1,),
            in_specs=[pl.BlockSpec(memory_space=pl.ANY)]
            + [pl.BlockSpec((None, n_q, 512), prev), pl.BlockSpec((PAGE, PAGE), const)]
            + [pl.BlockSpec((CMP_STRIDE * LANES, 4 * CMP_HIDDEN), const)] * 2
            + _cmp_weight_specs(const) + [pl.BlockSpec(cov_t.shape, const)],
            out_specs=[pl.BlockSpec((None, n_q, 512), prev), pl.BlockSpec((None, N_KVH * n_q, n_sp), prev)],
            scratch_shapes=[pltpu.VMEM((2, n_seg, CMP_STRIDE * LANES), BF16),
                            pltpu.VMEM((2, n_seg, LANES), F32)] + _page_scratch(n_pages)),
        out_shape=[jax.ShapeDtypeStruct((n_seq, n_q, 512), F32),
                   jax.ShapeDtypeStruct((n_seq, N_KVH * n_q, n_sp), F32)],
        compiler_params=pltpu.CompilerParams(dimension_semantics=("arbitrary",), vmem_limit_bytes=VMEM_LIMIT),
        name="nsa_sample_cmp",
    )(pt, cache, q_s, _segment_perm(), *wbd, *cw, cov_t)


def _smp_attn_kernel(n_pages, n_seq, pt_ref, cache_ref, *refs):
    (q_ref, msel_ref, ebig_ref, kvn_ref, win_ref, wkvn_ref, g_ref, ocmp_ref, o_ref, page_buf, page_sem) = refs
    pages = _fetch_pages(pt_ref, cache_ref, page_buf, page_sem, n_pages, 1, n_seq)
    past = n_pages * PAGE
    q = q_ref[...]
    n_q = q.shape[0]
    n_rows = N_HEADS * n_q
    lo8 = lax.broadcasted_iota(jnp.int32, (n_q, LANES), 1) < HEAD_DIM
    q_all = _stack_queries(q, lo8)
    qi = lax.broadcasted_iota(jnp.int32, (n_rows, 1), 0) & (n_q - 1)

    def pad_rows(x):
        return jnp.concatenate([x, jnp.zeros((LANES - n_q, LANES), F32)], axis=0).astype(BF16)

    def rep(x):
        return jnp.concatenate([x[0:n_q]] * GQA + [x[n_q:2 * n_q]] * GQA, axis=0)

    msel = msel_ref[...]
    n_blk = past // SLC_BLOCK
    lhs = jnp.concatenate([q_all, rep(1.0 - msel[:, 0:LANES]).astype(BF16)], axis=1)
    s_pages = [_nn(lhs, jnp.concatenate([pages[p][0].astype(BF16), ebig_ref[:, p * PAGE:(p + 1) * PAGE]], axis=0))
               for p in range(n_pages)]
    kvn = kvn_ref[...]
    k_tail = pad_rows(kvn[:, 256:384])
    v_tail = pad_rows(kvn[:, 384:512])
    ti = lax.broadcasted_iota(jnp.int32, (n_rows, LANES), 1)
    ok_tail = (rep(jnp.broadcast_to(msel[:, n_blk:n_blk + 1], (2 * n_q, LANES))) > 0.5) & (ti <= qi)
    s_tail = jnp.where(ok_tail, _nt(q_all, k_tail), NEG)
    m_run = s_tail
    for s_p in s_pages:
        m_run = jnp.maximum(m_run, s_p)
    m = jnp.max(m_run, axis=-1, keepdims=True)
    p_tail = jnp.exp2(s_tail - m)
    l_run = p_tail
    pv = _nn(p_tail.astype(BF16), v_tail)
    for p, s_p in enumerate(s_pages):
        p_p = jnp.exp2(s_p - m)
        l_run = l_run + p_p
        pv = pv + _nt(p_p.astype(BF16), pages[p][1].astype(BF16))
    o_slc = pv / jnp.sum(l_run, axis=-1, keepdims=True)

    wk = win_ref[0].astype(BF16)
    wv = win_ref[1].astype(BF16)
    wn = wkvn_ref[...]
    wk_tail = pad_rows(wn[:, 0:128])
    wv_tail = pad_rows(wn[:, 128:256])
    mi = lax.broadcasted_iota(jnp.int32, (n_rows, WINDOW), 1)
    sw_main = jnp.where(mi > qi, _nn(q_all, wk), NEG)
    sw_tail = jnp.where(ti <= qi, _nt(q_all, wk_tail), NEG)
    mw = jnp.maximum(jnp.max(sw_main, axis=-1, keepdims=True), jnp.max(sw_tail, axis=-1, keepdims=True))
    pw_main = jnp.exp2(sw_main - mw)
    pw_tail = jnp.exp2(sw_tail - mw)
    lw = jnp.sum(pw_main, axis=-1, keepdims=True) + jnp.sum(pw_tail, axis=-1, keepdims=True)
    o_win = (_nt(pw_main.astype(BF16), wv) + _nn(pw_tail.astype(BF16), wv_tail)) / lw

    gt = g_ref[...]
    ocmp = ocmp_ref[...]
    half = GQA * n_q
    for c in range(GQA):
        def pick(o):
            return jnp.where(lo8, o[c * n_q:(c + 1) * n_q], o[half + c * n_q:half + (c + 1) * n_q])

        def gate(br):
            return jnp.where(lo8, gt[:, br * 8 + c:br * 8 + c + 1], gt[:, br * 8 + GQA + c:br * 8 + GQA + c + 1])

        o_ref[:, c * LANES:(c + 1) * LANES] = (gate(0) * ocmp[:, c * LANES:(c + 1) * LANES]
                                               + gate(1) * pick(o_slc) + gate(2) * pick(o_win))


def _sample_attend(pt, cache, q_s, msel, expand, kv_new, win_state, wkv_new, gate, o_cmp):
    n_seq, n_q, _ = q_s.shape
    n_pages = pt.shape[0] // n_seq
    past = n_pages * PAGE
    n_sp = msel.shape[-1]
    seq3 = lambda b, pt: (b, 0, 0)
    kern = functools.partial(_smp_attn_kernel, n_pages, n_seq)
    return pl.pallas_call(
        kern,
        grid_spec=pltpu.PrefetchScalarGridSpec(
            num_scalar_prefetch=1, grid=(n_seq,),
            in_specs=[pl.BlockSpec(memory_space=pl.ANY)]
            + [pl.BlockSpec((None, n_q, 512), seq3), pl.BlockSpec((None, N_KVH * n_q, n_sp), seq3),
               pl.BlockSpec(expand.shape, lambda b, pt: (0, 0)),
               pl.BlockSpec((None, n_q, 512), seq3),
               pl.BlockSpec((None, 2, LANES, WINDOW), lambda b, pt: (b, 0, 0, 0)),
               pl.BlockSpec((None, n_q, 256), seq3), pl.BlockSpec((None, n_q, 128), seq3),
               pl.BlockSpec((None, n_q, 512), seq3)],
            out_specs=pl.BlockSpec((None, n_q, 512), seq3),
            scratch_shapes=_page_scratch(n_pages)),
        out_shape=jax.ShapeDtypeStruct((n_seq, n_q, 512), F32),
        compiler_params=pltpu.CompilerParams(dimension_semantics=("arbitrary",), vmem_limit_bytes=VMEM_LIMIT),
        name="nsa_sample_attn",
    )(pt, cache, q_s, msel, expand, kv_new, win_state, wkv_new, gate, o_cmp)


def _ffn_kernel(x_ref, oa_ref, u_ref, v_ref, ws_ref, bs_ref, woa_ref, wob_ref, g2_ref, w1_ref, w2_ref, gf_ref,
                y_ref, h_scr, hn_scr, ob_scr, acc_scr):
    j = pl.program_id(1)
    tm = x_ref.shape[0]

    @pl.when(j == 0)
    def _():
        lo = lax.broadcasted_iota(jnp.int32, (CHUNK, LANES), 1) < HEAD_DIM
        for ch in range(tm // CHUNK):
            rows = slice(ch * CHUNK, (ch + 1) * CHUNK)
            cols = []
            for c in range(4):
                vc = v_ref[rows, c * LANES:(c + 1) * LANES].astype(BF16)
                cols.append(jnp.where(lo, _nn(ws_ref[2 * c], vc), _nn(ws_ref[2 * c + 1], vc)))
            s = jnp.concatenate(cols, axis=1) + bs_ref[...]
            ob_scr[rows, :] = (u_ref[rows, :] * s).astype(BF16)
        h = x_ref[...] + _nn(oa_ref[...].astype(BF16), woa_ref[...]) + _nn(ob_scr[...], wob_ref[...])
        h_scr[...] = h
        hn_scr[...] = _rms(h, g2_ref[...]).astype(BF16)
        acc_scr[...] = jnp.zeros(acc_scr.shape, F32)

    f = jnp.maximum(_nn(hn_scr[...], w1_ref[...]), 0.0)
    acc_scr[...] += _nn((f * f).astype(BF16), w2_ref[...])

    @pl.when(j == pl.num_programs(1) - 1)
    def _():
        y_ref[...] = _rms(h_scr[...] + acc_scr[...], gf_ref[...])


def _out_ffn(x, o_a, u, v, ws, bs, woa, wob, g2, w1, w2, gf, tm=512, tf=2048):
    n = x.shape[0]
    row = lambda i, j: (i, 0)
    const = lambda i, j: (0, 0)
    return pl.pallas_call(
        _ffn_kernel,
        grid=(n // tm, D_FF // tf),
        in_specs=[pl.BlockSpec((tm, D_MODEL), row), pl.BlockSpec((tm, 512), row), pl.BlockSpec((tm, 512), row),
                  pl.BlockSpec((tm, 512), row), pl.BlockSpec((8, CHUNK, CHUNK), lambda i, j: (0, 0, 0)),
                  pl.BlockSpec((CHUNK, 512), const), pl.BlockSpec((512, D_MODEL), const),
                  pl.BlockSpec((512, D_MODEL), const), pl.BlockSpec((1, D_MODEL), const),
                  pl.BlockSpec((D_MODEL, tf), lambda i, j: (0, j)), pl.BlockSpec((tf, D_MODEL), lambda i, j: (j, 0)),
                  pl.BlockSpec((1, D_MODEL), const)],
        out_specs=pl.BlockSpec((tm, D_MODEL), row),
        out_shape=jax.ShapeDtypeStruct((n, D_MODEL), F32),
        scratch_shapes=[pltpu.VMEM((tm, D_MODEL), F32), pltpu.VMEM((tm, D_MODEL), BF16),
                        pltpu.VMEM((tm, 512), BF16), pltpu.VMEM((tm, D_MODEL), F32)],
        compiler_params=pltpu.CompilerParams(dimension_semantics=("arbitrary", "arbitrary"),
                                             vmem_limit_bytes=VMEM_LIMIT),
        name="nsa_out_ffn",
    )(x, o_a, u, v, ws, bs, woa, wob, g2, w1, w2, gf)


def _head_perm():
    j = np.arange(512)
    return ((j // 128) + 4 * ((j % 128) // 64)) * 64 + (j % 64)


def _cover_t(n_c, n_s, n_c_pad, n_s_pad):
    ci = np.arange(n_c)[:, None] * CMP_STRIDE
    sj = np.arange(n_s)[None, :] * SLC_BLOCK
    cover = np.clip(np.minimum(ci + CMP_BLOCK, sj + SLC_BLOCK) - np.maximum(ci, sj), 0, None) / CMP_BLOCK
    out = np.zeros((n_s_pad, n_c_pad), np.float32)
    out[:n_s, :n_c] = cover.T
    return jnp.asarray(out)


def _cmp_weights(cmp_w1, cmp_b1, cmp_w2, cmp_b2, cmp_pos):
    z = jnp.zeros((CMP_HIDDEN, HEAD_DIM), F32)
    w1 = [jnp.concatenate([cmp_w1[i, :1024], cmp_w1[i, 1024:]], axis=1).astype(BF16) for i in range(2)]
    pos = [cmp_pos[i].reshape(2, 1024) for i in range(2)]
    b1 = [cmp_b1[i].reshape(1, CMP_HIDDEN) for i in range(2)]
    w2a = [jnp.concatenate([cmp_w2[i], z], axis=1).astype(BF16) for i in range(2)]
    w2b = [jnp.concatenate([z, cmp_w2[i]], axis=1).astype(BF16) for i in range(2)]
    b2 = [jnp.concatenate([cmp_b2[i], cmp_b2[i]]).reshape(1, LANES) for i in range(2)]
    return (w1[0], w1[1], pos[0], pos[1], b1[0], b1[1], w2a[0], w2b[0], w2a[1], w2b[1], b2[0], b2[1])


def kernel(x_prompt, x_sample, cache_kv, state_win_kv, page_table, ln1_g, w_in, cmp_w1, cmp_b1, cmp_w2, cmp_b2,
           cmp_pos, ln_v_g, ln_v_b, w_s, b_s, w_out, ln2_g, w_ff1, w_ff2, ln_f_g):
    batch, seq, _ = x_prompt.shape
    n_seq, n_q, _ = x_sample.shape
    n_pages = page_table.shape[1]
    past = n_pages * PAGE
    perm = _head_perm()

    wi = w_in[0]
    w_all = jnp.concatenate(
        [wi[:, 0:512][:, perm], wi[:, 512:1304], jnp.zeros((D_MODEL, C_U - C_GATE - 24), F32), wi[:, 1304:2328]],
        axis=1).astype(BF16)
    g1 = ln1_g[0].reshape(1, D_MODEL)
    lvg = ln_v_g[0].reshape(1, 512)
    lvb = ln_v_b[0].reshape(1, 512)
    cw = _cmp_weights(cmp_w1[0], cmp_b1[0], cmp_w2[0], cmp_b2[0], cmp_pos[0])
    tril = jnp.tril(jnp.ones((CHUNK, CHUNK), F32))
    ws_p = (w_s[0] * tril).astype(BF16)
    bs_p = jnp.repeat(b_s[0].T, HEAD_DIM, axis=1)
    reps = CHUNK // n_q
    ws_s = jnp.einsum("ab,gij->gaibj", jnp.eye(reps, dtype=F32), (w_s[0] * tril)[:, :n_q, :n_q])
    ws_s = ws_s.reshape(8, CHUNK, CHUNK).astype(BF16)
    bs_s = jnp.tile(jnp.repeat(b_s[0].T[:n_q], HEAD_DIM, axis=1), (reps, 1))
    woa = w_out[0][:512][perm].astype(BF16)
    wob = w_out[0][512:].astype(BF16)
    g2 = ln2_g[0].reshape(1, D_MODEL)
    gf = ln_f_g.reshape(1, D_MODEL)
    w1 = w_ff1[0].astype(BF16)
    w2 = w_ff2[0].astype(BF16)

    xp = x_prompt.reshape(batch * seq, D_MODEL)
    q_p, kv_p, _, kvw_p, gate_p, u_p, v_p, kvt_p, wkvt_p = _project(xp, g1, w_all, lvg, lvb, seq)
    kc_p, vc_p = _compress_prompt(kv_p, cw, batch, seq)
    n_seg_p = seq // CMP_STRIDE
    cov_p = _cover_t(n_seg_p - 1, seq // SLC_BLOCK, n_seg_p, seq // SLC_BLOCK)
    oa_p = _nsa_prompt(q_p, gate_p, kc_p, vc_p, kvw_p, cov_p, batch, seq)
    y_p = _out_ffn(xp, oa_p, u_p, v_p, ws_p, bs_p, woa, wob, g2, w1, w2, gf)

    xs = x_sample.reshape(n_seq * n_q, D_MODEL)
    q_s, kv_s, wkv_s, _, gate_s, u_s, v_s, _, _ = _project(xs, g1, w_all, lvg, lvb, n_seq * n_q)
    cache = jnp.transpose(cache_kv[0], (0, 2, 3, 4, 1)).reshape(-1, 4, LANES, PAGE)
    win_state = jnp.transpose(state_win_kv[0], (0, 2, 3, 4, 1)).reshape(n_seq, 2, LANES, -1)
    pt = page_table.reshape(-1)
    n_seg_s = past // CMP_STRIDE
    n_s = past // SLC_BLOCK + 1
    n_sp = -(-n_s // LANES) * LANES
    cov_s = _cover_t(n_seg_s, n_s, n_seg_s, n_sp)
    q_s3 = q_s.reshape(n_seq, n_q, 512)
    wbd = (_blockdiag_w1(cmp_w1[0, 0]), _blockdiag_w1(cmp_w1[0, 1]))
    ocmp_s, msel = _sample_compress(pt, cache, q_s3, wbd, cw, cov_s)
    ebig_t = jnp.transpose(_block_bias(past))
    oa_s = _sample_attend(pt, cache, q_s3, msel, ebig_t, kv_s.reshape(n_seq, n_q, 512), win_state,
                          wkv_s.reshape(n_seq, n_q, 256), gate_s.reshape(n_seq, n_q, 128), ocmp_s)
    y_s = _out_ffn(xs, oa_s.reshape(n_seq * n_q, 512), u_s, v_s, ws_s, bs_s, woa, wob, g2, w1, w2, gf)

    keep = min(WINDOW, seq)
    new_kv_p = jnp.transpose(kvt_p.reshape(batch, 4, N_KVH, HEAD_DIM, seq), (0, 4, 1, 2, 3))
    new_win_p = jnp.transpose(wkvt_p[:, :, seq - keep:].reshape(batch, 2, N_KVH, HEAD_DIM, keep), (0, 4, 1, 2, 3))
    new_win_s = jnp.concatenate([state_win_kv[0], wkv_s.reshape(n_seq, n_q, 2, N_KVH, HEAD_DIM)], axis=1)[:, n_q:]
    return (y_p.reshape(batch, seq, D_MODEL),
            y_s.reshape(n_seq, n_q, D_MODEL),
            new_kv_p[None],
            kv_s.reshape(1, n_seq, n_q, 4, N_KVH, HEAD_DIM),
            new_win_p[None],
            new_win_s[None],
            v_s.reshape(1, n_seq, n_q, 512))
```

```python
import functools

import numpy as np
import jax
import jax.numpy as jnp
from jax import lax
from jax.experimental import pallas as pl
from jax.experimental.pallas import tpu as pltpu

F32 = jnp.float32
BF16 = jnp.bfloat16

D_MODEL = 1024
HEAD_DIM = 64
N_HEADS = 8
N_KVH = 2
GQA = 4
CMP_BLOCK = 32
CMP_STRIDE = 16
CMP_HIDDEN = 128
SLC_BLOCK = 64
N_SELECT = 16
WINDOW = 512
CHUNK = 128
D_FF = 4096
PAGE = 128
EPS = 1e-6
NEG = -1e30
LOG2E = 1.4426950408889634
FORCE_SCORE = 1e9
LANES = 128
VMEM_LIMIT = 56 * 1024 * 1024
SLC_TILE = 1024

C_Q, C_KV, C_WIN, C_GATE, C_U, C_V, C_END = 0, 512, 1024, 1280, 1408, 1920, 2432


def _nn(a, b):
    return jnp.dot(a, b, preferred_element_type=F32)


def _nt(a, b):
    return lax.dot_general(a, b, (((1,), (1,)), ((), ())), preferred_element_type=F32)


def _gelu(x):
    return 0.5 * x * (1.0 + jnp.tanh(0.7978845608028654 * (x + 0.044715 * (x * x * x))))


def _rms(x, g):
    return x * lax.rsqrt(jnp.mean(x * x, axis=-1, keepdims=True) + EPS) * g


def _softmax_rows(s, mask):
    s = jnp.where(mask, s, NEG)
    m = jnp.max(s, axis=-1, keepdims=True)
    p = jnp.where(mask, jnp.exp2(s - m), 0.0)
    l = jnp.sum(p, axis=-1, keepdims=True)
    return p / jnp.maximum(l, 1e-30)


def _proj_kernel(x_ref, g1_ref, w_ref, lvg_ref, lvb_ref,
                 q_ref, kv_ref, wkv_ref, kvw_ref, gate_ref, u_ref, v_ref, kvt_ref, wkvt_ref):
    x = x_ref[...]
    h = _rms(x, g1_ref[...]).astype(BF16)

    def z(a, b):
        return _nn(h, w_ref[:, a:b])

    q_ref[...] = z(C_Q, C_KV) * (HEAD_DIM ** -0.5 * LOG2E)
    kv = z(C_KV, C_WIN)
    kv_ref[...] = kv
    kvt_ref[...] = kv.T
    wkv = z(C_WIN, C_GATE)
    wkv_ref[...] = wkv
    wkvt_ref[...] = wkv.T
    kvw_ref[:, 0:256] = kv[:, 256:512].astype(BF16)
    kvw_ref[:, 256:512] = wkv.astype(BF16)
    gate_ref[...] = jax.nn.sigmoid(z(C_GATE, C_U))
    u_ref[...] = _gelu(z(C_U, C_V))
    zv = _gelu(z(C_V, C_END))
    mu = jnp.mean(zv, axis=-1, keepdims=True)
    var = jnp.mean(jnp.square(zv - mu), axis=-1, keepdims=True)
    v_ref[...] = (zv - mu) * lax.rsqrt(var + EPS) * lvg_ref[...] + lvb_ref[...]


def _project(x, g1, w_all, lvg, lvb, seq, tm=512):
    n = x.shape[0]
    per = seq // tm
    row = lambda i: (i, 0)
    const = lambda i: (0, 0)
    colmajor = lambda i: (i // per, 0, i % per)
    widths = (512, 512, 256, 512, 128, 512, 512)
    dtypes = (F32, F32, F32, BF16, F32, F32, F32)
    return pl.pallas_call(
        _proj_kernel,
        grid=(n // tm,),
        in_specs=[pl.BlockSpec((tm, D_MODEL), row), pl.BlockSpec((1, D_MODEL), const),
                  pl.BlockSpec((D_MODEL, C_END), const), pl.BlockSpec((1, 512), const),
                  pl.BlockSpec((1, 512), const)],
        out_specs=[pl.BlockSpec((tm, w), row) for w in widths]
        + [pl.BlockSpec((None, 512, tm), colmajor), pl.BlockSpec((None, 256, tm), colmajor)],
        out_shape=[jax.ShapeDtypeStruct((n, w), d) for w, d in zip(widths, dtypes)]
        + [jax.ShapeDtypeStruct((n // seq, 512, seq), F32), jax.ShapeDtypeStruct((n // seq, 256, seq), F32)],
        compiler_params=pltpu.CompilerParams(dimension_semantics=("arbitrary",), vmem_limit_bytes=VMEM_LIMIT),
        name="nsa_proj",
    )(x, g1, w_all, lvg, lvb)


def _compress_tail(h0, h1, w1_ref, pos_ref, b1_ref, w2a_ref, w2b_ref, b2_ref, n_seg):
    w1 = w1_ref[...]
    pos = pos_ref[...]
    p0 = jnp.broadcast_to(pos[0:1], (8, 1024)).astype(BF16)
    p1 = jnp.broadcast_to(pos[1:2], (8, 1024)).astype(BF16)
    c1 = b1_ref[...] + _nn(p0, w1[:, 0:128])[0:1] + _nn(p1, w1[:, 128:256])[0:1]

    def hidden(h):
        nxt = pltpu.roll(h[:, 128:256], n_seg - 1, 0)
        return _gelu(h[:, 0:128] + nxt + c1).astype(BF16)

    return _nn(hidden(h0), w2a_ref[...]) + _nn(hidden(h1), w2b_ref[...]) + b2_ref[...]


def _compress_core(lhs0, lhs1, w1_ref, pos_ref, b1_ref, w2a_ref, w2b_ref, b2_ref, n_seg):
    w1 = w1_ref[...]
    return _compress_tail(_nn(lhs0, w1), _nn(lhs1, w1), w1_ref, pos_ref, b1_ref, w2a_ref, w2b_ref, b2_ref, n_seg)


def _cmp_prompt_kernel(k_ref, v_ref, w1k_ref, w1v_ref, posk_ref, posv_ref, b1k_ref, b1v_ref,
                       w2ak_ref, w2bk_ref, w2av_ref, w2bv_ref, b2k_ref, b2v_ref,
                       kc_ref, vc_ref, u_scr):
    n_seg = k_ref.shape[0] // CMP_STRIDE
    lo = lax.broadcasted_iota(jnp.int32, (n_seg, LANES), 1) < HEAD_DIM

    def run(x_ref, w1, pos, b1, w2a, w2b, b2, out_ref):
        for pr in range(CMP_STRIDE // 2):
            a = x_ref[pl.ds(2 * pr, n_seg, stride=CMP_STRIDE), :]
            b = x_ref[pl.ds(2 * pr + 1, n_seg, stride=CMP_STRIDE), :]
            u_scr[0, :, pr * LANES:(pr + 1) * LANES] = jnp.where(lo, a, pltpu.roll(b, HEAD_DIM, 1)).astype(BF16)
            u_scr[1, :, pr * LANES:(pr + 1) * LANES] = jnp.where(lo, pltpu.roll(a, HEAD_DIM, 1), b).astype(BF16)
        out_ref[0] = _compress_core(u_scr[0], u_scr[1], w1, pos, b1, w2a, w2b, b2, n_seg)

    run(k_ref, w1k_ref, posk_ref, b1k_ref, w2ak_ref, w2bk_ref, b2k_ref, kc_ref)
    run(v_ref, w1v_ref, posv_ref, b1v_ref, w2av_ref, w2bv_ref, b2v_ref, vc_ref)


def _cmp_weight_specs(const):
    return ([pl.BlockSpec((1024, 256), const)] * 2 + [pl.BlockSpec((2, 1024), const)] * 2
            + [pl.BlockSpec((1, 128), const)] * 2 + [pl.BlockSpec((128, 128), const)] * 4
            + [pl.BlockSpec((1, 128), const)] * 2)


def _compress_prompt(kv, cw, batch, seq):
    n_seg = seq // CMP_STRIDE
    const = lambda b: (0, 0)
    return pl.pallas_call(
        _cmp_prompt_kernel,
        grid=(batch,),
        in_specs=[pl.BlockSpec((seq, LANES), lambda b: (b, 0)), pl.BlockSpec((seq, LANES), lambda b: (b, 1))]
        + _cmp_weight_specs(const),
        out_specs=[pl.BlockSpec((1, n_seg, LANES), lambda b: (b, 0, 0))] * 2,
        out_shape=[jax.ShapeDtypeStruct((batch, n_seg, LANES), F32)] * 2,
        scratch_shapes=[pltpu.VMEM((2, n_seg, 1024), BF16)],
        compiler_params=pltpu.CompilerParams(dimension_semantics=("arbitrary",), vmem_limit_bytes=VMEM_LIMIT),
        name="nsa_cmp_prompt",
    )(kv, kv, *cw)


def _rank_row(score, blk, j, rj):
    lower = jnp.where(blk < j, 1.0, 0.0)
    beat = jnp.where(score > rj, 1.0, jnp.where(score == rj, lower, 0.0))
    return jnp.sum(beat, axis=0, keepdims=True)


def _scores_t(imp_t, blk, t):
    cur = t >> 6
    forced = (blk == 0) | (blk == cur) | (blk == cur - 1)
    visible = blk * SLC_BLOCK <= t
    return jnp.where(forced, FORCE_SCORE, jnp.where(visible, imp_t, -jnp.inf)), visible


def _importance_t(cov_t, psum):
    hi = psum.astype(BF16)
    lo_ = (psum - hi.astype(F32)).astype(BF16)
    c = cov_t.astype(BF16)
    return _nt(c, hi) + _nt(c, lo_)


def _nsa_prompt_kernel(q_ref, g_ref, kc_ref, vc_ref, kvw_ref, covt_ref, ebig_ref, gexp_ref, o_ref,
                       mrun_scr, acc_scr, s_scr):
    qb = pl.program_id(1)
    t0 = qb * 128
    n_rows = N_HEADS * 128
    half = GQA * 128
    lo = lax.broadcasted_iota(jnp.int32, (128, LANES), 1) < HEAD_DIM
    t_all = t0 + (lax.broadcasted_iota(jnp.int32, (n_rows, 1), 0) & 127)
    q = q_ref[...]
    qm = jnp.concatenate(
        [jnp.where(lo if k == 0 else jnp.logical_not(lo), q[:, g * LANES:(g + 1) * LANES], 0.0)
         for k in range(N_KVH) for g in range(GQA)], axis=0).astype(BF16)

    def head(x, k, g):
        return x[(k * GQA + g) * 128:(k * GQA + g + 1) * 128]

    kc = kc_ref[0].astype(BF16)
    vc = vc_ref[0].astype(BF16)
    n_c = kc.shape[0]
    t_q = t0 + lax.broadcasted_iota(jnp.int32, (128, 1), 0)

    def all_heads(bias):
        return jnp.concatenate([bias] * N_HEADS, axis=0)

    s0 = pl.multiple_of(jnp.maximum(t0 - WINDOW, 0), 128)
    wk = kvw_ref[pl.ds(s0, WINDOW + 128), 256:384]
    wv = kvw_ref[pl.ds(s0, WINDOW + 128), 384:512]
    low = lax.broadcasted_iota(jnp.int32, (WINDOW + 128, LANES), 1) < HEAD_DIM
    kpos = s0 + lax.broadcasted_iota(jnp.int32, (128, WINDOW + 128), 1)
    s_w = _nt(qm, wk) + all_heads(jnp.where((kpos <= t_q) & (kpos > t_q - WINDOW), 0.0, NEG))
    p_w = jnp.exp2(s_w - jnp.max(s_w, axis=-1, keepdims=True)).astype(BF16)
    win = [_nn(p_w[0:half], jnp.where(low, wv, 1.0).astype(BF16)),
           _nn(p_w[half:n_rows], jnp.where(low, 1.0, wv).astype(BF16))]

    n_iota = lax.broadcasted_iota(jnp.int32, (128, n_c), 1)
    s_c = _nt(qm, kc) + all_heads(jnp.where(n_iota * CMP_STRIDE + (CMP_BLOCK - 1) <= t_q, 0.0, NEG))
    e_c = jnp.exp2(s_c - jnp.max(s_c, axis=-1, keepdims=True))
    row_ok = jnp.where(t_all >= CMP_BLOCK - 1, 1.0, 0.0)
    p_c = e_c * (row_ok / jnp.sum(e_c, axis=-1, keepdims=True))
    o_cmp = _nn(p_c.astype(BF16), vc)
    psum = [sum(head(p_c, k, g) for g in range(GQA)) for k in range(N_KVH)]

    n_s = covt_ref.shape[0]
    blk = lax.broadcasted_iota(jnp.int32, (n_s, 128), 0)
    t_lane = t0 + lax.broadcasted_iota(jnp.int32, (n_s, 128), 1)
    nsel = []
    for k in range(N_KVH):
        score, visible = _scores_t(_importance_t(covt_ref[...], psum[k]), blk, t_lane)
        cnt = jnp.concatenate([_rank_row(score, blk, j, score[j:j + 1, :]) for j in range(n_s)], axis=0)
        m_t = jnp.where((cnt < N_SELECT) & visible, 1.0, 0.0)
        m_t = jnp.concatenate([m_t, jnp.zeros((128 - n_s, 128), F32)], axis=0)
        nsel.append((1.0 - m_t.T).astype(BF16))

    lhs = jnp.concatenate([qm, jnp.concatenate([nsel[0]] * GQA + [nsel[1]] * GQA, axis=0)], axis=1)
    mrun_scr[...] = jnp.full(mrun_scr.shape, NEG, F32)
    acc_scr[...] = jnp.zeros(acc_scr.shape, F32)
    tk = SLC_TILE
    n_full = qb // (tk // 128)

    def max_tile(k0, causal):
        rhs = jnp.concatenate([kvw_ref[pl.ds(k0, tk), 0:128], ebig_ref[pl.ds(k0, tk), :]], axis=1)
        s = _nt(lhs, rhs)
        if causal:
            s = s + all_heads(jnp.where(k0 + lax.broadcasted_iota(jnp.int32, (128, tk), 1) <= t_q, 0.0, NEG))
        s_scr[:, pl.ds(k0, tk)] = s
        m = s[:, 0:LANES]
        for j in range(1, tk // LANES):
            m = jnp.maximum(m, s[:, j * LANES:(j + 1) * LANES])
        mrun_scr[...] = jnp.maximum(mrun_scr[...], m)

    def loop1(kt, carry):
        max_tile(pl.multiple_of(kt * tk, tk), False)
        return carry

    max_tile(pl.multiple_of(n_full * tk, tk), True)
    lax.fori_loop(0, n_full, loop1, 0)
    mb =jnp.broadcast_to(jnp.max(mrun_scr[...], axis=-1, keepdims=True), (n_rows, LANES))

    def loop2(kt, carry):
        k0 = pl.multiple_of(kt * tk, tk)
        v_t = kvw_ref[pl.ds(k0, tk), 128:256]
        lov = lax.broadcasted_iota(jnp.int32, (tk, LANES), 1) < HEAD_DIM
        p = jnp.exp2(s_scr[:, pl.ds(k0, tk)] - jnp.concatenate([mb] * (tk // LANES), axis=1)).astype(BF16)
        acc_scr[0] += _nn(p[0:half], jnp.where(lov, v_t, 1.0).astype(BF16))
        acc_scr[1] += _nn(p[half:n_rows], jnp.where(lov, 1.0, v_t).astype(BF16))
        return carry

    lax.fori_loop(0, n_full + 1, loop2, 0)

    def normalised(n0, n1):
        return jnp.where(lo, n0, n1) / pltpu.roll(jnp.where(lo, n1, n0), HEAD_DIM, 1)

    gt = g_ref[...]
    g_hi = gt.astype(BF16)
    g_lo = (gt - g_hi.astype(F32)).astype(BF16)
    gexp = _nn(g_hi, gexp_ref[...]) + _nn(g_lo, gexp_ref[...])
    for c in range(GQA):
        cols = slice(c * LANES, (c + 1) * LANES)
        rows = slice(c * 128, (c + 1) * 128)
        o_c = jnp.where(lo, head(o_cmp, 0, c), head(o_cmp, 1, c))
        o_s = normalised(acc_scr[0, rows, :], acc_scr[1, rows, :])
        o_w = normalised(win[0][rows], win[1][rows])
        o_ref[:, cols] = (gexp[:, cols] * o_c + gexp[:, 512 + c * LANES:512 + (c + 1) * LANES] * o_s
                          + gexp[:, 1024 + c * LANES:1024 + (c + 1) * LANES] * o_w).astype(BF16)


def _block_bias(seq):
    m = (np.arange(seq)[:, None] // SLC_BLOCK == np.arange(LANES)[None, :]).astype(np.float32) * NEG
    return jnp.asarray(m).astype(BF16)


def _gate_expand():
    m = np.zeros((LANES, 3 * 512), np.float32)
    col = np.arange(512)
    head = col // LANES + GQA * ((col % LANES) // HEAD_DIM)
    for br in range(3):
        m[br * N_HEADS + head, br * 512 + col] = 1.0
    return jnp.asarray(m).astype(BF16)


def _nsa_prompt(q, gate, kc, vc, kvw, cov_t, batch, seq):
    n_qb = seq // 128
    n_seg = kc.shape[1]
    row = lambda b, i: (b * n_qb + i, 0)
    const = lambda b, i: (0, 0)
    return pl.pallas_call(
        _nsa_prompt_kernel,
        grid=(batch, n_qb),
        in_specs=[pl.BlockSpec((128, 512), row), pl.BlockSpec((128, 128), row),
                  pl.BlockSpec((1, n_seg, LANES), lambda b, i: (b, 0, 0)),
                  pl.BlockSpec((1, n_seg, LANES), lambda b, i: (b, 0, 0)),
                  pl.BlockSpec((seq, 512), lambda b, i: (b, 0)),
                  pl.BlockSpec(cov_t.shape, const), pl.BlockSpec((seq, LANES), const),
                  pl.BlockSpec((LANES, 3 * 512), const)],
        out_specs=pl.BlockSpec((128, 512), row),
        out_shape=jax.ShapeDtypeStruct((batch * seq, 512), BF16),
        scratch_shapes=[pltpu.VMEM((N_HEADS * 128, LANES), F32), pltpu.VMEM((N_KVH, GQA * 128, LANES), F32),
                        pltpu.VMEM((N_HEADS * 128, seq), F32)],
        compiler_params=pltpu.CompilerParams(dimension_semantics=("arbitrary", "arbitrary"),
                                             vmem_limit_bytes=VMEM_LIMIT),
        name="nsa_prompt_attn",
    )(q, gate, kc, vc, kvw, cov_t, _block_bias(seq), _gate_expand())


def _stack_queries(q, lo):
    rows = [jnp.where(lo if k == 0 else jnp.logical_not(lo), q[:, g * LANES:(g + 1) * LANES], 0.0)
            for k in range(N_KVH) for g in range(GQA)]
    return jnp.concatenate(rows, axis=0).astype(BF16)


def _fetch_pages(pt_ref, cache_ref, buf, sem, n_pages, half, n_seq):
    b = pl.program_id(0)
    slot = lax.rem(b, 2)

    def copies(step, dst_slot):
        seq = jnp.minimum(step, n_seq - 1)
        return [pltpu.make_async_copy(cache_ref.at[pt_ref[seq * n_pages + p], pl.ds(2 * half, 2)],
                                      buf.at[dst_slot, p], sem.at[dst_slot]) for p in range(n_pages)]

    @pl.when(b == 0)
    def _():
        for c in copies(0, 0):
            c.start()

    @pl.when(b + 1 < pl.num_programs(0))
    def _():
        for c in copies(b + 1, 1 - slot):
            c.start()

    for c in copies(b, slot):
        c.wait()
    return [buf.at[slot, p] for p in range(n_pages)]


def _smp_cmp_kernel(n_pages, n_seq, pt_ref, cache_ref, *refs):
    (q_ref, perm_ref, wbdk_ref, wbdv_ref, w1k_ref, w1v_ref, posk_ref, posv_ref, b1k_ref, b1v_ref, w2ak_ref,
     w2bk_ref, w2av_ref, w2bv_ref, b2k_ref, b2v_ref, covt_ref, ocmp_ref, msel_ref,
     u_scr, kv_scr, page_buf, page_sem) = refs

    @pl.when(pl.program_id(0) == 0)
    def _():
        kv_scr[...] = jnp.zeros(kv_scr.shape, F32)

    pages = _fetch_pages(pt_ref, cache_ref, page_buf, page_sem, n_pages, 0, n_seq)
    seg_pp = PAGE // CMP_STRIDE
    n_seg = n_pages * seg_pp
    past = n_pages * PAGE

    perm = perm_ref[...]
    n_chunk = 4
    pairs_per = (n_pages // 2) // n_chunk

    def gather(ch):
        for pp in range(ch * pairs_per, (ch + 1) * pairs_per):
            ra = _nt(perm, jnp.concatenate([pages[2 * pp][0], pages[2 * pp][1]], axis=0).astype(BF16))
            rb = _nt(perm, jnp.concatenate([pages[2 * pp + 1][0], pages[2 * pp + 1][1]], axis=0).astype(BF16))
            rows = slice(pp * 2 * seg_pp, (pp + 1) * 2 * seg_pp)
            for s in range(CMP_STRIDE):
                piece = jnp.concatenate([ra[s * seg_pp:(s + 1) * seg_pp], rb[s * seg_pp:(s + 1) * seg_pp]], axis=0)
                piece = piece.astype(BF16)
                u_scr[0, rows, s * LANES:(s + 1) * LANES] = piece[:, 0:LANES]
                u_scr[1, rows, s * LANES:(s + 1) * LANES] = piece[:, LANES:2 * LANES]

    kc = kv_scr[0]
    vc = kv_scr[1]
    gather(0)

    q = q_ref[...]
    n_q = q.shape[0]
    lo8 = lax.broadcasted_iota(jnp.int32, (n_q, LANES), 1) < HEAD_DIM
    q_all = _stack_queries(q, lo8)
    t_row = past + (lax.broadcasted_iota(jnp.int32, (N_HEADS * n_q, n_seg), 0) & (n_q - 1))
    n_iota = lax.broadcasted_iota(jnp.int32, (N_HEADS * n_q, n_seg), 1)
    p = _softmax_rows(_nt(q_all, kc.astype(BF16)), n_iota * CMP_STRIDE + (CMP_BLOCK - 1) <= t_row)
    o = _nn(p.astype(BF16), vc.astype(BF16))
    half = GQA * n_q
    for c in range(GQA):
        ocmp_ref[:, c * LANES:(c + 1) * LANES] = jnp.where(lo8, o[c * n_q:(c + 1) * n_q],
                                                         o[half + c * n_q:half + (c + 1) * n_q])
    psum = [sum(p[k * half + g * n_q:k * half + (g + 1) * n_q] for g in range(GQA)) for k in range(N_KVH)]
    psum = jnp.concatenate(psum + [jnp.zeros((LANES - N_KVH * n_q, n_seg), F32)], axis=0)
    gather(1)

    n_sp = covt_ref.shape[0]
    n_s = past // SLC_BLOCK + 1
    rows_used = -(-n_s // 8) * 8
    n_r = N_KVH * n_q
    blk_t = lax.broadcasted_iota(jnp.int32, (n_sp, LANES), 0)
    t_lane = past + (lax.broadcasted_iota(jnp.int32, (n_sp, LANES), 1) & (n_q - 1))
    score_t, visible_t = _scores_t(_importance_t(covt_ref[...], psum), blk_t, t_lane)
    score = score_t.T[0:n_r]
    visible = jnp.where(visible_t, 1.0, 0.0).T[0:n_r] > 0.5
    gather(2)
    gather(3)

    ii = lax.broadcasted_iota(jnp.int32, (rows_used, n_sp), 0)
    jj = lax.broadcasted_iota(jnp.int32, (rows_used, n_sp), 1)
    lower = jnp.where(ii < jj, 1.0, 0.0)

    def rank_rows(r0, r1):
        out = []
        for r in range(r0, r1):
            col = score_t[0:rows_used, r:r + 1]
            row = score[r:r + 1, :]
            beat = jnp.where(col > row, 1.0, jnp.where(col == row, lower, 0.0))
            out.append(jnp.sum(beat, axis=0, keepdims=True))
        return out

    hk = _nn(u_scr[0], wbdk_ref[...])
    cnt = rank_rows(0, n_r // 2)
    hv = _nn(u_scr[1], wbdv_ref[...])
    cnt = jnp.concatenate(cnt + rank_rows(n_r // 2, n_r), axis=0)
    msel_ref[...] = jnp.where((cnt < N_SELECT) & visible, 1.0, 0.0)
    kv_scr[0] = _compress_tail(hk[:, 0:256], hk[:, 256:512], w1k_ref, posk_ref, b1k_ref, w2ak_ref, w2bk_ref, b2k_ref,
                               n_seg)
    kv_scr[1] = _compress_tail(hv[:, 0:256], hv[:, 256:512], w1v_ref, posv_ref, b1v_ref, w2av_ref, w2bv_ref, b2v_ref,
                               n_seg)


def _page_scratch(n_pages):
    return [pltpu.VMEM((2, n_pages, 2, LANES, PAGE), F32), pltpu.SemaphoreType.DMA((2,))]


def _segment_perm():
    r = np.arange(PAGE)
    m = np.zeros((PAGE, PAGE), np.float32)
    m[r, (r % (PAGE // CMP_STRIDE)) * CMP_STRIDE + r // (PAGE // CMP_STRIDE)] = 1.0
    return jnp.asarray(m).astype(BF16)


def _blockdiag_w1(w1):
    w = jnp.transpose(w1.reshape(2, CMP_STRIDE, HEAD_DIM, CMP_HIDDEN), (1, 2, 0, 3))
    w = w.reshape(CMP_STRIDE, HEAD_DIM, 2 * CMP_HIDDEN)
    z = jnp.zeros_like(w)
    out = jnp.concatenate([jnp.concatenate([w, z], axis=2), jnp.concatenate([z, w], axis=2)], axis=1)
    return out.reshape(CMP_STRIDE * LANES, 4 * CMP_HIDDEN).astype(BF16)


def _sample_compress(pt, cache, q_s, wbd, cw, cov_t):
    n_seq, n_q, _ = q_s.shape
    n_pages = pt.shape[0] // n_seq
    n_seg = n_pages * (PAGE // CMP_STRIDE)
    n_sp = cov_t.shape[0]
    const = lambda b, pt: (0, 0)
    kern = functools.partial(_smp_cmp_kernel, n_pages, n_seq)
    prev = lambda b, pt: (jnp.maximum(b - 1, 0), 0, 0)
    return pl.pallas_call(
        kern,
        grid_spec=pltpu.PrefetchScalarGridSpec(
            num_scalar_prefetch=1, grid=(n_seq + 1,),
            in_specs=[pl.BlockSpec(memory_space=pl.ANY)]
            + [pl.BlockSpec((None, n_q, 512), prev), pl.BlockSpec((PAGE, PAGE), const)]
            + [pl.BlockSpec((CMP_STRIDE * LANES, 4 * CMP_HIDDEN), const)] * 2
            + _cmp_weight_specs(const) + [pl.BlockSpec(cov_t.shape, const)],
            out_specs=[pl.BlockSpec((None, n_q, 512), prev), pl.BlockSpec((None, N_KVH * n_q, n_sp), prev)],
            scratch_shapes=[pltpu.VMEM((2, n_seg, CMP_STRIDE * LANES), BF16),
                            pltpu.VMEM((2, n_seg, LANES), F32)] + _page_scratch(n_pages)),
        out_shape=[jax.ShapeDtypeStruct((n_seq, n_q, 512), F32),
                   jax.ShapeDtypeStruct((n_seq, N_KVH * n_q, n_sp), F32)],
        compiler_params=pltpu.CompilerParams(dimension_semantics=("arbitrary",), vmem_limit_bytes=VMEM_LIMIT),
        name="nsa_sample_cmp",
    )(pt, cache, q_s, _segment_perm(), *wbd, *cw, cov_t)


def _smp_attn_kernel(n_pages, n_seq, pt_ref, cache_ref, *refs):
    (q_ref, msel_ref, ebig_ref, kvn_ref, win_ref, wkvn_ref, g_ref, ocmp_ref, o_ref, page_buf, page_sem) = refs
    pages = _fetch_pages(pt_ref, cache_ref, page_buf, page_sem, n_pages, 1, n_seq)
    past = n_pages * PAGE
    q = q_ref[...]
    n_q = q.shape[0]
    n_rows = N_HEADS * n_q
    lo8 = lax.broadcasted_iota(jnp.int32, (n_q, LANES), 1) < HEAD_DIM
    q_all = _stack_queries(q, lo8)
    qi = lax.broadcasted_iota(jnp.int32, (n_rows, 1), 0) & (n_q - 1)

    def pad_rows(x):
        return jnp.concatenate([x, jnp.zeros((LANES - n_q, LANES), F32)], axis=0).astype(BF16)

    def rep(x):
        return jnp.concatenate([x[0:n_q]] * GQA + [x[n_q:2 * n_q]] * GQA, axis=0)

    msel = msel_ref[...]
    n_blk = past // SLC_BLOCK
    lhs = jnp.concatenate([q_all, rep(1.0 - msel[:, 0:LANES]).astype(BF16)], axis=1)
    s_pages = [_nn(lhs, jnp.concatenate([pages[p][0].astype(BF16), ebig_ref[:, p * PAGE:(p + 1) * PAGE]], axis=0))
               for p in range(n_pages)]
    kvn = kvn_ref[...]
    k_tail = pad_rows(kvn[:, 256:384])
    v_tail = pad_rows(kvn[:, 384:512])
    ti = lax.broadcasted_iota(jnp.int32, (n_rows, LANES), 1)
    ok_tail = (rep(jnp.broadcast_to(msel[:, n_blk:n_blk + 1], (2 * n_q, LANES))) > 0.5) & (ti <= qi)
    s_tail = jnp.where(ok_tail, _nt(q_all, k_tail), NEG)
    m_run = s_tail
    for s_p in s_pages:
        m_run = jnp.maximum(m_run, s_p)
    m = jnp.max(m_run, axis=-1, keepdims=True)
    p_tail = jnp.exp2(s_tail - m)
    l_run = p_tail
    pv = _nn(p_tail.astype(BF16), v_tail)
    for p, s_p in enumerate(s_pages):
        p_p = jnp.exp2(s_p - m)
        l_run = l_run + p_p
        pv = pv + _nt(p_p.astype(BF16), pages[p][1].astype(BF16))
    o_slc = pv / jnp.sum(l_run, axis=-1, keepdims=True)

    wk = win_ref[0].astype(BF16)
    wv = win_ref[1].astype(BF16)
    wn = wkvn_ref[...]
    wk_tail = pad_rows(wn[:, 0:128])
    wv_tail = pad_rows(wn[:, 128:256])
    mi = lax.broadcasted_iota(jnp.int32, (n_rows, WINDOW), 1)
    sw_main = jnp.where(mi > qi, _nn(q_all, wk), NEG)
    sw_tail = jnp.where(ti <= qi, _nt(q_all, wk_tail), NEG)
    mw = jnp.maximum(jnp.max(sw_main, axis=-1, keepdims=True), jnp.max(sw_tail, axis=-1, keepdims=True))
    pw_main = jnp.exp2(sw_main - mw)
    pw_tail = jnp.exp2(sw_tail - mw)
    lw = jnp.sum(pw_main, axis=-1, keepdims=True) + jnp.sum(pw_tail, axis=-1, keepdims=True)
    o_win = (_nt(pw_main.astype(BF16), wv) + _nn(pw_tail.astype(BF16), wv_tail)) / lw

    gt = g_ref[...]
    ocmp = ocmp_ref[...]
    half = GQA * n_q
    for c in range(GQA):
        def pick(o):
            return jnp.where(lo8, o[c * n_q:(c + 1) * n_q], o[half + c * n_q:half + (c + 1) * n_q])

        def gate(br):
            return jnp.where(lo8, gt[:, br * 8 + c:br * 8 + c + 1], gt[:, br * 8 + GQA + c:br * 8 + GQA + c + 1])

        o_ref[:, c * LANES:(c + 1) * LANES] = (gate(0) * ocmp[:, c * LANES:(c + 1) * LANES]
                                               + gate(1) * pick(o_slc) + gate(2) * pick(o_win))


def _sample_attend(pt, cache, q_s, msel, expand, kv_new, win_state, wkv_new, gate, o_cmp):
    n_seq, n_q, _ = q_s.shape
    n_pages = pt.shape[0] // n_seq
    past = n_pages * PAGE
    n_sp = msel.shape[-1]
    seq3 = lambda b, pt: (b, 0, 0)
    kern = functools.partial(_smp_attn_kernel, n_pages, n_seq)
    return pl.pallas_call(
        kern,
        grid_spec=pltpu.PrefetchScalarGridSpec(
            num_scalar_prefetch=1, grid=(n_seq,),
            in_specs=[pl.BlockSpec(memory_space=pl.ANY)]
            + [pl.BlockSpec((None, n_q, 512), seq3), pl.BlockSpec((None, N_KVH * n_q, n_sp), seq3),
               pl.BlockSpec(expand.shape, lambda b, pt: (0, 0)),
               pl.BlockSpec((None, n_q, 512), seq3),
               pl.BlockSpec((None, 2, LANES, WINDOW), lambda b, pt: (b, 0, 0, 0)),
               pl.BlockSpec((None, n_q, 256), seq3), pl.BlockSpec((None, n_q, 128), seq3),
               pl.BlockSpec((None, n_q, 512), seq3)],
            out_specs=pl.BlockSpec((None, n_q, 512), seq3),
            scratch_shapes=_page_scratch(n_pages)),
        out_shape=jax.ShapeDtypeStruct((n_seq, n_q, 512), F32),
        compiler_params=pltpu.CompilerParams(dimension_semantics=("arbitrary",), vmem_limit_bytes=VMEM_LIMIT),
        name="nsa_sample_attn",
    )(pt, cache, q_s, msel, expand, kv_new, win_state, wkv_new, gate, o_cmp)


def _ffn_kernel(x_ref, oa_ref, u_ref, v_ref, ws_ref, bs_ref, woa_ref, wob_ref, g2_ref, w1_ref, w2_ref, gf_ref,
                y_ref, h_scr, hn_scr, ob_scr, acc_scr):
    j = pl.program_id(1)
    tm = x_ref.shape[0]

    @pl.when(j == 0)
    def _():
        lo = lax.broadcasted_iota(jnp.int32, (CHUNK, LANES), 1) < HEAD_DIM
        for ch in range(tm // CHUNK):
            rows = slice(ch * CHUNK, (ch + 1) * CHUNK)
            cols = []
            for c in range(4):
                vc = v_ref[rows, c * LANES:(c + 1) * LANES].astype(BF16)
                cols.append(jnp.where(lo, _nn(ws_ref[2 * c], vc), _nn(ws_ref[2 * c + 1], vc)))
            s = jnp.concatenate(cols, axis=1) + bs_ref[...]
            ob_scr[rows, :] = (u_ref[rows, :] * s).astype(BF16)
        h = x_ref[...] + _nn(oa_ref[...].astype(BF16), woa_ref[...]) + _nn(ob_scr[...], wob_ref[...])
        h_scr[...] = h
        hn_scr[...] = _rms(h, g2_ref[...]).astype(BF16)
        acc_scr[...] = jnp.zeros(acc_scr.shape, F32)

    f = jnp.maximum(_nn(hn_scr[...], w1_ref[...]), 0.0)
    acc_scr[...] += _nn((f * f).astype(BF16), w2_ref[...])

    @pl.when(j == pl.num_programs(1) - 1)
    def _():
        y_ref[...] = _rms(h_scr[...] + acc_scr[...], gf_ref[...])


def _out_ffn(x, o_a, u, v, ws, bs, woa, wob, g2, w1, w2, gf, tm=512, tf=2048):
    n = x.shape[0]
    row = lambda i, j: (i, 0)
    const = lambda i, j: (0, 0)
    return pl.pallas_call(
        _ffn_kernel,
        grid=(n // tm, D_FF // tf),
        in_specs=[pl.BlockSpec((tm, D_MODEL), row), pl.BlockSpec((tm, 512), row), pl.BlockSpec((tm, 512), row),
                  pl.BlockSpec((tm, 512), row), pl.BlockSpec((8, CHUNK, CHUNK), lambda i, j: (0, 0, 0)),
                  pl.BlockSpec((CHUNK, 512), const), pl.BlockSpec((512, D_MODEL), const),
                  pl.BlockSpec((512, D_MODEL), const), pl.BlockSpec((1, D_MODEL), const),
                  pl.BlockSpec((D_MODEL, tf), lambda i, j: (0, j)), pl.BlockSpec((tf, D_MODEL), lambda i, j: (j, 0)),
                  pl.BlockSpec((1, D_MODEL), const)],
        out_specs=pl.BlockSpec((tm, D_MODEL), row),
        out_shape=jax.ShapeDtypeStruct((n, D_MODEL), F32),
        scratch_shapes=[pltpu.VMEM((tm, D_MODEL), F32), pltpu.VMEM((tm, D_MODEL), BF16),
                        pltpu.VMEM((tm, 512), BF16), pltpu.VMEM((tm, D_MODEL), F32)],
        compiler_params=pltpu.CompilerParams(dimension_semantics=("arbitrary", "arbitrary"),
                                             vmem_limit_bytes=VMEM_LIMIT),
        name="nsa_out_ffn",
    )(x, o_a, u, v, ws, bs, woa, wob, g2, w1, w2, gf)


def _head_perm():
    j = np.arange(512)
    return ((j // 128) + 4 * ((j % 128) // 64)) * 64 + (j % 64)


def _cover_t(n_c, n_s, n_c_pad, n_s_pad):
    ci = np.arange(n_c)[:, None] * CMP_STRIDE
    sj = np.arange(n_s)[None, :] * SLC_BLOCK
    cover = np.clip(np.minimum(ci + CMP_BLOCK, sj + SLC_BLOCK) - np.maximum(ci, sj), 0, None) / CMP_BLOCK
    out = np.zeros((n_s_pad, n_c_pad), np.float32)
    out[:n_s, :n_c] = cover.T
    return jnp.asarray(out)


def _cmp_weights(cmp_w1, cmp_b1, cmp_w2, cmp_b2, cmp_pos):
    z = jnp.zeros((CMP_HIDDEN, HEAD_DIM), F32)
    w1 = [jnp.concatenate([cmp_w1[i, :1024], cmp_w1[i, 1024:]], axis=1).astype(BF16) for i in range(2)]
    pos = [cmp_pos[i].reshape(2, 1024) for i in range(2)]
    b1 = [cmp_b1[i].reshape(1, CMP_HIDDEN) for i in range(2)]
    w2a = [jnp.concatenate([cmp_w2[i], z], axis=1).astype(BF16) for i in range(2)]
    w2b = [jnp.concatenate([z, cmp_w2[i]], axis=1).astype(BF16) for i in range(2)]
    b2 = [jnp.concatenate([cmp_b2[i], cmp_b2[i]]).reshape(1, LANES) for i in range(2)]
    return (w1[0], w1[1], pos[0], pos[1], b1[0], b1[1], w2a[0], w2b[0], w2a[1], w2b[1], b2[0], b2[1])


def kernel(x_prompt, x_sample, cache_kv, state_win_kv, page_table, ln1_g, w_in, cmp_w1, cmp_b1, cmp_w2, cmp_b2,
           cmp_pos, ln_v_g, ln_v_b, w_s, b_s, w_out, ln2_g, w_ff1, w_ff2, ln_f_g):
    batch, seq, _ = x_prompt.shape
    n_seq, n_q, _ = x_sample.shape
    n_pages = page_table.shape[1]
    past = n_pages * PAGE
    perm = _head_perm()

    wi = w_in[0]
    w_all = jnp.concatenate(
        [wi[:, 0:512][:, perm], wi[:, 512:1304], jnp.zeros((D_MODEL, C_U - C_GATE - 24), F32), wi[:, 1304:2328]],
        axis=1).astype(BF16)
    g1 = ln1_g[0].reshape(1, D_MODEL)
    lvg = ln_v_g[0].reshape(1, 512)
    lvb = ln_v_b[0].reshape(1, 512)
    cw = _cmp_weights(cmp_w1[0], cmp_b1[0], cmp_w2[0], cmp_b2[0], cmp_pos[0])
    tril = jnp.tril(jnp.ones((CHUNK, CHUNK), F32))
    ws_p = (w_s[0] * tril).astype(BF16)
    bs_p = jnp.repeat(b_s[0].T, HEAD_DIM, axis=1)
    reps = CHUNK // n_q
    ws_s = jnp.einsum("ab,gij->gaibj", jnp.eye(reps, dtype=F32), (w_s[0] * tril)[:, :n_q, :n_q])
    ws_s = ws_s.reshape(8, CHUNK, CHUNK).astype(BF16)
    bs_s = jnp.tile(jnp.repeat(b_s[0].T[:n_q], HEAD_DIM, axis=1), (reps, 1))
    woa = w_out[0][:512][perm].astype(BF16)
    wob = w_out[0][512:].astype(BF16)
    g2 = ln2_g[0].reshape(1, D_MODEL)
    gf = ln_f_g.reshape(1, D_MODEL)
    w1 = w_ff1[0].astype(BF16)
    w2 = w_ff2[0].astype(BF16)

    xp = x_prompt.reshape(batch * seq, D_MODEL)
    q_p, kv_p, _, kvw_p, gate_p, u_p, v_p, kvt_p, wkvt_p = _project(xp, g1, w_all, lvg, lvb, seq)
    kc_p, vc_p = _compress_prompt(kv_p, cw, batch, seq)
    n_seg_p = seq // CMP_STRIDE
    cov_p = _cover_t(n_seg_p - 1, seq // SLC_BLOCK, n_seg_p, seq // SLC_BLOCK)
    oa_p = _nsa_prompt(q_p, gate_p, kc_p, vc_p, kvw_p, cov_p, batch, seq)
    y_p = _out_ffn(xp, oa_p, u_p, v_p, ws_p, bs_p, woa, wob, g2, w1, w2, gf)

    xs = x_sample.reshape(n_seq * n_q, D_MODEL)
    q_s, kv_s, wkv_s, _, gate_s, u_s, v_s, _, _ = _project(xs, g1, w_all, lvg, lvb, n_seq * n_q)
    cache = jnp.transpose(cache_kv[0], (0, 2, 3, 4, 1)).reshape(-1, 4, LANES, PAGE)
    win_state = jnp.transpose(state_win_kv[0], (0, 2, 3, 4, 1)).reshape(n_seq, 2, LANES, -1)
    pt = page_table.reshape(-1)
    n_seg_s = past // CMP_STRIDE
    n_s = past // SLC_BLOCK + 1
    n_sp = -(-n_s // LANES) * LANES
    cov_s = _cover_t(n_seg_s, n_s, n_seg_s, n_sp)
    q_s3 = q_s.reshape(n_seq, n_q, 512)
    wbd = (_blockdiag_w1(cmp_w1[0, 0]), _blockdiag_w1(cmp_w1[0, 1]))
    ocmp_s, msel = _sample_compress(pt, cache, q_s3, wbd, cw, cov_s)
    ebig_t = jnp.transpose(_block_bias(past))
    oa_s = _sample_attend(pt, cache, q_s3, msel, ebig_t, kv_s.reshape(n_seq, n_q, 512), win_state,
                          wkv_s.reshape(n_seq, n_q, 256), gate_s.reshape(n_seq, n_q, 128), ocmp_s)
    y_s = _out_ffn(xs, oa_s.reshape(n_seq * n_q, 512), u_s, v_s, ws_s, bs_s, woa, wob, g2, w1, w2, gf)

    keep = min(WINDOW, seq)
    new_kv_p = jnp.transpose(kvt_p.reshape(batch, 4, N_KVH, HEAD_DIM, seq), (0, 4, 1, 2, 3))
    new_win_p = jnp.transpose(wkvt_p[:, :, seq - keep:].reshape(batch, 2, N_KVH, HEAD_DIM, keep), (0, 4, 1, 2, 3))
    new_win_s = jnp.concatenate([state_win_kv[0], wkv_s.reshape(n_seq, n_q, 2, N_KVH, HEAD_DIM)], axis=1)[:, n_q:]
    return (y_p.reshape(batch, seq, D_MODEL),
            y_s.reshape(n_seq, n_q, D_MODEL),
            new_kv_p[None],
            kv_s.reshape(1, n_seq, n_q, 4, N_KVH, HEAD_DIM),
            new_win_p[None],
            new_win_s[None],
            v_s.reshape(1, n_seq, n_q, 512))
```
